```python
import jax, jax.numpy as jnp
from jax import lax
import numpy as np

D_MODEL = 1024
BATCH = 16
SEQ = 2048
DEPTH = 1

CHUNK = 64
MIX_WIDTH = 2 * D_MODEL
SSM_WIDTH = D_MODEL
SSM_HEAD_DIM = 64
SSM_HEADS = SSM_WIDTH // SSM_HEAD_DIM
SSM_GROUPS = 2
SSM_STATE = 128
CONV_WIDTH = 4
CONV_DIM = SSM_WIDTH + 2 * SSM_GROUPS * SSM_STATE
LSTM_WIDTH = MIX_WIDTH - SSM_WIDTH
LSTM_HEAD_DIM = 128
LSTM_HEADS = LSTM_WIDTH // LSTM_HEAD_DIM
N_EXPERTS = 32
TOP_K = 4
D_FF = D_MODEL
SWIGLU_LIMIT = 7.0
SWIGLU_ALPHA = 1.702
MOE_BLOCK = 256
RMS_EPS = 1e-6
IN_SIZES = (SSM_WIDTH, CONV_DIM, SSM_HEADS,
            LSTM_WIDTH, LSTM_WIDTH, LSTM_WIDTH, LSTM_WIDTH, LSTM_HEADS, LSTM_HEADS)
IN_PROJ_DIM = SSM_WIDTH + CONV_DIM + SSM_HEADS + 4 * LSTM_WIDTH + 2 * LSTM_HEADS

kernel_name = 'hybrid_ssd_mlstm_moe_block'


def _rms_norm(x, w):
    xf = x.astype(jnp.float32)
    y = xf * lax.rsqrt(jnp.mean(xf * xf, axis=-1, keepdims=True) + RMS_EPS)
    return (y * w.astype(jnp.float32)).astype(x.dtype)


def _segsum(a):
    t = a.shape[-1]
    cs = jnp.cumsum(a, axis=-1)
    diff = cs[..., :, None] - cs[..., None, :]
    mask = jnp.tril(jnp.ones((t, t), dtype=bool))
    return jnp.where(mask, diff, -jnp.inf)


def _causal_conv(u, w, b):
    ch = u.shape[-1]
    y = lax.conv_general_dilated(u, w[:, None, :], window_strides=(1,),
                                 padding=[(w.shape[0] - 1, 0)],
                                 dimension_numbers=('NWC', 'WIO', 'NWC'),
                                 feature_group_count=ch)
    return y + b


def _ssd(xs, dt, a_neg, bm, cm):
    b, s, h, p = xs.shape
    g, n = bm.shape[2], bm.shape[3]
    e, c, l = h // g, s // CHUNK, CHUNK
    xd = (xs * dt[..., None]).reshape(b, c, l, g, e, p)
    a = jnp.transpose((dt * a_neg).reshape(b, c, l, g, e), (0, 3, 4, 1, 2))
    bc = bm.reshape(b, c, l, g, n)
    cc = cm.reshape(b, c, l, g, n)
    a_cs = jnp.cumsum(a, axis=-1)
    decay_in = jnp.exp(_segsum(a))
    cb = jnp.einsum('bclgn,bcsgn->bgcls', cc, bc)
    y_diag = jnp.einsum('bgecls,bcsgep->bclgep', cb[:, :, None] * decay_in, xd)
    decay_to_end = jnp.exp(a_cs[..., -1:] - a_cs)
    states = jnp.einsum('bcsgn,bgecs,bcsgep->bcgepn', bc, decay_to_end, xd)
    states = jnp.concatenate([jnp.zeros_like(states[:, :1]), states], axis=1)
    chunk_decay = jnp.exp(_segsum(jnp.pad(a_cs[..., -1], ((0, 0), (0, 0), (0, 0), (1, 0)))))
    states = jnp.einsum('bgezc,bcgepn->bzgepn', chunk_decay, states)[:, :-1]
    y_off = jnp.einsum('bclgn,bcgepn,bgecl->bclgep', cc, states, jnp.exp(a_cs))
    return (y_diag + y_off).reshape(b, s, h, p)


def _mlstm(q, k, v, i_pre, f_pre):
    b, s, h, dh = q.shape
    c, l = s // CHUNK, CHUNK
    q = q.reshape(b, c, l, h, dh)
    k = k.reshape(b, c, l, h, dh) * (dh ** -0.5)
    v = v.reshape(b, c, l, h, dh)
    log_f = jnp.transpose(jax.nn.log_sigmoid(f_pre).reshape(b, c, l, h), (0, 3, 1, 2))
    log_i = jnp.transpose(i_pre.reshape(b, c, l, h), (0, 3, 1, 2))
    cum_f = jnp.cumsum(log_f, axis=-1)
    chunk_f = cum_f[..., -1]
    a_end = chunk_f[..., None] - cum_f + log_i
    m_loc = jnp.max(a_end, axis=-1)
    w_end = jnp.exp(a_end - m_loc[..., None])
    c_loc = jnp.einsum('bhcl,bclhk,bclhv->bhckv', w_end, k, v)
    n_loc = jnp.einsum('bhcl,bclhk->bhck', w_end, k)

    def step(carry, inp):
        c_st, n_st, m_st = carry
        f_c, m_c, c_c, n_c = inp
        m_new = jnp.maximum(f_c + m_st, m_c)
        s_old = jnp.exp(f_c + m_st - m_new)
        s_new = jnp.exp(m_c - m_new)
        c_next = s_old[..., None, None] * c_st + s_new[..., None, None] * c_c
        n_next = s_old[..., None] * n_st + s_new[..., None] * n_c
        return (c_next, n_next, m_new), (c_st, n_st, m_st)

    init = (jnp.zeros((b, h, dh, dh), q.dtype), jnp.zeros((b, h, dh), q.dtype),
            jnp.zeros((b, h), q.dtype))
    _, (c_prev, n_prev, m_prev) = lax.scan(
        step, init, (jnp.moveaxis(chunk_f, 2, 0), jnp.moveaxis(m_loc, 2, 0),
                     jnp.moveaxis(c_loc, 2, 0), jnp.moveaxis(n_loc, 2, 0)))
    c_prev = jnp.moveaxis(c_prev, 0, 2)
    n_prev = jnp.moveaxis(n_prev, 0, 2)
    m_prev = jnp.moveaxis(m_prev, 0, 2)
    causal = jnp.tril(jnp.ones((l, l), dtype=bool))
    log_d = jnp.where(causal, cum_f[..., :, None] - cum_f[..., None, :] + log_i[..., None, :],
                      -jnp.inf)
    log_inter = cum_f + m_prev[..., None]
    m_row = jnp.maximum(log_inter, jnp.max(log_d, axis=-1))
    sc = jnp.einsum('bclhk,bcjhk->bhclj', q, k) * jnp.exp(log_d - m_row[..., None])
    w_inter = jnp.exp(log_inter - m_row)
    num = (jnp.einsum('bhclj,bcjhv->bclhv', sc, v)
           + jnp.einsum('bclhk,bhckv->bclhv', q, c_prev) * jnp.transpose(w_inter, (0, 2, 3, 1))[..., None])
    den = jnp.sum(sc, axis=-1) + jnp.einsum('bclhk,bhck->bhcl', q, n_prev) * w_inter
    den = jnp.maximum(jnp.abs(den), jnp.exp(-m_row))
    hout = num / jnp.transpose(den, (0, 2, 3, 1))[..., None]
    return hout.reshape(b, s, h, dh)


def _moe(xn, w_router, b_router, w_gu, b_gu, w_dn, b_dn):
    bsz, s, d = xn.shape
    t = bsz * s
    xf = xn.reshape(t, d)
    logits = (xf @ w_router + b_router).astype(jnp.float32)
    top_val, top_idx = lax.top_k(logits, TOP_K)
    gates = jax.nn.softmax(top_val, axis=-1)
    n_assign = t * TOP_K
    n_blocks = -(-n_assign // MOE_BLOCK) + N_EXPERTS
    n_rows = n_blocks * MOE_BLOCK
    flat_e = top_idx.reshape(n_assign)
    flat_tok = jnp.arange(n_assign, dtype=jnp.int32) // TOP_K
    flat_gate = gates.reshape(n_assign)
    order = jnp.argsort(flat_e)
    sorted_e = flat_e[order]
    counts = jnp.bincount(flat_e, length=N_EXPERTS)
    starts = jnp.cumsum(counts) - counts
    padded = (counts + MOE_BLOCK - 1) // MOE_BLOCK * MOE_BLOCK
    padded_ends = jnp.cumsum(padded)
    padded_starts = padded_ends - padded
    dest = padded_starts[sorted_e] + jnp.arange(n_assign, dtype=jnp.int32) - starts[sorted_e]
    row_tok = jnp.zeros((n_rows,), jnp.int32).at[dest].set(flat_tok[order])
    row_gate = jnp.zeros((n_rows,), jnp.float32).at[dest].set(flat_gate[order])
    block_expert = jnp.minimum(
        jnp.searchsorted(padded_ends, jnp.arange(n_blocks, dtype=jnp.int32) * MOE_BLOCK, side='right'),
        N_EXPERTS - 1).astype(jnp.int32)

    def expert_block(args):
        tok, gate, e = args
        xb = xf[tok]
        hh = xb @ w_gu[e] + b_gu[e]
        g_h = jnp.minimum(hh[:, :D_FF], SWIGLU_LIMIT)
        u_h = jnp.clip(hh[:, D_FF:], -SWIGLU_LIMIT, SWIGLU_LIMIT)
        act = (u_h + 1.0) * (g_h * jax.nn.sigmoid(SWIGLU_ALPHA * g_h))
        out = act @ w_dn[e] + b_dn[e]
        return out * gate.astype(out.dtype)[:, None]

    outs = lax.map(expert_block, (row_tok.reshape(n_blocks, MOE_BLOCK),
                                  row_gate.reshape(n_blocks, MOE_BLOCK), block_expert))
    y = jax.ops.segment_sum(outs.reshape(n_rows, d), row_tok, num_segments=t)
    return y.reshape(bsz, s, d).astype(xn.dtype)


def setup_inputs(seed: int = 0) -> dict:
    key = jax.random.key(seed)
    ks = jax.random.split(key, 24)
    f32 = jnp.float32
    nrm = lambda k, shape, scale: jax.random.normal(k, shape, f32) * scale
    dt0 = jnp.exp(jax.random.uniform(ks[5], (DEPTH, SSM_HEADS), f32, np.log(1e-3), np.log(1e-1)))
    return {
        'x': jax.random.normal(ks[0], (BATCH, SEQ, D_MODEL), f32),
        'norm_mix_w': 1.0 + nrm(ks[1], (DEPTH, D_MODEL), 0.02),
        'w_in': nrm(ks[2], (DEPTH, D_MODEL, IN_PROJ_DIM), D_MODEL ** -0.5),
        'conv_w': nrm(ks[3], (DEPTH, CONV_WIDTH, CONV_DIM), CONV_WIDTH ** -0.5),
        'conv_b': nrm(ks[4], (DEPTH, CONV_DIM), 0.02),
        'dt_bias': dt0 + jnp.log(-jnp.expm1(-dt0)),
        'a_log': jnp.log(jax.random.uniform(ks[6], (DEPTH, SSM_HEADS), f32, 1.0, 16.0)),
        'd_skip': 1.0 + nrm(ks[7], (DEPTH, SSM_HEADS), 0.1),
        'ssm_norm_w': 1.0 + nrm(ks[8], (DEPTH, SSM_WIDTH), 0.02),
        'lstm_i_bias': nrm(ks[9], (DEPTH, LSTM_HEADS), 0.1),
        'lstm_f_bias': 3.0 + 3.0 * jax.random.uniform(ks[10], (DEPTH, LSTM_HEADS), f32),
        'lstm_norm_w': 1.0 + nrm(ks[11], (DEPTH, LSTM_WIDTH), 0.02),
        'w_out': nrm(ks[12], (DEPTH, MIX_WIDTH, D_MODEL), MIX_WIDTH ** -0.5),
        'norm_ffn_w': 1.0 + nrm(ks[13], (DEPTH, D_MODEL), 0.02),
        'w_router': nrm(ks[14], (DEPTH, D_MODEL, N_EXPERTS), D_MODEL ** -0.5),
        'b_router': nrm(ks[15], (DEPTH, N_EXPERTS), 0.01),
        'w_gate_up': nrm(ks[16], (DEPTH, N_EXPERTS, D_MODEL, 2 * D_FF), D_MODEL ** -0.5),
        'b_gate_up': nrm(ks[17], (DEPTH, N_EXPERTS, 2 * D_FF), 0.02),
        'w_down': nrm(ks[18], (DEPTH, N_EXPERTS, D_FF, D_MODEL), D_FF ** -0.5),
        'b_down': nrm(ks[19], (DEPTH, N_EXPERTS, D_MODEL), 0.02),
        'norm_final_w': 1.0 + nrm(ks[20], (D_MODEL,), 0.02),
    }


def reference(x, norm_mix_w, w_in, conv_w, conv_b, dt_bias, a_log, d_skip, ssm_norm_w,
              lstm_i_bias, lstm_f_bias, lstm_norm_w, w_out, norm_ffn_w, w_router, b_router,
              w_gate_up, b_gate_up, w_down, b_down, norm_final_w):
    f32 = jnp.float32
    b, s = x.shape[0], x.shape[1]
    splits = np.cumsum(IN_SIZES)[:-1].tolist()
    for layer in range(DEPTH):
        xn = _rms_norm(x, norm_mix_w[layer])
        proj = (xn @ w_in[layer]).astype(f32)
        z, xbc, dt_raw, q, k, v, o_pre, i_pre, f_pre = jnp.split(proj, splits, axis=-1)
        xbc = jax.nn.silu(_causal_conv(xbc, conv_w[layer].astype(f32), conv_b[layer].astype(f32)))
        xs, bm, cm = jnp.split(xbc, [SSM_WIDTH, SSM_WIDTH + SSM_GROUPS * SSM_STATE], axis=-1)
        xs = xs.reshape(b, s, SSM_HEADS, SSM_HEAD_DIM)
        bm = bm.reshape(b, s, SSM_GROUPS, SSM_STATE)
        cm = cm.reshape(b, s, SSM_GROUPS, SSM_STATE)
        dt = jax.nn.softplus(dt_raw + dt_bias[layer].astype(f32))
        a_neg = -jnp.exp(a_log[layer].astype(f32))
        y_ssm = _ssd(xs, dt, a_neg, bm, cm) + d_skip[layer].astype(f32)[:, None] * xs
        y_ssm = _rms_norm(y_ssm.reshape(b, s, SSM_WIDTH) * jax.nn.silu(z), ssm_norm_w[layer])
        h_l = _mlstm(q.reshape(b, s, LSTM_HEADS, LSTM_HEAD_DIM),
                     k.reshape(b, s, LSTM_HEADS, LSTM_HEAD_DIM),
                     v.reshape(b, s, LSTM_HEADS, LSTM_HEAD_DIM),
                     i_pre + lstm_i_bias[layer].astype(f32),
                     f_pre + lstm_f_bias[layer].astype(f32))
        h_l = _rms_norm(h_l, lstm_norm_w[layer].reshape(LSTM_HEADS, LSTM_HEAD_DIM))
        y_lstm = jax.nn.sigmoid(o_pre) * h_l.reshape(b, s, LSTM_WIDTH)
        mixed = jnp.concatenate([y_ssm, y_lstm], axis=-1).astype(x.dtype)
        x = x + mixed @ w_out[layer]
        x = x + _moe(_rms_norm(x, norm_ffn_w[layer]), w_router[layer], b_router[layer],
                     w_gate_up[layer], b_gate_up[layer], w_down[layer], b_down[layer])
    return _rms_norm(x, norm_final_w)
```

```python
import functools

import jax
import jax.numpy as jnp
import numpy as np
from jax import lax
from jax.experimental import pallas as pl
from jax.experimental.pallas import tpu as pltpu

F32 = jnp.float32
BF16 = jnp.bfloat16

D_MODEL = 1024
SSM_WIDTH = 1024
SSM_HEAD_DIM = 64
SSM_HEADS = 16
SSM_GROUPS = 2
SSM_STATE = 128
CONV_WIDTH = 4
CONV_DIM = SSM_WIDTH + 2 * SSM_GROUPS * SSM_STATE
LSTM_WIDTH = 1024
LSTM_HEAD_DIM = 128
LSTM_HEADS = 8
N_EXPERTS = 32
TOP_K = 4
D_FF = 1024
SWIGLU_LIMIT = 7.0
SWIGLU_ALPHA = 1.702
MOE_BLOCK = 256
RMS_EPS = 1e-6

LANES = 128
SUBLANES = 8
MIX_CHUNK = 256
ROW_TILE = 512
COMBINE_TILE = 256
VMEM_LIMIT = 56 * 1024 * 1024

GATE_COLS = 3 * LANES
NEG_INF = float("-inf")


def _cparams(sem):
    return pltpu.CompilerParams(dimension_semantics=sem, vmem_limit_bytes=VMEM_LIMIT)


def _rms(x, w):
    return x * lax.rsqrt(jnp.mean(x * x, axis=-1, keepdims=True) + RMS_EPS) * w


def _sigmoid(x):
    return 1.0 / (1.0 + jnp.exp(-x))


def _softplus(x):
    return jnp.maximum(x, 0.0) + jnp.log(1.0 + jnp.exp(-jnp.abs(x)))


def _split3(a):
    hi = a.astype(BF16)
    r = a - hi.astype(F32)
    mid = r.astype(BF16)
    lo = (r - mid.astype(F32)).astype(BF16)
    return hi, mid, lo


def _dot(a, b):
    return jnp.dot(a, b, preferred_element_type=F32)


def _dot_nt(a, b):
    return lax.dot_general(a, b, (((1,), (1,)), ((), ())), preferred_element_type=F32)


def _dot_tn(a, b):
    return lax.dot_general(a, b, (((0,), (0,)), ((), ())), preferred_element_type=F32)


def _sel_dot(sel_bf, a):
    hi, mid, lo = _split3(a)
    return _dot(sel_bf, hi) + _dot(sel_bf, mid) + _dot(sel_bf, lo)


def _expand(a, sel_bf):
    hi, mid, lo = _split3(a)
    return _dot(hi, sel_bf) + _dot(mid, sel_bf) + _dot(lo, sel_bf)


_INPROJ_WIDTHS = (SSM_WIDTH, CONV_DIM, LSTM_WIDTH, LSTM_WIDTH, LSTM_WIDTH, LSTM_WIDTH, GATE_COLS)


def _inproj_kernel(x_ref, nw_ref, w_ref, z_ref, xbc_ref, q_ref, k_ref, v_ref, o_ref, g_ref):
    xb = _rms(x_ref[...], nw_ref[...]).astype(BF16)
    off = 0
    for ref, width in zip((z_ref, xbc_ref, q_ref, k_ref, v_ref, o_ref, g_ref), _INPROJ_WIDTHS):
        ref[...] = _dot(xb, w_ref[:, off:off + width]).astype(ref.dtype)
        off += width


def _inproj(x2, nw, w_all):
    t = x2.shape[0]
    tm = min(ROW_TILE, t)
    ncol = w_all.shape[1]
    out_shape = [jax.ShapeDtypeStruct((t, w), BF16) for w in _INPROJ_WIDTHS[:-1]]
    out_shape.append(jax.ShapeDtypeStruct((t, GATE_COLS), F32))
    return pl.pallas_call(
        _inproj_kernel,
        grid=(t // tm,),
        in_specs=[
            pl.BlockSpec((tm, D_MODEL), lambda i: (i, 0)),
            pl.BlockSpec((1, D_MODEL), lambda i: (0, 0)),
            pl.BlockSpec((D_MODEL, ncol), lambda i: (0, 0), pipeline_mode=pl.Buffered(1)),
        ],
        out_specs=[pl.BlockSpec((tm, w), lambda i: (i, 0)) for w in _INPROJ_WIDTHS],
        out_shape=out_shape,
        compiler_params=_cparams(("arbitrary",)),
        name="inproj",
    )(x2, nw, w_all)


def _ssd_kernel(xbc_ref, z_ref, g_ref, convw_ref, convb_ref, dtb_ref, aneg_ref, dskip_ref, nw_ref,
                sel_ref, y_ref, u_scr, st_scr):
    L = xbc_ref.shape[0]
    gw = SSM_WIDTH // SSM_GROUPS

    @pl.when(pl.program_id(1) == 0)
    def _():
        u_scr[0:SUBLANES, :] = jnp.zeros((SUBLANES, CONV_DIM), F32)
        st_scr[...] = jnp.zeros_like(st_scr)

    u_scr[SUBLANES:SUBLANES + L, :] = xbc_ref[...].astype(F32)
    acc = jnp.broadcast_to(convb_ref[...], (L, CONV_DIM))
    for j in range(CONV_WIDTH):
        back = CONV_WIDTH - 1 - j
        acc = acc + convw_ref[j:j + 1, :] * u_scr[SUBLANES - back:SUBLANES - back + L, :]
    u_scr[0:SUBLANES, :] = u_scr[L:L + SUBLANES, :]
    xbc = acc * _sigmoid(acc)
    xs = xbc[:, :SSM_WIDTH]
    xs_bf = xs.astype(BF16)
    bm = xbc[:, SSM_WIDTH:SSM_WIDTH + SSM_GROUPS * SSM_STATE].astype(BF16)
    cm = xbc[:, SSM_WIDTH + SSM_GROUPS * SSM_STATE:].astype(BF16)

    row = lax.broadcasted_iota(jnp.int32, (L, L), 0)
    col = lax.broadcasted_iota(jnp.int32, (L, L), 1)
    causal = row >= col
    tril_bf = causal.astype(F32).astype(BF16)

    lane = lax.broadcasted_iota(jnp.int32, (1, LANES), 1)
    dt = jnp.where(lane < SSM_HEADS, _softplus(g_ref[...] + dtb_ref[...]), 0.0)
    acs = _sel_dot(tril_bf, dt * aneg_ref[...])
    acs_t = acs.T
    dt_t = dt.T
    sel = sel_ref[...]
    exp_acs = jnp.exp(acs)
    acs_last = acs[L - 1:L, :]
    dt_x = _expand(dt, sel)
    ea_x = _expand(exp_acs, sel)
    de_x = _expand(jnp.exp(acs_last - acs), sel)
    xw = (xs * dt_x * de_x).astype(BF16)

    pair_lane = lax.broadcasted_iota(jnp.int32, (1, LANES), 1)
    ydiag = []
    for j in range(SSM_HEADS // 2):
        g = (2 * j) // (SSM_HEADS // SSM_GROUPS)
        cb = _dot_nt(cm[:, g * SSM_STATE:(g + 1) * SSM_STATE], bm[:, g * SSM_STATE:(g + 1) * SSM_STATE])
        ms = []
        for h in (2 * j, 2 * j + 1):
            seg = acs[:, h:h + 1] - acs_t[h:h + 1, :]
            dec = jnp.exp(jnp.where(causal, seg, NEG_INF))
            ms.append((cb * dec * dt_t[h:h + 1, :]).astype(BF16))
        xpair = xs_bf[:, j * LANES:(j + 1) * LANES]
        zero = jnp.zeros_like(xpair)
        rhs = jnp.concatenate([jnp.where(pair_lane < SSM_HEAD_DIM, xpair, zero),
                               jnp.where(pair_lane >= SSM_HEAD_DIM, xpair, zero)], axis=0)
        ydiag.append(_dot(jnp.concatenate(ms, axis=1), rhs))
    y = jnp.concatenate(ydiag, axis=1)

    yoff = []
    for g in range(SSM_GROUPS):
        st = st_scr[g]
        yoff.append(_dot(cm[:, g * SSM_STATE:(g + 1) * SSM_STATE], st.astype(BF16)))
        upd = _dot_tn(bm[:, g * SSM_STATE:(g + 1) * SSM_STATE], xw[:, g * gw:(g + 1) * gw])
        st_scr[g] = st * ea_x[L - 1:L, g * gw:(g + 1) * gw] + upd
    y = y + jnp.concatenate(yoff, axis=1) * ea_x + dskip_ref[...] * xs

    zz = z_ref[...].astype(F32)
    y = y * (zz * _sigmoid(zz))
    y_ref[...] = _rms(y, nw_ref[...]).astype(BF16)


def _ssd(xbc, z, gates, convw, convb, dtb, aneg, dskip_x, nw, sel, b, s):
    L = min(MIX_CHUNK, s)
    nc = s // L
    tok = lambda bi, ci: (bi * nc + ci, 0)
    const = lambda bi, ci: (0, 0)
    return pl.pallas_call(
        _ssd_kernel,
        grid=(b, nc),
        in_specs=[
            pl.BlockSpec((L, CONV_DIM), tok),
            pl.BlockSpec((L, SSM_WIDTH), tok),
            pl.BlockSpec((L, LANES), tok),
            pl.BlockSpec((CONV_WIDTH, CONV_DIM), const),
            pl.BlockSpec((1, CONV_DIM), const),
            pl.BlockSpec((1, LANES), const),
            pl.BlockSpec((1, LANES), const),
            pl.BlockSpec((1, SSM_WIDTH), const),
            pl.BlockSpec((1, SSM_WIDTH), const),
            pl.BlockSpec((LANES, SSM_WIDTH), const),
        ],
        out_specs=pl.BlockSpec((L, SSM_WIDTH), tok),
        out_shape=jax.ShapeDtypeStruct((b * s, SSM_WIDTH), BF16),
        scratch_shapes=[pltpu.VMEM((L + 2 * SUBLANES, CONV_DIM), F32),
                        pltpu.VMEM((SSM_GROUPS, SSM_STATE, SSM_WIDTH // SSM_GROUPS), F32)],
        compiler_params=_cparams(("arbitrary", "arbitrary")),
        name="ssd",
    )(xbc, z, gates, convw, convb, dtb, aneg, dskip_x, nw, sel)


def _mlstm_kernel(q_ref, k_ref, v_ref, o_ref, gi_ref, gf_ref, ib_ref, fb_ref, nw_ref, y_ref,
                  st_scr, m_scr):
    L = q_ref.shape[0]
    dh = LSTM_HEAD_DIM
    scale = dh ** -0.5

    @pl.when(pl.program_id(1) == 0)
    def _():
        st_scr[...] = jnp.zeros_like(st_scr)
        m_scr[...] = jnp.zeros_like(m_scr)

    row = lax.broadcasted_iota(jnp.int32, (L, L), 0)
    col = lax.broadcasted_iota(jnp.int32, (L, L), 1)
    causal = row >= col
    tril_bf = causal.astype(F32).astype(BF16)
    lane = lax.broadcasted_iota(jnp.int32, (1, LANES), 1)
    live = lane < LSTM_HEADS

    ii = jnp.where(live, gi_ref[...] + ib_ref[...], 0.0)
    logf = jnp.where(live, -_softplus(-(gf_ref[...] + fb_ref[...])), 0.0)
    cumf = _sel_dot(tril_bf, logf)
    g = ii - cumf
    rid = lax.broadcasted_iota(jnp.int32, (L, LANES), 0)
    cmx = g
    step = 1
    while step < L:
        cmx = jnp.maximum(cmx, jnp.where(rid >= step, pltpu.roll(cmx, step, axis=0), NEG_INF))
        step *= 2
    m_prev = m_scr[...]
    mx = jnp.maximum(m_prev, cmx)
    w_inter = jnp.exp(m_prev - mx)
    enm = jnp.exp(-(cumf + mx))
    g_t = g.T
    m_last = mx[L - 1:L, :]
    wk = jnp.exp(g - m_last) * scale
    sc = jnp.exp(m_prev - m_last)
    m_scr[...] = cumf[L - 1:L, :] + m_last

    ones_bf = jnp.ones((L, dh), BF16)
    for h in range(LSTM_HEADS):
        hs = slice(h * dh, (h + 1) * dh)
        qh = q_ref[:, hs]
        kh = k_ref[:, hs]
        vaug = jnp.concatenate([v_ref[:, hs], ones_bf], axis=1)
        dm = jnp.exp(jnp.where(causal, g_t[h:h + 1, :] - mx[:, h:h + 1], NEG_INF))
        p = (_dot_nt(qh, kh) * scale * dm).astype(BF16)
        r1 = _dot(p, vaug)
        st = st_scr[h]
        r2 = _dot(qh, st.astype(BF16))
        wcol = w_inter[:, h:h + 1]
        num = r1[:, :dh] + r2[:, :dh] * wcol
        den = r1[:, dh:] + r2[:, dh:] * wcol
        den = jnp.maximum(jnp.abs(den), enm[:, h:h + 1])
        hh = _rms(num / den, nw_ref[:, hs])
        oo = o_ref[:, hs].astype(F32)
        y_ref[:, hs] = (_sigmoid(oo) * hh).astype(BF16)
        kw = (kh.astype(F32) * wk[:, h:h + 1]).astype(BF16)
        st_scr[h] = st * sc[:, h:h + 1] + _dot_tn(kw, vaug)


def _mlstm(q, k, v, o, gates, ib, fb, nw, b, s):
    L = min(MIX_CHUNK, s)
    nc = s // L
    tok = lambda bi, ci: (bi * nc + ci, 0)
    const = lambda bi, ci: (0, 0)
    big = pl.BlockSpec((L, LSTM_WIDTH), tok)
    return pl.pallas_call(
        _mlstm_kernel,
        grid=(b, nc),
        in_specs=[big, big, big, big,
                  pl.BlockSpec((L, LANES), lambda bi, ci: (bi * nc + ci, 1)),
                  pl.BlockSpec((L, LANES), lambda bi, ci: (bi * nc + ci, 2)),
                  pl.BlockSpec((1, LANES), const),
                  pl.BlockSpec((1, LANES), const),
                  pl.BlockSpec((1, LSTM_WIDTH), const)],
        out_specs=big,
        out_shape=jax.ShapeDtypeStruct((b * s, LSTM_WIDTH), BF16),
        scratch_shapes=[pltpu.VMEM((LSTM_HEADS, LSTM_HEAD_DIM, 2 * LSTM_HEAD_DIM), F32),
                        pltpu.VMEM((1, LANES), F32)],
        compiler_params=_cparams(("arbitrary", "arbitrary")),
        name="mlstm",
    )(q, k, v, o, gates, gates, ib, fb, nw)


META_IDX, META_GATE, META_RANK = 0, TOP_K, 2 * TOP_K


def _outproj_kernel(ys_ref, yl_ref, x_ref, wo_ref, nw_ref, wr_ref, br_ref, h_ref, hn_ref, meta_ref,
                    cnt_ref, cnt_scr):
    tm = x_ref.shape[0]

    @pl.when(pl.program_id(0) == 0)
    def _():
        cnt_scr[...] = jnp.zeros_like(cnt_scr)

    h = (x_ref[...] + _dot(ys_ref[...], wo_ref[:SSM_WIDTH, :]) + _dot(yl_ref[...], wo_ref[SSM_WIDTH:, :]))
    h_ref[...] = h
    hn = _rms(h, nw_ref[...])
    hn_ref[...] = hn
    vals = _dot(hn.astype(BF16), wr_ref[...]) + br_ref[...]

    lane = lax.broadcasted_iota(jnp.int32, (tm, LANES), 1)
    member = jnp.zeros((tm, LANES), F32)
    tops, idxs, sels = [], [], []
    for _ in range(TOP_K):
        m = jnp.max(vals, axis=-1, keepdims=True)
        idx = jnp.min(jnp.where(vals == m, lane, LANES), axis=-1, keepdims=True)
        sel = lane == idx
        vals = jnp.where(sel, NEG_INF, vals)
        member = member + sel.astype(F32)
        tops.append(m)
        idxs.append(idx)
        sels.append(sel)
    es = [jnp.exp(t - tops[0]) for t in tops]
    inv = 1.0 / (es[0] + es[1] + es[2] + es[3])

    r = lax.broadcasted_iota(jnp.int32, (tm, tm), 0)
    c = lax.broadcasted_iota(jnp.int32, (tm, tm), 1)
    strict = (r > c).astype(F32).astype(BF16)
    carry = cnt_scr[0:1, :]
    rank_all = _dot(strict, member.astype(BF16)) + carry
    total = carry + jnp.sum(member, axis=0, keepdims=True)
    cnt_scr[...] = jnp.broadcast_to(total, cnt_scr.shape)
    cnt_ref[...] = jnp.broadcast_to(total, cnt_ref.shape)

    meta = jnp.zeros((tm, LANES), F32)
    for kk in range(TOP_K):
        rank = jnp.sum(jnp.where(sels[kk], rank_all, 0.0), axis=-1, keepdims=True)
        meta = jnp.where(lane == META_IDX + kk, idxs[kk].astype(F32), meta)
        meta = jnp.where(lane == META_GATE + kk, es[kk] * inv, meta)
        meta = jnp.where(lane == META_RANK + kk, rank, meta)
    meta_ref[...] = meta


def _outproj(ys, yl, x2, wo, nw, wr, br):
    t = x2.shape[0]
    tm = min(ROW_TILE, t)
    tokspec = lambda w: pl.BlockSpec((tm, w), lambda i: (i, 0))
    const = lambda i: (0, 0)
    return pl.pallas_call(
        _outproj_kernel,
        grid=(t // tm,),
        in_specs=[tokspec(SSM_WIDTH), tokspec(LSTM_WIDTH), tokspec(D_MODEL),
                  pl.BlockSpec((SSM_WIDTH + LSTM_WIDTH, D_MODEL), const, pipeline_mode=pl.Buffered(1)),
                  pl.BlockSpec((1, D_MODEL), const),
                  pl.BlockSpec((D_MODEL, LANES), const),
                  pl.BlockSpec((1, LANES), const)],
        out_specs=[tokspec(D_MODEL), tokspec(D_MODEL), tokspec(LANES),
                   pl.BlockSpec((SUBLANES, LANES), const)],
        out_shape=[jax.ShapeDtypeStruct((t, D_MODEL), F32), jax.ShapeDtypeStruct((t, D_MODEL), F32),
                   jax.ShapeDtypeStruct((t, LANES), F32), jax.ShapeDtypeStruct((SUBLANES, LANES), F32)],
        scratch_shapes=[pltpu.VMEM((SUBLANES, LANES), F32)],
        compiler_params=_cparams(("arbitrary",)),
        name="outproj_router",
    )(ys, yl, x2, wo, nw, wr, br)


def _gather_rows(idx_ref, base, n, src_hbm, dst, sem):
    def body(r, carry):
        pltpu.make_async_copy(src_hbm.at[pl.ds(idx_ref[base + r], 1), :], dst.at[pl.ds(r, 1), :], sem).start()
        return carry
    lax.fori_loop(0, n, body, 0, unroll=8)


def _wait_rows(src_hbm, dst, sem):
    pltpu.make_async_copy(src_hbm.at[pl.ds(0, dst.shape[0]), :], dst, sem).wait()


def _experts_kernel(be_ref, nused_ref, tok_cur, tok_next, hn_hbm, wgu_ref, bgu_ref, wdn_ref, bdn_ref,
                    out_ref, xbuf, sem, wgu_bf, wdn_bf):
    i = pl.program_id(0)
    n_used = nused_ref[0]
    slot = i % 2

    @pl.when(i == 0)
    def _():
        _gather_rows(tok_cur.at[0, 0], 0, MOE_BLOCK, hn_hbm, xbuf.at[0], sem.at[0])

    @pl.when(i + 1 < n_used)
    def _():
        _gather_rows(tok_next.at[0, 0], 0, MOE_BLOCK, hn_hbm, xbuf.at[1 - slot], sem.at[1 - slot])

    @pl.when(i < n_used)
    def _():
        @pl.when(jnp.logical_or(i == 0, be_ref[i] != be_ref[jnp.maximum(i - 1, 0)]))
        def _():
            wgu_bf[...] = wgu_ref[...].astype(BF16)
            wdn_bf[...] = wdn_ref[...].astype(BF16)

        _wait_rows(hn_hbm, xbuf.at[slot], sem.at[slot])
        xb = xbuf[slot].astype(BF16)
        hh = _dot(xb, wgu_bf[...]) + bgu_ref[...]
        gh = jnp.minimum(hh[:, :D_FF], SWIGLU_LIMIT)
        uh = jnp.clip(hh[:, D_FF:], -SWIGLU_LIMIT, SWIGLU_LIMIT)
        act = (uh + 1.0) * (gh * _sigmoid(SWIGLU_ALPHA * gh))
        out_ref[...] = _dot(act.astype(BF16), wdn_bf[...]) + bdn_ref[...]

    @pl.when(i >= n_used)
    def _():
        out_ref[...] = jnp.zeros_like(out_ref)


def _experts(block_expert, n_used, row_tok3, hn, wgu, bgu3, wdn, bdn3):
    n_blocks = block_expert.shape[0]
    last = n_blocks - 1
    grid_spec = pltpu.PrefetchScalarGridSpec(
        num_scalar_prefetch=2,
        grid=(n_blocks,),
        in_specs=[
            pl.BlockSpec((1, 1, MOE_BLOCK), lambda i, be, nu: (i, 0, 0), memory_space=pltpu.SMEM),
            pl.BlockSpec((1, 1, MOE_BLOCK), lambda i, be, nu: (jnp.minimum(i + 1, last), 0, 0),
                         memory_space=pltpu.SMEM),
            pl.BlockSpec(memory_space=pl.ANY),
            pl.BlockSpec((None, D_MODEL, 2 * D_FF), lambda i, be, nu: (be[i], 0, 0)),
            pl.BlockSpec((None, 1, 2 * D_FF), lambda i, be, nu: (be[i], 0, 0)),
            pl.BlockSpec((None, D_FF, D_MODEL), lambda i, be, nu: (be[i], 0, 0)),
            pl.BlockSpec((None, 1, D_MODEL), lambda i, be, nu: (be[i], 0, 0)),
        ],
        out_specs=pl.BlockSpec((MOE_BLOCK, D_MODEL), lambda i, be, nu: (i, 0)),
        scratch_shapes=[pltpu.VMEM((2, MOE_BLOCK, D_MODEL), F32),
                        pltpu.SemaphoreType.DMA((2,)),
                        pltpu.VMEM((D_MODEL, 2 * D_FF), BF16),
                        pltpu.VMEM((D_FF, D_MODEL), BF16)],
    )
    return pl.pallas_call(
        _experts_kernel,
        grid_spec=grid_spec,
        out_shape=jax.ShapeDtypeStruct((n_blocks * MOE_BLOCK, D_MODEL), F32),
        compiler_params=_cparams(("arbitrary",)),
        name="experts",
    )(block_expert, n_used, row_tok3, row_tok3, hn, wgu, bgu3, wdn, bdn3)


def _combine_kernel(pos_cur, pos_next, outs_hbm, h_ref, meta_ref, nw_ref, y_ref, buf, sem):
    i = pl.program_id(0)
    n = pl.num_programs(0)
    tc = h_ref.shape[0]
    slot = i % 2

    def issue(pos_ref, s):
        for kk in range(TOP_K):
            _gather_rows(pos_ref.at[0, 0], kk * tc, tc, outs_hbm, buf.at[s, kk], sem.at[s])

    @pl.when(i == 0)
    def _():
        issue(pos_cur, 0)

    @pl.when(i + 1 < n)
    def _():
        issue(pos_next, 1 - slot)

    for kk in range(TOP_K):
        _wait_rows(outs_hbm, buf.at[slot, kk], sem.at[slot])
    y = h_ref[...]
    meta = meta_ref[...]
    for kk in range(TOP_K):
        y = y + meta[:, META_GATE + kk:META_GATE + kk + 1] * buf[slot, kk]
    y_ref[...] = _rms(y, nw_ref[...])


def _combine(pos3, outs, h, meta, nw):
    t = h.shape[0]
    tc = min(COMBINE_TILE, t)
    n = t // tc
    return pl.pallas_call(
        _combine_kernel,
        grid=(n,),
        in_specs=[
            pl.BlockSpec((1, 1, TOP_K * tc), lambda i: (i, 0, 0), memory_space=pltpu.SMEM),
            pl.BlockSpec((1, 1, TOP_K * tc), lambda i: (jnp.minimum(i + 1, n - 1), 0, 0),
                         memory_space=pltpu.SMEM),
            pl.BlockSpec(memory_space=pl.ANY),
            pl.BlockSpec((tc, D_MODEL), lambda i: (i, 0)),
            pl.BlockSpec((tc, LANES), lambda i: (i, 0)),
            pl.BlockSpec((1, D_MODEL), lambda i: (0, 0)),
        ],
        out_specs=pl.BlockSpec((tc, D_MODEL), lambda i: (i, 0)),
        out_shape=jax.ShapeDtypeStruct((t, D_MODEL), F32),
        scratch_shapes=[pltpu.VMEM((2, TOP_K, tc, D_MODEL), F32), pltpu.SemaphoreType.DMA((2,))],
        compiler_params=_cparams(("arbitrary",)),
        name="combine",
    )(pos3, pos3, outs, h, meta, nw)


def _pad_lanes(v, fill=0.0):
    v = v.astype(F32).reshape(1, -1)
    return jnp.pad(v, ((0, 0), (0, LANES - v.shape[1])), constant_values=fill)


def kernel(x, norm_mix_w, w_in, conv_w, conv_b, dt_bias, a_log, d_skip, ssm_norm_w, lstm_i_bias,
           lstm_f_bias, lstm_norm_w, w_out, norm_ffn_w, w_router, b_router, w_gate_up, b_gate_up,
           w_down, b_down, norm_final_w):
    b, s, d = x.shape
    t = b * s
    x2 = x.reshape(t, d).astype(F32)
    depth = w_in.shape[0]
    assert depth == 1, "the combine kernel fuses the final norm, so exactly one layer is supported"
    for layer in range(depth):
        wi = w_in[layer]
        c0 = SSM_WIDTH
        c1 = c0 + CONV_DIM
        c2 = c1 + SSM_HEADS
        c3 = c2 + 4 * LSTM_WIDTH
        c4 = c3 + LSTM_HEADS
        padc = lambda w: jnp.pad(w, ((0, 0), (0, LANES - w.shape[1])))
        w_all = jnp.concatenate([wi[:, :c1], wi[:, c2:c3], padc(wi[:, c1:c2]), padc(wi[:, c3:c4]),
                                 padc(wi[:, c4:])], axis=1).astype(BF16)
        sel = (jnp.arange(LANES)[:, None] == (jnp.arange(SSM_WIDTH) // SSM_HEAD_DIM)[None, :]).astype(BF16)
        a_neg = _pad_lanes(-jnp.exp(a_log[layer].astype(F32)))
        dskip_x = jnp.repeat(d_skip[layer].astype(F32), SSM_HEAD_DIM).reshape(1, SSM_WIDTH)
        wr = jnp.pad(w_router[layer], ((0, 0), (0, LANES - N_EXPERTS))).astype(BF16)
        br = _pad_lanes(b_router[layer], fill=NEG_INF)

        z, xbc, q, k, v, o, gates = _inproj(x2, norm_mix_w[layer].reshape(1, d).astype(F32), w_all)
        y_ssd = _ssd(xbc, z, gates, conv_w[layer].astype(F32), conv_b[layer].reshape(1, -1).astype(F32),
                     _pad_lanes(dt_bias[layer]), a_neg, dskip_x,
                     ssm_norm_w[layer].reshape(1, -1).astype(F32), sel, b, s)
        y_lstm = _mlstm(q, k, v, o, gates, _pad_lanes(lstm_i_bias[layer]), _pad_lanes(lstm_f_bias[layer]),
                        lstm_norm_w[layer].reshape(1, -1).astype(F32), b, s)
        h, hn, meta, cnt = _outproj(y_ssd, y_lstm, x2, w_out[layer].astype(BF16),
                                    norm_ffn_w[layer].reshape(1, d).astype(F32), wr, br)

        idx = meta[:, META_IDX:META_IDX + TOP_K].astype(jnp.int32)
        rank = meta[:, META_RANK:META_RANK + TOP_K].astype(jnp.int32)
        counts = cnt[0, :N_EXPERTS].astype(jnp.int32)
        n_assign = t * TOP_K
        n_blocks = -(-n_assign // MOE_BLOCK) + N_EXPERTS
        padded = (counts + MOE_BLOCK - 1) // MOE_BLOCK * MOE_BLOCK
        padded_ends = jnp.cumsum(padded)
        padded_starts = padded_ends - padded
        pos = padded_starts[idx] + rank
        block_expert = jnp.minimum(
            jnp.searchsorted(padded_ends, jnp.arange(n_blocks, dtype=jnp.int32) * MOE_BLOCK, side='right'),
            N_EXPERTS - 1).astype(jnp.int32)
        n_used = (padded_ends[-1:] // MOE_BLOCK).astype(jnp.int32)
        tok_of = jnp.broadcast_to(jnp.arange(t, dtype=jnp.int32)[:, None], (t, TOP_K))
        row_tok = jnp.zeros((n_blocks * MOE_BLOCK,), jnp.int32).at[pos.reshape(-1)].set(tok_of.reshape(-1))

        outs = _experts(block_expert, n_used, row_tok.reshape(n_blocks, 1, MOE_BLOCK), hn,
                        w_gate_up[layer], b_gate_up[layer].reshape(N_EXPERTS, 1, -1),
                        w_down[layer], b_down[layer].reshape(N_EXPERTS, 1, -1))
        tc = min(COMBINE_TILE, t)
        pos3 = pos.reshape(t // tc, tc, TOP_K).transpose(0, 2, 1).reshape(t // tc, 1, TOP_K * tc)
        x2 = _combine(pos3, outs, h, meta, norm_final_w.reshape(1, d).astype(F32))
    return x2.reshape(b, s, d).astype(x.dtype)
```

```python
import functools

import jax
import jax.numpy as jnp
import numpy as np
from jax import lax
from jax.experimental import pallas as pl
from jax.experimental.pallas import tpu as pltpu

F32 = jnp.float32
BF16 = jnp.bfloat16

D_MODEL = 1024
SSM_WIDTH = 1024
SSM_HEAD_DIM = 64
SSM_HEADS = 16
SSM_GROUPS = 2
SSM_STATE = 128
CONV_WIDTH = 4
CONV_DIM = SSM_WIDTH + 2 * SSM_GROUPS * SSM_STATE
LSTM_WIDTH = 1024
LSTM_HEAD_DIM = 128
LSTM_HEADS = 8
N_EXPERTS = 32
TOP_K = 4
D_FF = 1024
SWIGLU_LIMIT = 7.0
SWIGLU_ALPHA = 1.702
MOE_BLOCK = 256
RMS_EPS = 1e-6

LANES = 128
SUBLANES = 8
MIX_CHUNK = 256
ROW_TILE = 512
COMBINE_TILE = 256
VMEM_LIMIT = 56 * 1024 * 1024

GATE_COLS = 3 * LANES
NEG_INF = float("-inf")


def _cparams(sem):
    return pltpu.CompilerParams(dimension_semantics=sem, vmem_limit_bytes=VMEM_LIMIT)


def _rms(x, w):
    return x * lax.rsqrt(jnp.mean(x * x, axis=-1, keepdims=True) + RMS_EPS) * w


def _sigmoid(x):
    return 1.0 / (1.0 + jnp.exp(-x))


def _softplus(x):
    return jnp.maximum(x, 0.0) + jnp.log(1.0 + jnp.exp(-jnp.abs(x)))


def _split3(a):
    hi = a.astype(BF16)
    r = a - hi.astype(F32)
    mid = r.astype(BF16)
    lo = (r - mid.astype(F32)).astype(BF16)
    return hi, mid, lo


def _dot(a, b):
    return jnp.dot(a, b, preferred_element_type=F32)


def _dot_nt(a, b):
    return lax.dot_general(a, b, (((1,), (1,)), ((), ())), preferred_element_type=F32)


def _dot_tn(a, b):
    return lax.dot_general(a, b, (((0,), (0,)), ((), ())), preferred_element_type=F32)


def _sel_dot(sel_bf, a):
    hi, mid, lo = _split3(a)
    return _dot(sel_bf, hi) + _dot(sel_bf, mid) + _dot(sel_bf, lo)


def _expand(a, sel_bf):
    hi, mid, lo = _split3(a)
    return _dot(hi, sel_bf) + _dot(mid, sel_bf) + _dot(lo, sel_bf)


ROW_SUBTILES = D_MODEL // LANES


def _store_row_tiles(ref, x):
    n = x.shape[0]
    for s in range(ROW_SUBTILES):
        ref[pl.ds(s, n, stride=ROW_SUBTILES), :] = x[:, s * LANES:(s + 1) * LANES]


def _load_row_tile_cols(ref, n):
    return [ref[pl.ds(s, n, stride=ROW_SUBTILES), :] for s in range(ROW_SUBTILES)]


def _row_copy(src, src_row, dst, dst_row, sem):
    return pltpu.make_async_copy(
        src.at[pl.ds(pl.multiple_of(src_row * ROW_SUBTILES, ROW_SUBTILES), ROW_SUBTILES), :],
        dst.at[pl.ds(pl.multiple_of(dst_row * ROW_SUBTILES, ROW_SUBTILES), ROW_SUBTILES), :], sem)


def _wait_row_copies(src, dst, n_rows, sem):
    size = n_rows * ROW_SUBTILES
    pltpu.make_async_copy(src.at[pl.ds(0, size), :], dst.at[pl.ds(0, size), :], sem).wait()


_INPROJ_WIDTHS = (SSM_WIDTH, CONV_DIM, LSTM_WIDTH, LSTM_WIDTH, LSTM_WIDTH, LSTM_WIDTH, GATE_COLS)


def _inproj_kernel(x_ref, nw_ref, w_ref, z_ref, xbc_ref, q_ref, k_ref, v_ref, o_ref, g_ref):
    xb = _rms(x_ref[...], nw_ref[...]).astype(BF16)
    off = 0
    for ref, width in zip((z_ref, xbc_ref, q_ref, k_ref, v_ref, o_ref, g_ref), _INPROJ_WIDTHS):
        ref[...] = _dot(xb, w_ref[:, off:off + width]).astype(ref.dtype)
        off += width


def _inproj(x2, nw, w_all):
    t = x2.shape[0]
    tm = min(ROW_TILE, t)
    ncol = w_all.shape[1]
    out_shape = [jax.ShapeDtypeStruct((t, w), BF16) for w in _INPROJ_WIDTHS[:-1]]
    out_shape.append(jax.ShapeDtypeStruct((t, GATE_COLS), F32))
    return pl.pallas_call(
        _inproj_kernel,
        grid=(t // tm,),
        in_specs=[
            pl.BlockSpec((tm, D_MODEL), lambda i: (i, 0)),
            pl.BlockSpec((1, D_MODEL), lambda i: (0, 0)),
            pl.BlockSpec((D_MODEL, ncol), lambda i: (0, 0), pipeline_mode=pl.Buffered(1)),
        ],
        out_specs=[pl.BlockSpec((tm, w), lambda i: (i, 0)) for w in _INPROJ_WIDTHS],
        out_shape=out_shape,
        compiler_params=_cparams(("arbitrary",)),
        name="inproj",
    )(x2, nw, w_all)


def _ssd_kernel(xbc_ref, z_ref, g_ref, convw_ref, convb_ref, dtb_ref, aneg_ref, dskip_ref, nw_ref,
                sel_ref, y_ref, u_scr, st_scr):
    L = xbc_ref.shape[0]
    gw = SSM_WIDTH // SSM_GROUPS

    @pl.when(pl.program_id(1) == 0)
    def _():
        u_scr[0:SUBLANES, :] = jnp.zeros((SUBLANES, CONV_DIM), F32)
        st_scr[...] = jnp.zeros_like(st_scr)

    u_scr[SUBLANES:SUBLANES + L, :] = xbc_ref[...].astype(F32)
    acc = jnp.broadcast_to(convb_ref[...], (L, CONV_DIM))
    for j in range(CONV_WIDTH):
        back = CONV_WIDTH - 1 - j
        acc = acc + convw_ref[j:j + 1, :] * u_scr[SUBLANES - back:SUBLANES - back + L, :]
    u_scr[0:SUBLANES, :] = u_scr[L:L + SUBLANES, :]
    xbc = acc * _sigmoid(acc)
    xs = xbc[:, :SSM_WIDTH]
    xs_bf = xs.astype(BF16)
    bm = xbc[:, SSM_WIDTH:SSM_WIDTH + SSM_GROUPS * SSM_STATE].astype(BF16)
    cm = xbc[:, SSM_WIDTH + SSM_GROUPS * SSM_STATE:].astype(BF16)

    row = lax.broadcasted_iota(jnp.int32, (L, L), 0)
    col = lax.broadcasted_iota(jnp.int32, (L, L), 1)
    causal = row >= col
    tril_bf = causal.astype(F32).astype(BF16)

    lane = lax.broadcasted_iota(jnp.int32, (1, LANES), 1)
    dt = jnp.where(lane < SSM_HEADS, _softplus(g_ref[...] + dtb_ref[...]), 0.0)
    acs = _sel_dot(tril_bf, dt * aneg_ref[...])
    acs_t = acs.T
    dt_t = dt.T
    sel = sel_ref[...]
    exp_acs = jnp.exp(acs)
    acs_last = acs[L - 1:L, :]
    dt_x = _expand(dt, sel)
    ea_x = _expand(exp_acs, sel)
    de_x = _expand(jnp.exp(acs_last - acs), sel)
    xw = (xs * dt_x * de_x).astype(BF16)

    pair_lane = lax.broadcasted_iota(jnp.int32, (1, LANES), 1)
    ydiag = []
    for j in range(SSM_HEADS // 2):
        g = (2 * j) // (SSM_HEADS // SSM_GROUPS)
        cb = _dot_nt(cm[:, g * SSM_STATE:(g + 1) * SSM_STATE], bm[:, g * SSM_STATE:(g + 1) * SSM_STATE])
        ms = []
        for h in (2 * j, 2 * j + 1):
            seg = acs[:, h:h + 1] - acs_t[h:h + 1, :]
            dec = jnp.exp(jnp.where(causal, seg, NEG_INF))
            ms.append((cb * dec * dt_t[h:h + 1, :]).astype(BF16))
        xpair = xs_bf[:, j * LANES:(j + 1) * LANES]
        zero = jnp.zeros_like(xpair)
        rhs = jnp.concatenate([jnp.where(pair_lane < SSM_HEAD_DIM, xpair, zero),
                               jnp.where(pair_lane >= SSM_HEAD_DIM, xpair, zero)], axis=0)
        ydiag.append(_dot(jnp.concatenate(ms, axis=1), rhs))
    y = jnp.concatenate(ydiag, axis=1)

    yoff = []
    for g in range(SSM_GROUPS):
        st = st_scr[g]
        yoff.append(_dot(cm[:, g * SSM_STATE:(g + 1) * SSM_STATE], st.astype(BF16)))
        upd = _dot_tn(bm[:, g * SSM_STATE:(g + 1) * SSM_STATE], xw[:, g * gw:(g + 1) * gw])
        st_scr[g] = st * ea_x[L - 1:L, g * gw:(g + 1) * gw] + upd
    y = y + jnp.concatenate(yoff, axis=1) * ea_x + dskip_ref[...] * xs

    zz = z_ref[...].astype(F32)
    y = y * (zz * _sigmoid(zz))
    y_ref[...] = _rms(y, nw_ref[...]).astype(BF16)


def _ssd(xbc, z, gates, convw, convb, dtb, aneg, dskip_x, nw, sel, b, s):
    L = min(MIX_CHUNK, s)
    nc = s // L
    tok = lambda bi, ci: (bi * nc + ci, 0)
    const = lambda bi, ci: (0, 0)
    return pl.pallas_call(
        _ssd_kernel,
        grid=(b, nc),
        in_specs=[
            pl.BlockSpec((L, CONV_DIM), tok),
            pl.BlockSpec((L, SSM_WIDTH), tok),
            pl.BlockSpec((L, LANES), tok),
            pl.BlockSpec((CONV_WIDTH, CONV_DIM), const),
            pl.BlockSpec((1, CONV_DIM), const),
            pl.BlockSpec((1, LANES), const),
            pl.BlockSpec((1, LANES), const),
            pl.BlockSpec((1, SSM_WIDTH), const),
            pl.BlockSpec((1, SSM_WIDTH), const),
            pl.BlockSpec((LANES, SSM_WIDTH), const),
        ],
        out_specs=pl.BlockSpec((L, SSM_WIDTH), tok),
        out_shape=jax.ShapeDtypeStruct((b * s, SSM_WIDTH), BF16),
        scratch_shapes=[pltpu.VMEM((L + 2 * SUBLANES, CONV_DIM), F32),
                        pltpu.VMEM((SSM_GROUPS, SSM_STATE, SSM_WIDTH // SSM_GROUPS), F32)],
        compiler_params=_cparams(("arbitrary", "arbitrary")),
        name="ssd",
    )(xbc, z, gates, convw, convb, dtb, aneg, dskip_x, nw, sel)


def _mlstm_kernel(q_ref, k_ref, v_ref, o_ref, gi_ref, gf_ref, ib_ref, fb_ref, nw_ref, y_ref,
                  st_scr, m_scr):
    L = q_ref.shape[0]
    dh = LSTM_HEAD_DIM
    scale = dh ** -0.5

    @pl.when(pl.program_id(1) == 0)
    def _():
        st_scr[...] = jnp.zeros_like(st_scr)
        m_scr[...] = jnp.zeros_like(m_scr)

    row = lax.broadcasted_iota(jnp.int32, (L, L), 0)
    col = lax.broadcasted_iota(jnp.int32, (L, L), 1)
    causal = row >= col
    tril_bf = causal.astype(F32).astype(BF16)
    lane = lax.broadcasted_iota(jnp.int32, (1, LANES), 1)
    live = lane < LSTM_HEADS

    ii = jnp.where(live, gi_ref[...] + ib_ref[...], 0.0)
    logf = jnp.where(live, -_softplus(-(gf_ref[...] + fb_ref[...])), 0.0)
    cumf = _sel_dot(tril_bf, logf)
    g = ii - cumf
    rid = lax.broadcasted_iota(jnp.int32, (L, LANES), 0)
    cmx = g
    step = 1
    while step < L:
        cmx = jnp.maximum(cmx, jnp.where(rid >= step, pltpu.roll(cmx, step, axis=0), NEG_INF))
        step *= 2
    m_prev = m_scr[...]
    mx = jnp.maximum(m_prev, cmx)
    w_inter = jnp.exp(m_prev - mx)
    enm = jnp.exp(-(cumf + mx))
    g_t = g.T
    m_last = mx[L - 1:L, :]
    wk = jnp.exp(g - m_last) * scale
    sc = jnp.exp(m_prev - m_last)
    m_scr[...] = cumf[L - 1:L, :] + m_last

    ones_bf = jnp.ones((L, dh), BF16)
    for h in range(LSTM_HEADS):
        hs = slice(h * dh, (h + 1) * dh)
        qh = q_ref[:, hs]
        kh = k_ref[:, hs]
        vaug = jnp.concatenate([v_ref[:, hs], ones_bf], axis=1)
        dm = jnp.exp(jnp.where(causal, g_t[h:h + 1, :] - mx[:, h:h + 1], NEG_INF))
        p = (_dot_nt(qh, kh) * scale * dm).astype(BF16)
        r1 = _dot(p, vaug)
        st = st_scr[h]
        r2 = _dot(qh, st.astype(BF16))
        wcol = w_inter[:, h:h + 1]
        num = r1[:, :dh] + r2[:, :dh] * wcol
        den = r1[:, dh:] + r2[:, dh:] * wcol
        den = jnp.maximum(jnp.abs(den), enm[:, h:h + 1])
        hh = _rms(num / den, nw_ref[:, hs])
        oo = o_ref[:, hs].astype(F32)
        y_ref[:, hs] = (_sigmoid(oo) * hh).astype(BF16)
        kw = (kh.astype(F32) * wk[:, h:h + 1]).astype(BF16)
        st_scr[h] = st * sc[:, h:h + 1] + _dot_tn(kw, vaug)


def _mlstm(q, k, v, o, gates, ib, fb, nw, b, s):
    L = min(MIX_CHUNK, s)
    nc = s // L
    tok = lambda bi, ci: (bi * nc + ci, 0)
    const = lambda bi, ci: (0, 0)
    big = pl.BlockSpec((L, LSTM_WIDTH), tok)
    return pl.pallas_call(
        _mlstm_kernel,
        grid=(b, nc),
        in_specs=[big, big, big, big,
                  pl.BlockSpec((L, LANES), lambda bi, ci: (bi * nc + ci, 1)),
                  pl.BlockSpec((L, LANES), lambda bi, ci: (bi * nc + ci, 2)),
                  pl.BlockSpec((1, LANES), const),
                  pl.BlockSpec((1, LANES), const),
                  pl.BlockSpec((1, LSTM_WIDTH), const)],
        out_specs=big,
        out_shape=jax.ShapeDtypeStruct((b * s, LSTM_WIDTH), BF16),
        scratch_shapes=[pltpu.VMEM((LSTM_HEADS, LSTM_HEAD_DIM, 2 * LSTM_HEAD_DIM), F32),
                        pltpu.VMEM((1, LANES), F32)],
        compiler_params=_cparams(("arbitrary", "arbitrary")),
        name="mlstm",
    )(q, k, v, o, gates, gates, ib, fb, nw)


META_IDX, META_GATE, META_RANK = 0, TOP_K, 2 * TOP_K


def _outproj_kernel(ys_ref, yl_ref, x_ref, wo_ref, nw_ref, wr_ref, br_ref, h_ref, hn_ref, meta_ref,
                    cnt_ref, cnt_scr):
    tm = x_ref.shape[0]

    @pl.when(pl.program_id(0) == 0)
    def _():
        cnt_scr[...] = jnp.zeros_like(cnt_scr)

    h = (x_ref[...] + _dot(ys_ref[...], wo_ref[:SSM_WIDTH, :]) + _dot(yl_ref[...], wo_ref[SSM_WIDTH:, :]))
    h_ref[...] = h
    hn = _rms(h, nw_ref[...])
    _store_row_tiles(hn_ref, hn)
    vals = _dot(hn.astype(BF16), wr_ref[...]) + br_ref[...]

    lane = lax.broadcasted_iota(jnp.int32, (tm, LANES), 1)
    member = jnp.zeros((tm, LANES), F32)
    tops, idxs, sels = [], [], []
    for _ in range(TOP_K):
        m = jnp.max(vals, axis=-1, keepdims=True)
        idx = jnp.min(jnp.where(vals == m, lane, LANES), axis=-1, keepdims=True)
        sel = lane == idx
        vals = jnp.where(sel, NEG_INF, vals)
        member = member + sel.astype(F32)
        tops.append(m)
        idxs.append(idx)
        sels.append(sel)
    es = [jnp.exp(t - tops[0]) for t in tops]
    inv = 1.0 / (es[0] + es[1] + es[2] + es[3])

    r = lax.broadcasted_iota(jnp.int32, (tm, tm), 0)
    c = lax.broadcasted_iota(jnp.int32, (tm, tm), 1)
    strict = (r > c).astype(F32).astype(BF16)
    carry = cnt_scr[0:1, :]
    rank_all = _dot(strict, member.astype(BF16)) + carry
    total = carry + jnp.sum(member, axis=0, keepdims=True)
    cnt_scr[...] = jnp.broadcast_to(total, cnt_scr.shape)
    cnt_ref[...] = jnp.broadcast_to(total, cnt_ref.shape)

    meta = jnp.zeros((tm, LANES), F32)
    for kk in range(TOP_K):
        rank = jnp.sum(jnp.where(sels[kk], rank_all, 0.0), axis=-1, keepdims=True)
        meta = jnp.where(lane == META_IDX + kk, idxs[kk].astype(F32), meta)
        meta = jnp.where(lane == META_GATE + kk, es[kk] * inv, meta)
        meta = jnp.where(lane == META_RANK + kk, rank, meta)
    meta_ref[...] = meta


def _outproj(ys, yl, x2, wo, nw, wr, br):
    t = x2.shape[0]
    tm = min(ROW_TILE, t)
    tokspec = lambda w: pl.BlockSpec((tm, w), lambda i: (i, 0))
    const = lambda i: (0, 0)
    return pl.pallas_call(
        _outproj_kernel,
        grid=(t // tm,),
        in_specs=[tokspec(SSM_WIDTH), tokspec(LSTM_WIDTH), tokspec(D_MODEL),
                  pl.BlockSpec((SSM_WIDTH + LSTM_WIDTH, D_MODEL), const, pipeline_mode=pl.Buffered(1)),
                  pl.BlockSpec((1, D_MODEL), const),
                  pl.BlockSpec((D_MODEL, LANES), const),
                  pl.BlockSpec((1, LANES), const)],
        out_specs=[tokspec(D_MODEL), pl.BlockSpec((tm * ROW_SUBTILES, LANES), lambda i: (i, 0)), tokspec(LANES),
                   pl.BlockSpec((SUBLANES, LANES), const)],
        out_shape=[jax.ShapeDtypeStruct((t, D_MODEL), F32), jax.ShapeDtypeStruct((t * ROW_SUBTILES, LANES), F32),
                   jax.ShapeDtypeStruct((t, LANES), F32), jax.ShapeDtypeStruct((SUBLANES, LANES), F32)],
        scratch_shapes=[pltpu.VMEM((SUBLANES, LANES), F32)],
        compiler_params=_cparams(("arbitrary",)),
        name="outproj_router",
    )(ys, yl, x2, wo, nw, wr, br)


DISPATCH_TILE = 256
DISPATCH_GROUP = 4
PAD_CHUNKS = (128, 64, 32, 16, 8, 4, 2, 1)


def _dispatch_kernel(padstart_ref, padlen_ref, misc_ref, pos_ref, hn_hbm, xs_hbm, sem, zsem, zeros_scr):
    i = pl.program_id(0)
    n = pl.num_programs(0)
    td = pos_ref.shape[-1] // TOP_K

    @pl.when(i == 0)
    def _():
        zeros_scr[...] = jnp.zeros_like(zeros_scr)
        for e in range(N_EXPERTS):
            row = padstart_ref[e]
            nrow = padlen_ref[e]
            for chunk in PAD_CHUNKS:
                @pl.when((nrow & chunk) != 0)
                def _(row=row, chunk=chunk):
                    pltpu.make_async_copy(
                        zeros_scr.at[pl.ds(0, chunk * ROW_SUBTILES), :],
                        xs_hbm.at[pl.ds(pl.multiple_of(row * ROW_SUBTILES, ROW_SUBTILES), chunk * ROW_SUBTILES), :],
                        zsem).start()
                row = row + (nrow & chunk)

        def tail_copy(blk):
            return pltpu.make_async_copy(
                zeros_scr,
                xs_hbm.at[pl.ds(pl.multiple_of(blk * (MOE_BLOCK * ROW_SUBTILES), ROW_SUBTILES),
                                MOE_BLOCK * ROW_SUBTILES), :], zsem)

        n_used, n_blocks = misc_ref[1], misc_ref[2]

        def start_tail(blk, carry):
            tail_copy(blk).start()
            return carry
        lax.fori_loop(n_used, n_blocks, start_tail, 0)

        def drain_row(j, carry):
            _wait_row_copies(zeros_scr, xs_hbm, 1, zsem)
            return carry
        lax.fori_loop(0, misc_ref[0], drain_row, 0)

        def drain_tail(blk, carry):
            tail_copy(blk).wait()
            return carry
        lax.fori_loop(n_used, n_blocks, drain_tail, 0)

    base = i * td

    def body(g, carry):
        r0 = g * DISPATCH_GROUP
        slots = [pos_ref[0, 0, r0 * TOP_K + j] for j in range(DISPATCH_GROUP * TOP_K)]
        for j, slot in enumerate(slots):
            _row_copy(hn_hbm, base + r0 + j // TOP_K, xs_hbm, slot, sem).start()
        return carry
    lax.fori_loop(0, td // DISPATCH_GROUP, body, 0)

    @pl.when(i > 0)
    def _():
        _wait_row_copies(hn_hbm, xs_hbm, td * TOP_K, sem)

    @pl.when(i == n - 1)
    def _():
        _wait_row_copies(hn_hbm, xs_hbm, td * TOP_K, sem)


def _dispatch(pad_start, pad_len, misc, pos_flat, hn_rt, n_rows):
    t = pos_flat.shape[0] // TOP_K
    td = min(DISPATCH_TILE, t)
    n = t // td
    grid_spec = pltpu.PrefetchScalarGridSpec(
        num_scalar_prefetch=3,
        grid=(n,),
        in_specs=[pl.BlockSpec((1, 1, TOP_K * td), lambda i, *_: (i, 0, 0), memory_space=pltpu.SMEM),
                  pl.BlockSpec(memory_space=pl.ANY)],
        out_specs=pl.BlockSpec(memory_space=pl.ANY),
        scratch_shapes=[pltpu.SemaphoreType.DMA(()), pltpu.SemaphoreType.DMA(()),
                        pltpu.VMEM((MOE_BLOCK * ROW_SUBTILES, LANES), F32)],
    )
    return pl.pallas_call(
        _dispatch_kernel,
        grid_spec=grid_spec,
        out_shape=jax.ShapeDtypeStruct((n_rows * ROW_SUBTILES, LANES), F32),
        compiler_params=_cparams(("arbitrary",)),
        name="dispatch",
    )(pad_start, pad_len, misc, pos_flat.reshape(n, 1, TOP_K * td), hn_rt)


def _experts_kernel(be_ref, nused_ref, x_ref, wgu_ref, bgu_ref, wdn_ref, bdn_ref, out_ref, wgu_bf, wdn_bf):
    i = pl.program_id(0)
    n_used = nused_ref[0]

    @pl.when(i < n_used)
    def _():
        @pl.when(jnp.logical_or(i == 0, be_ref[i] != be_ref[jnp.maximum(i - 1, 0)]))
        def _():
            wgu_bf[...] = wgu_ref[...].astype(BF16)
            wdn_bf[...] = wdn_ref[...].astype(BF16)

        xb = jnp.concatenate(_load_row_tile_cols(x_ref, MOE_BLOCK), axis=1).astype(BF16)
        hh = _dot(xb, wgu_bf[...]) + bgu_ref[...]
        gh = jnp.minimum(hh[:, :D_FF], SWIGLU_LIMIT)
        uh = jnp.clip(hh[:, D_FF:], -SWIGLU_LIMIT, SWIGLU_LIMIT)
        act = (uh + 1.0) * (gh * _sigmoid(SWIGLU_ALPHA * gh))
        _store_row_tiles(out_ref, _dot(act.astype(BF16), wdn_bf[...]) + bdn_ref[...])

    @pl.when(i >= n_used)
    def _():
        out_ref[...] = jnp.zeros_like(out_ref)


def _experts(block_expert, n_used, xs_rt, wgu, bgu3, wdn, bdn3):
    n_blocks = block_expert.shape[0]
    blk = MOE_BLOCK * ROW_SUBTILES
    grid_spec = pltpu.PrefetchScalarGridSpec(
        num_scalar_prefetch=2,
        grid=(n_blocks,),
        in_specs=[
            pl.BlockSpec((blk, LANES), lambda i, be, nu: (i, 0)),
            pl.BlockSpec((None, D_MODEL, 2 * D_FF), lambda i, be, nu: (be[i], 0, 0)),
            pl.BlockSpec((None, 1, 2 * D_FF), lambda i, be, nu: (be[i], 0, 0)),
            pl.BlockSpec((None, D_FF, D_MODEL), lambda i, be, nu: (be[i], 0, 0)),
            pl.BlockSpec((None, 1, D_MODEL), lambda i, be, nu: (be[i], 0, 0)),
        ],
        out_specs=pl.BlockSpec((blk, LANES), lambda i, be, nu: (i, 0)),
        scratch_shapes=[pltpu.VMEM((D_MODEL, 2 * D_FF), BF16),
                        pltpu.VMEM((D_FF, D_MODEL), BF16)],
    )
    return pl.pallas_call(
        _experts_kernel,
        grid_spec=grid_spec,
        out_shape=jax.ShapeDtypeStruct((n_blocks * blk, LANES), F32),
        compiler_params=_cparams(("arbitrary",)),
        name="experts",
    )(block_expert, n_used, xs_rt, wgu, bgu3, wdn, bdn3)


def _combine_kernel(pos_cur, pos_next, outs_hbm, h_ref, meta_ref, nw_ref, y_ref, buf, sem):
    i = pl.program_id(0)
    n = pl.num_programs(0)
    tc = h_ref.shape[0]
    slot = i % 2

    def issue(pos_ref, s):
        def body(r, carry):
            for kk in range(TOP_K):
                _row_copy(outs_hbm, pos_ref[0, 0, r * TOP_K + kk], buf.at[s, kk], r, sem.at[s]).start()
            return carry
        lax.fori_loop(0, tc, body, 0, unroll=4)

    @pl.when(i == 0)
    def _():
        issue(pos_cur, 0)

    @pl.when(i + 1 < n)
    def _():
        issue(pos_next, 1 - slot)

    for kk in range(TOP_K):
        _wait_row_copies(outs_hbm, buf.at[slot, kk], tc, sem.at[slot])
    meta = meta_ref[...]
    gates = [meta[:, META_GATE + kk:META_GATE + kk + 1] for kk in range(TOP_K)]
    rows = [_load_row_tile_cols(buf.at[slot, kk], tc) for kk in range(TOP_K)]
    cols = []
    for s in range(ROW_SUBTILES):
        acc = h_ref[:, s * LANES:(s + 1) * LANES]
        for kk in range(TOP_K):
            acc = acc + gates[kk] * rows[kk][s]
        cols.append(acc)
    y_ref[...] = _rms(jnp.concatenate(cols, axis=1), nw_ref[...])


def _combine(pos_flat, outs_rt, h, meta, nw):
    t = h.shape[0]
    tc = min(COMBINE_TILE, t)
    n = t // tc
    pos3 = pos_flat.reshape(n, 1, TOP_K * tc)
    return pl.pallas_call(
        _combine_kernel,
        grid=(n,),
        in_specs=[
            pl.BlockSpec((1, 1, TOP_K * tc), lambda i: (i, 0, 0), memory_space=pltpu.SMEM),
            pl.BlockSpec((1, 1, TOP_K * tc), lambda i: (jnp.minimum(i + 1, n - 1), 0, 0),
                         memory_space=pltpu.SMEM),
            pl.BlockSpec(memory_space=pl.ANY),
            pl.BlockSpec((tc, D_MODEL), lambda i: (i, 0)),
            pl.BlockSpec((tc, LANES), lambda i: (i, 0)),
            pl.BlockSpec((1, D_MODEL), lambda i: (0, 0)),
        ],
        out_specs=pl.BlockSpec((tc, D_MODEL), lambda i: (i, 0)),
        out_shape=jax.ShapeDtypeStruct((t, D_MODEL), F32),
        scratch_shapes=[pltpu.VMEM((2, TOP_K, tc * ROW_SUBTILES, LANES), F32), pltpu.SemaphoreType.DMA((2,))],
        compiler_params=_cparams(("arbitrary",)),
        name="combine",
    )(pos3, pos3, outs_rt, h, meta, nw)


def _pad_lanes(v, fill=0.0):
    v = v.astype(F32).reshape(1, -1)
    return jnp.pad(v, ((0, 0), (0, LANES - v.shape[1])), constant_values=fill)


def kernel(x, norm_mix_w, w_in, conv_w, conv_b, dt_bias, a_log, d_skip, ssm_norm_w, lstm_i_bias,
           lstm_f_bias, lstm_norm_w, w_out, norm_ffn_w, w_router, b_router, w_gate_up, b_gate_up,
           w_down, b_down, norm_final_w):
    b, s, d = x.shape
    t = b * s
    x2 = x.reshape(t, d).astype(F32)
    depth = w_in.shape[0]
    assert depth == 1, "the combine kernel fuses the final norm, so exactly one layer is supported"
    for layer in range(depth):
        wi = w_in[layer]
        c0 = SSM_WIDTH
        c1 = c0 + CONV_DIM
        c2 = c1 + SSM_HEADS
        c3 = c2 + 4 * LSTM_WIDTH
        c4 = c3 + LSTM_HEADS
        padc = lambda w: jnp.pad(w, ((0, 0), (0, LANES - w.shape[1])))
        w_all = jnp.concatenate([wi[:, :c1], wi[:, c2:c3], padc(wi[:, c1:c2]), padc(wi[:, c3:c4]),
                                 padc(wi[:, c4:])], axis=1).astype(BF16)
        sel = (jnp.arange(LANES)[:, None] == (jnp.arange(SSM_WIDTH) // SSM_HEAD_DIM)[None, :]).astype(BF16)
        a_neg = _pad_lanes(-jnp.exp(a_log[layer].astype(F32)))
        dskip_x = jnp.repeat(d_skip[layer].astype(F32), SSM_HEAD_DIM).reshape(1, SSM_WIDTH)
        wr = jnp.pad(w_router[layer], ((0, 0), (0, LANES - N_EXPERTS))).astype(BF16)
        br = _pad_lanes(b_router[layer], fill=NEG_INF)

        z, xbc, q, k, v, o, gates = _inproj(x2, norm_mix_w[layer].reshape(1, d).astype(F32), w_all)
        y_ssd = _ssd(xbc, z, gates, conv_w[layer].astype(F32), conv_b[layer].reshape(1, -1).astype(F32),
                     _pad_lanes(dt_bias[layer]), a_neg, dskip_x,
                     ssm_norm_w[layer].reshape(1, -1).astype(F32), sel, b, s)
        y_lstm = _mlstm(q, k, v, o, gates, _pad_lanes(lstm_i_bias[layer]), _pad_lanes(lstm_f_bias[layer]),
                        lstm_norm_w[layer].reshape(1, -1).astype(F32), b, s)
        h, hn_rt, meta, cnt = _outproj(y_ssd, y_lstm, x2, w_out[layer].astype(BF16),
                                       norm_ffn_w[layer].reshape(1, d).astype(F32), wr, br)

        idx = meta[:, META_IDX:META_IDX + TOP_K].astype(jnp.int32)
        rank = meta[:, META_RANK:META_RANK + TOP_K].astype(jnp.int32)
        counts = cnt[0, :N_EXPERTS].astype(jnp.int32)
        n_blocks = -(-(t * TOP_K) // MOE_BLOCK) + N_EXPERTS
        padded = (counts + MOE_BLOCK - 1) // MOE_BLOCK * MOE_BLOCK
        padded_ends = jnp.cumsum(padded)
        padded_starts = padded_ends - padded
        onehot = idx[..., None] == jnp.arange(N_EXPERTS, dtype=jnp.int32)
        pos = (jnp.sum(jnp.where(onehot, padded_starts, 0), axis=-1) + rank).reshape(-1)
        block_start = jnp.arange(n_blocks, dtype=jnp.int32) * MOE_BLOCK
        block_expert = jnp.minimum(
            jnp.sum((padded_ends[None, :] <= block_start[:, None]).astype(jnp.int32), axis=1), N_EXPERTS - 1)
        n_used = padded_ends[-1:] // MOE_BLOCK
        pad_len = padded - counts
        misc = jnp.concatenate([jnp.sum(pad_len, keepdims=True), n_used,
                                jnp.full((1,), n_blocks, jnp.int32)])

        xs_rt = _dispatch(padded_starts + counts, pad_len, misc, pos, hn_rt, n_blocks * MOE_BLOCK)
        outs_rt = _experts(block_expert, n_used, xs_rt, w_gate_up[layer],
                           b_gate_up[layer].reshape(N_EXPERTS, 1, -1), w_down[layer],
                           b_down[layer].reshape(N_EXPERTS, 1, -1))
        x2 = _combine(pos, outs_rt, h, meta, norm_final_w.reshape(1, d).astype(F32))
    return x2.reshape(b, s, d).astype(x.dtype)
```

```python
import functools

import jax
import jax.numpy as jnp
import numpy as np
from jax import lax
from jax.experimental import pallas as pl
from jax.experimental.pallas import tpu as pltpu

F32 = jnp.float32
BF16 = jnp.bfloat16

D_MODEL = 1024
SSM_WIDTH = 1024
SSM_HEAD_DIM = 64
SSM_HEADS = 16
SSM_GROUPS = 2
SSM_STATE = 128
CONV_WIDTH = 4
CONV_DIM = SSM_WIDTH + 2 * SSM_GROUPS * SSM_STATE
LSTM_WIDTH = 1024
LSTM_HEAD_DIM = 128
LSTM_HEADS = 8
N_EXPERTS = 32
TOP_K = 4
D_FF = 1024
SWIGLU_LIMIT = 7.0
SWIGLU_ALPHA = 1.702
MOE_BLOCK = 256
RMS_EPS = 1e-6

LANES = 128
SUBLANES = 8
MIX_CHUNK = 256
ROW_TILE = 512
COMBINE_TILE = 256
VMEM_LIMIT = 56 * 1024 * 1024

GATE_COLS = 3 * LANES
NEG_INF = float("-inf")


def _cparams(sem):
    return pltpu.CompilerParams(dimension_semantics=sem, vmem_limit_bytes=VMEM_LIMIT)


def _rms(x, w):
    return x * lax.rsqrt(jnp.mean(x * x, axis=-1, keepdims=True) + RMS_EPS) * w


def _sigmoid(x):
    return 1.0 / (1.0 + jnp.exp(-x))


def _softplus(x):
    return jnp.maximum(x, 0.0) + jnp.log(1.0 + jnp.exp(-jnp.abs(x)))


def _split3(a):
    hi = a.astype(BF16)
    r = a - hi.astype(F32)
    mid = r.astype(BF16)
    lo = (r - mid.astype(F32)).astype(BF16)
    return hi, mid, lo


def _dot(a, b):
    return jnp.dot(a, b, preferred_element_type=F32)


def _dot_nt(a, b):
    return lax.dot_general(a, b, (((1,), (1,)), ((), ())), preferred_element_type=F32)


def _dot_tn(a, b):
    return lax.dot_general(a, b, (((0,), (0,)), ((), ())), preferred_element_type=F32)


def _sel_dot(sel_bf, a):
    hi, mid, lo = _split3(a)
    return _dot(sel_bf, hi) + _dot(sel_bf, mid) + _dot(sel_bf, lo)


def _expand(a, sel_bf):
    hi, mid, lo = _split3(a)
    return _dot(hi, sel_bf) + _dot(mid, sel_bf) + _dot(lo, sel_bf)


ROW_SUBTILES = D_MODEL // LANES


def _store_row_tiles(ref, x):
    n = x.shape[0]
    for s in range(ROW_SUBTILES):
        ref[pl.ds(s, n, stride=ROW_SUBTILES), :] = x[:, s * LANES:(s + 1) * LANES]


def _load_row_tile_cols(ref, n):
    return [ref[pl.ds(s, n, stride=ROW_SUBTILES), :] for s in range(ROW_SUBTILES)]


def _row_copy(src, src_row, dst, dst_row, sem):
    return pltpu.make_async_copy(
        src.at[pl.ds(pl.multiple_of(src_row * ROW_SUBTILES, ROW_SUBTILES), ROW_SUBTILES), :],
        dst.at[pl.ds(pl.multiple_of(dst_row * ROW_SUBTILES, ROW_SUBTILES), ROW_SUBTILES), :], sem)


def _wait_row_copies(src, dst, n_rows, sem):
    size = n_rows * ROW_SUBTILES
    pltpu.make_async_copy(src.at[pl.ds(0, size), :], dst.at[pl.ds(0, size), :], sem).wait()


_INPROJ_WIDTHS = (SSM_WIDTH, CONV_DIM, LSTM_WIDTH, LSTM_WIDTH, LSTM_WIDTH, LSTM_WIDTH, GATE_COLS)


def _inproj_kernel(x_ref, nw_ref, w_ref, z_ref, xbc_ref, q_ref, k_ref, v_ref, o_ref, g_ref):
    xb = _rms(x_ref[...], nw_ref[...]).astype(BF16)
    off = 0
    for ref, width in zip((z_ref, xbc_ref, q_ref, k_ref, v_ref, o_ref, g_ref), _INPROJ_WIDTHS):
        ref[...] = _dot(xb, w_ref[:, off:off + width]).astype(ref.dtype)
        off += width


def _inproj(x2, nw, w_all):
    t = x2.shape[0]
    tm = min(ROW_TILE, t)
    ncol = w_all.shape[1]
    out_shape = [jax.ShapeDtypeStruct((t, w), BF16) for w in _INPROJ_WIDTHS[:-1]]
    out_shape.append(jax.ShapeDtypeStruct((t, GATE_COLS), F32))
    return pl.pallas_call(
        _inproj_kernel,
        grid=(t // tm,),
        in_specs=[
            pl.BlockSpec((tm, D_MODEL), lambda i: (i, 0)),
            pl.BlockSpec((1, D_MODEL), lambda i: (0, 0)),
            pl.BlockSpec((D_MODEL, ncol), lambda i: (0, 0), pipeline_mode=pl.Buffered(1)),
        ],
        out_specs=[pl.BlockSpec((tm, w), lambda i: (i, 0)) for w in _INPROJ_WIDTHS],
        out_shape=out_shape,
        compiler_params=_cparams(("arbitrary",)),
        name="inproj",
    )(x2, nw, w_all)


def _ssd_kernel(xbc_ref, z_ref, g_ref, convw_ref, convb_ref, dtb_ref, aneg_ref, dskip_ref, nw_ref,
                sel_ref, y_ref, u_scr, st_scr):
    L = xbc_ref.shape[0]
    gw = SSM_WIDTH // SSM_GROUPS

    @pl.when(pl.program_id(1) == 0)
    def _():
        u_scr[0:SUBLANES, :] = jnp.zeros((SUBLANES, CONV_DIM), F32)
        st_scr[...] = jnp.zeros_like(st_scr)

    u_scr[SUBLANES:SUBLANES + L, :] = xbc_ref[...].astype(F32)
    acc = jnp.broadcast_to(convb_ref[...], (L, CONV_DIM))
    for j in range(CONV_WIDTH):
        back = CONV_WIDTH - 1 - j
        acc = acc + convw_ref[j:j + 1, :] * u_scr[SUBLANES - back:SUBLANES - back + L, :]
    u_scr[0:SUBLANES, :] = u_scr[L:L + SUBLANES, :]
    xbc = acc * _sigmoid(acc)
    xs = xbc[:, :SSM_WIDTH]
    xs_bf = xs.astype(BF16)
    bm = xbc[:, SSM_WIDTH:SSM_WIDTH + SSM_GROUPS * SSM_STATE].astype(BF16)
    cm = xbc[:, SSM_WIDTH + SSM_GROUPS * SSM_STATE:].astype(BF16)

    row = lax.broadcasted_iota(jnp.int32, (L, L), 0)
    col = lax.broadcasted_iota(jnp.int32, (L, L), 1)
    causal = row >= col
    tril_bf = causal.astype(F32).astype(BF16)

    lane = lax.broadcasted_iota(jnp.int32, (1, LANES), 1)
    dt = jnp.where(lane < SSM_HEADS, _softplus(g_ref[...] + dtb_ref[...]), 0.0)
    acs = _sel_dot(tril_bf, dt * aneg_ref[...])
    acs_t = acs.T
    dt_t = dt.T
    sel = sel_ref[...]
    exp_acs = jnp.exp(acs)
    acs_last = acs[L - 1:L, :]
    dt_x = _expand(dt, sel)
    ea_x = _expand(exp_acs, sel)
    de_x = _expand(jnp.exp(acs_last - acs), sel)
    xw = (xs * dt_x * de_x).astype(BF16)

    pair_lane = lax.broadcasted_iota(jnp.int32, (1, LANES), 1)
    ydiag = []
    for j in range(SSM_HEADS // 2):
        g = (2 * j) // (SSM_HEADS // SSM_GROUPS)
        cb = _dot_nt(cm[:, g * SSM_STATE:(g + 1) * SSM_STATE], bm[:, g * SSM_STATE:(g + 1) * SSM_STATE])
        ms = []
        for h in (2 * j, 2 * j + 1):
            seg = acs[:, h:h + 1] - acs_t[h:h + 1, :]
            dec = jnp.exp(jnp.where(causal, seg, NEG_INF))
            ms.append((cb * dec * dt_t[h:h + 1, :]).astype(BF16))
        xpair = xs_bf[:, j * LANES:(j + 1) * LANES]
        zero = jnp.zeros_like(xpair)
        rhs = jnp.concatenate([jnp.where(pair_lane < SSM_HEAD_DIM, xpair, zero),
                               jnp.where(pair_lane >= SSM_HEAD_DIM, xpair, zero)], axis=0)
        ydiag.append(_dot(jnp.concatenate(ms, axis=1), rhs))
    y = jnp.concatenate(ydiag, axis=1)

    yoff = []
    for g in range(SSM_GROUPS):
        st = st_scr[g]
        yoff.append(_dot(cm[:, g * SSM_STATE:(g + 1) * SSM_STATE], st.astype(BF16)))
        upd = _dot_tn(bm[:, g * SSM_STATE:(g + 1) * SSM_STATE], xw[:, g * gw:(g + 1) * gw])
        st_scr[g] = st * ea_x[L - 1:L, g * gw:(g + 1) * gw] + upd
    y = y + jnp.concatenate(yoff, axis=1) * ea_x + dskip_ref[...] * xs

    zz = z_ref[...].astype(F32)
    y = y * (zz * _sigmoid(zz))
    y_ref[...] = _rms(y, nw_ref[...]).astype(BF16)


def _ssd(xbc, z, gates, convw, convb, dtb, aneg, dskip_x, nw, sel, b, s):
    L = min(MIX_CHUNK, s)
    nc = s // L
    tok = lambda bi, ci: (bi * nc + ci, 0)
    const = lambda bi, ci: (0, 0)
    return pl.pallas_call(
        _ssd_kernel,
        grid=(b, nc),
        in_specs=[
            pl.BlockSpec((L, CONV_DIM), tok),
            pl.BlockSpec((L, SSM_WIDTH), tok),
            pl.BlockSpec((L, LANES), tok),
            pl.BlockSpec((CONV_WIDTH, CONV_DIM), const),
            pl.BlockSpec((1, CONV_DIM), const),
            pl.BlockSpec((1, LANES), const),
            pl.BlockSpec((1, LANES), const),
            pl.BlockSpec((1, SSM_WIDTH), const),
            pl.BlockSpec((1, SSM_WIDTH), const),
            pl.BlockSpec((LANES, SSM_WIDTH), const),
        ],
        out_specs=pl.BlockSpec((L, SSM_WIDTH), tok),
        out_shape=jax.ShapeDtypeStruct((b * s, SSM_WIDTH), BF16),
        scratch_shapes=[pltpu.VMEM((L + 2 * SUBLANES, CONV_DIM), F32),
                        pltpu.VMEM((SSM_GROUPS, SSM_STATE, SSM_WIDTH // SSM_GROUPS), F32)],
        compiler_params=_cparams(("arbitrary", "arbitrary")),
        name="ssd",
    )(xbc, z, gates, convw, convb, dtb, aneg, dskip_x, nw, sel)


def _mlstm_kernel(q_ref, k_ref, v_ref, o_ref, gi_ref, gf_ref, ib_ref, fb_ref, nw_ref, y_ref,
                  st_scr, m_scr):
    L = q_ref.shape[0]
    dh = LSTM_HEAD_DIM
    scale = dh ** -0.5

    @pl.when(pl.program_id(1) == 0)
    def _():
        st_scr[...] = jnp.zeros_like(st_scr)
        m_scr[...] = jnp.zeros_like(m_scr)

    row = lax.broadcasted_iota(jnp.int32, (L, L), 0)
    col = lax.broadcasted_iota(jnp.int32, (L, L), 1)
    causal = row >= col
    tril_bf = causal.astype(F32).astype(BF16)
    lane = lax.broadcasted_iota(jnp.int32, (1, LANES), 1)
    live = lane < LSTM_HEADS

    ii = jnp.where(live, gi_ref[...] + ib_ref[...], 0.0)
    logf = jnp.where(live, -_softplus(-(gf_ref[...] + fb_ref[...])), 0.0)
    cumf = _sel_dot(tril_bf, logf)
    g = ii - cumf
    rid = lax.broadcasted_iota(jnp.int32, (L, LANES), 0)
    cmx = g
    step = 1
    while step < L:
        cmx = jnp.maximum(cmx, jnp.where(rid >= step, pltpu.roll(cmx, step, axis=0), NEG_INF))
        step *= 2
    m_prev = m_scr[...]
    mx = jnp.maximum(m_prev, cmx)
    w_inter = jnp.exp(m_prev - mx)
    enm = jnp.exp(-(cumf + mx))
    g_t = g.T
    m_last = mx[L - 1:L, :]
    wk = jnp.exp(g - m_last) * scale
    sc = jnp.exp(m_prev - m_last)
    m_scr[...] = cumf[L - 1:L, :] + m_last

    ones_bf = jnp.ones((L, dh), BF16)
    for h in range(LSTM_HEADS):
        hs = slice(h * dh, (h + 1) * dh)
        qh = q_ref[:, hs]
        kh = k_ref[:, hs]
        vaug = jnp.concatenate([v_ref[:, hs], ones_bf], axis=1)
        dm = jnp.exp(jnp.where(causal, g_t[h:h + 1, :] - mx[:, h:h + 1], NEG_INF))
        p = (_dot_nt(qh, kh) * scale * dm).astype(BF16)
        r1 = _dot(p, vaug)
        st = st_scr[h]
        r2 = _dot(qh, st.astype(BF16))
        wcol = w_inter[:, h:h + 1]
        num = r1[:, :dh] + r2[:, :dh] * wcol
        den = r1[:, dh:] + r2[:, dh:] * wcol
        den = jnp.maximum(jnp.abs(den), enm[:, h:h + 1])
        hh = _rms(num / den, nw_ref[:, hs])
        oo = o_ref[:, hs].astype(F32)
        y_ref[:, hs] = (_sigmoid(oo) * hh).astype(BF16)
        kw = (kh.astype(F32) * wk[:, h:h + 1]).astype(BF16)
        st_scr[h] = st * sc[:, h:h + 1] + _dot_tn(kw, vaug)


def _mlstm(q, k, v, o, gates, ib, fb, nw, b, s):
    L = min(MIX_CHUNK, s)
    nc = s // L
    tok = lambda bi, ci: (bi * nc + ci, 0)
    const = lambda bi, ci: (0, 0)
    big = pl.BlockSpec((L, LSTM_WIDTH), tok)
    return pl.pallas_call(
        _mlstm_kernel,
        grid=(b, nc),
        in_specs=[big, big, big, big,
                  pl.BlockSpec((L, LANES), lambda bi, ci: (bi * nc + ci, 1)),
                  pl.BlockSpec((L, LANES), lambda bi, ci: (bi * nc + ci, 2)),
                  pl.BlockSpec((1, LANES), const),
                  pl.BlockSpec((1, LANES), const),
                  pl.BlockSpec((1, LSTM_WIDTH), const)],
        out_specs=big,
        out_shape=jax.ShapeDtypeStruct((b * s, LSTM_WIDTH), BF16),
        scratch_shapes=[pltpu.VMEM((LSTM_HEADS, LSTM_HEAD_DIM, 2 * LSTM_HEAD_DIM), F32),
                        pltpu.VMEM((1, LANES), F32)],
        compiler_params=_cparams(("arbitrary", "arbitrary")),
        name="mlstm",
    )(q, k, v, o, gates, gates, ib, fb, nw)


META_IDX, META_GATE, META_RANK = 0, TOP_K, 2 * TOP_K


def _outproj_kernel(ys_ref, yl_ref, x_ref, wo_ref, nw_ref, wr_ref, br_ref, h_ref, hn_ref, meta_ref,
                    cnt_ref, cnt_scr):
    tm = x_ref.shape[0]

    @pl.when(pl.program_id(0) == 0)
    def _():
        cnt_scr[...] = jnp.zeros_like(cnt_scr)

    h = (x_ref[...] + _dot(ys_ref[...], wo_ref[:SSM_WIDTH, :]) + _dot(yl_ref[...], wo_ref[SSM_WIDTH:, :]))
    h_ref[...] = h
    hn = _rms(h, nw_ref[...])
    _store_row_tiles(hn_ref, hn)
    vals = _dot(hn.astype(BF16), wr_ref[...]) + br_ref[...]

    lane = lax.broadcasted_iota(jnp.int32, (tm, LANES), 1)
    member = jnp.zeros((tm, LANES), F32)
    tops, idxs, sels = [], [], []
    for _ in range(TOP_K):
        m = jnp.max(vals, axis=-1, keepdims=True)
        idx = jnp.min(jnp.where(vals == m, lane, LANES), axis=-1, keepdims=True)
        sel = lane == idx
        vals = jnp.where(sel, NEG_INF, vals)
        member = member + sel.astype(F32)
        tops.append(m)
        idxs.append(idx)
        sels.append(sel)
    es = [jnp.exp(t - tops[0]) for t in tops]
    inv = 1.0 / (es[0] + es[1] + es[2] + es[3])

    r = lax.broadcasted_iota(jnp.int32, (tm, tm), 0)
    c = lax.broadcasted_iota(jnp.int32, (tm, tm), 1)
    strict = (r > c).astype(F32).astype(BF16)
    carry = cnt_scr[0:1, :]
    rank_all = _dot(strict, member.astype(BF16)) + carry
    total = carry + jnp.sum(member, axis=0, keepdims=True)
    cnt_scr[...] = jnp.broadcast_to(total, cnt_scr.shape)
    cnt_ref[...] = jnp.broadcast_to(total, cnt_ref.shape)

    meta = jnp.zeros((tm, LANES), F32)
    for kk in range(TOP_K):
        rank = jnp.sum(jnp.where(sels[kk], rank_all, 0.0), axis=-1, keepdims=True)
        meta = jnp.where(lane == META_IDX + kk, idxs[kk].astype(F32), meta)
        meta = jnp.where(lane == META_GATE + kk, es[kk] * inv, meta)
        meta = jnp.where(lane == META_RANK + kk, rank, meta)
    meta_ref[...] = meta


def _outproj(ys, yl, x2, wo, nw, wr, br):
    t = x2.shape[0]
    tm = min(ROW_TILE, t)
    tokspec = lambda w: pl.BlockSpec((tm, w), lambda i: (i, 0))
    const = lambda i: (0, 0)
    return pl.pallas_call(
        _outproj_kernel,
        grid=(t // tm,),
        in_specs=[tokspec(SSM_WIDTH), tokspec(LSTM_WIDTH), tokspec(D_MODEL),
                  pl.BlockSpec((SSM_WIDTH + LSTM_WIDTH, D_MODEL), const, pipeline_mode=pl.Buffered(1)),
                  pl.BlockSpec((1, D_MODEL), const),
                  pl.BlockSpec((D_MODEL, LANES), const),
                  pl.BlockSpec((1, LANES), const)],
        out_specs=[tokspec(D_MODEL), pl.BlockSpec((tm * ROW_SUBTILES, LANES), lambda i: (i, 0)), tokspec(LANES),
                   pl.BlockSpec((SUBLANES, LANES), const)],
        out_shape=[jax.ShapeDtypeStruct((t, D_MODEL), F32), jax.ShapeDtypeStruct((t * ROW_SUBTILES, LANES), F32),
                   jax.ShapeDtypeStruct((t, LANES), F32), jax.ShapeDtypeStruct((SUBLANES, LANES), F32)],
        scratch_shapes=[pltpu.VMEM((SUBLANES, LANES), F32)],
        compiler_params=_cparams(("arbitrary",)),
        name="outproj_router",
    )(ys, yl, x2, wo, nw, wr, br)


DISPATCH_TILE = 512
DISPATCH_GROUP = 4
PAD_CHUNKS = (128, 64, 32, 16, 8, 4, 2, 1)


def _dispatch_kernel(padstart_ref, padlen_ref, misc_ref, pos_ref, hn_ref, xs_hbm, sem, zsem, zeros_scr):
    i = pl.program_id(0)
    td = pos_ref.shape[-1] // TOP_K

    @pl.when(i == 0)
    def _():
        zeros_scr[...] = jnp.zeros_like(zeros_scr)
        for e in range(N_EXPERTS):
            row = padstart_ref[e]
            nrow = padlen_ref[e]
            for chunk in PAD_CHUNKS:
                @pl.when((nrow & chunk) != 0)
                def _(row=row, chunk=chunk):
                    pltpu.make_async_copy(
                        zeros_scr.at[pl.ds(0, chunk * ROW_SUBTILES), :],
                        xs_hbm.at[pl.ds(pl.multiple_of(row * ROW_SUBTILES, ROW_SUBTILES), chunk * ROW_SUBTILES), :],
                        zsem).start()
                row = row + (nrow & chunk)

        def tail_copy(blk):
            return pltpu.make_async_copy(
                zeros_scr,
                xs_hbm.at[pl.ds(pl.multiple_of(blk * (MOE_BLOCK * ROW_SUBTILES), ROW_SUBTILES),
                                MOE_BLOCK * ROW_SUBTILES), :], zsem)

        n_used, n_blocks = misc_ref[1], misc_ref[2]

        def start_tail(blk, carry):
            tail_copy(blk).start()
            return carry
        lax.fori_loop(n_used, n_blocks, start_tail, 0)

        def drain_row(j, carry):
            _wait_row_copies(zeros_scr, xs_hbm, 1, zsem)
            return carry
        lax.fori_loop(0, misc_ref[0], drain_row, 0)

        def drain_tail(blk, carry):
            tail_copy(blk).wait()
            return carry
        lax.fori_loop(n_used, n_blocks, drain_tail, 0)

    def body(g, carry):
        r0 = g * DISPATCH_GROUP
        slots = [pos_ref[0, 0, r0 * TOP_K + j] for j in range(DISPATCH_GROUP * TOP_K)]
        for j, slot in enumerate(slots):
            _row_copy(hn_ref, r0 + j // TOP_K, xs_hbm, slot, sem).start()
        return carry
    lax.fori_loop(0, td // DISPATCH_GROUP, body, 0)

    for _ in range(TOP_K):
        _wait_row_copies(hn_ref, xs_hbm, td, sem)


def _dispatch(pad_start, pad_len, misc, pos_flat, hn_rt, n_rows):
    t = pos_flat.shape[0] // TOP_K
    td = min(DISPATCH_TILE, t)
    n = t // td
    grid_spec = pltpu.PrefetchScalarGridSpec(
        num_scalar_prefetch=3,
        grid=(n,),
        in_specs=[pl.BlockSpec((1, 1, TOP_K * td), lambda i, *_: (i, 0, 0), memory_space=pltpu.SMEM),
                  pl.BlockSpec((td * ROW_SUBTILES, LANES), lambda i, *_: (i, 0))],
        out_specs=pl.BlockSpec(memory_space=pl.ANY),
        scratch_shapes=[pltpu.SemaphoreType.DMA(()), pltpu.SemaphoreType.DMA(()),
                        pltpu.VMEM((MOE_BLOCK * ROW_SUBTILES, LANES), F32)],
    )
    return pl.pallas_call(
        _dispatch_kernel,
        grid_spec=grid_spec,
        out_shape=jax.ShapeDtypeStruct((n_rows * ROW_SUBTILES, LANES), F32),
        compiler_params=_cparams(("arbitrary",)),
        name="dispatch",
    )(pad_start, pad_len, misc, pos_flat.reshape(n, 1, TOP_K * td), hn_rt)


def _experts_kernel(be_ref, nused_ref, x_ref, wgu_ref, bgu_ref, wdn_ref, bdn_ref, out_ref, wgu_bf, wdn_bf):
    i = pl.program_id(0)
    n_used = nused_ref[0]

    @pl.when(i < n_used)
    def _():
        @pl.when(jnp.logical_or(i == 0, be_ref[i] != be_ref[jnp.maximum(i - 1, 0)]))
        def _():
            wgu_bf[...] = wgu_ref[...].astype(BF16)
            wdn_bf[...] = wdn_ref[...].astype(BF16)

        xb = jnp.concatenate(_load_row_tile_cols(x_ref, MOE_BLOCK), axis=1).astype(BF16)
        hh = _dot(xb, wgu_bf[...]) + bgu_ref[...]
        gh = jnp.minimum(hh[:, :D_FF], SWIGLU_LIMIT)
        uh = jnp.clip(hh[:, D_FF:], -SWIGLU_LIMIT, SWIGLU_LIMIT)
        act = (uh + 1.0) * (gh * _sigmoid(SWIGLU_ALPHA * gh))
        _store_row_tiles(out_ref, _dot(act.astype(BF16), wdn_bf[...]) + bdn_ref[...])

    @pl.when(i >= n_used)
    def _():
        out_ref[...] = jnp.zeros_like(out_ref)


def _experts(block_expert, n_used, xs_rt, wgu, bgu3, wdn, bdn3):
    n_blocks = block_expert.shape[0]
    blk = MOE_BLOCK * ROW_SUBTILES
    grid_spec = pltpu.PrefetchScalarGridSpec(
        num_scalar_prefetch=2,
        grid=(n_blocks,),
        in_specs=[
            pl.BlockSpec((blk, LANES), lambda i, be, nu: (i, 0)),
            pl.BlockSpec((None, D_MODEL, 2 * D_FF), lambda i, be, nu: (be[i], 0, 0)),
            pl.BlockSpec((None, 1, 2 * D_FF), lambda i, be, nu: (be[i], 0, 0)),
            pl.BlockSpec((None, D_FF, D_MODEL), lambda i, be, nu: (be[i], 0, 0)),
            pl.BlockSpec((None, 1, D_MODEL), lambda i, be, nu: (be[i], 0, 0)),
        ],
        out_specs=pl.BlockSpec((blk, LANES), lambda i, be, nu: (i, 0)),
        scratch_shapes=[pltpu.VMEM((D_MODEL, 2 * D_FF), BF16),
                        pltpu.VMEM((D_FF, D_MODEL), BF16)],
    )
    return pl.pallas_call(
        _experts_kernel,
        grid_spec=grid_spec,
        out_shape=jax.ShapeDtypeStruct((n_blocks * blk, LANES), F32),
        compiler_params=_cparams(("arbitrary",)),
        name="experts",
    )(block_expert, n_used, xs_rt, wgu, bgu3, wdn, bdn3)


def _combine_kernel(pos_cur, pos_next, outs_hbm, h_ref, meta_ref, nw_ref, y_ref, buf, sem):
    i = pl.program_id(0)
    n = pl.num_programs(0)
    tc = h_ref.shape[0]
    slot = i % 2

    def issue(pos_ref, s):
        def body(r, carry):
            for kk in range(TOP_K):
                _row_copy(outs_hbm, pos_ref[0, 0, r * TOP_K + kk], buf.at[s, kk], r, sem.at[s]).start()
            return carry
        lax.fori_loop(0, tc, body, 0, unroll=4)

    @pl.when(i == 0)
    def _():
        issue(pos_cur, 0)

    @pl.when(i + 1 < n)
    def _():
        issue(pos_next, 1 - slot)

    for kk in range(TOP_K):
        _wait_row_copies(outs_hbm, buf.at[slot, kk], tc, sem.at[slot])
    meta = meta_ref[...]
    gates = [meta[:, META_GATE + kk:META_GATE + kk + 1] for kk in range(TOP_K)]
    rows = [_load_row_tile_cols(buf.at[slot, kk], tc) for kk in range(TOP_K)]
    cols = []
    for s in range(ROW_SUBTILES):
        acc = h_ref[:, s * LANES:(s + 1) * LANES]
        for kk in range(TOP_K):
            acc = acc + gates[kk] * rows[kk][s]
        cols.append(acc)
    y_ref[...] = _rms(jnp.concatenate(cols, axis=1), nw_ref[...])


def _combine(pos_flat, outs_rt, h, meta, nw):
    t = h.shape[0]
    tc = min(COMBINE_TILE, t)
    n = t // tc
    pos3 = pos_flat.reshape(n, 1, TOP_K * tc)
    return pl.pallas_call(
        _combine_kernel,
        grid=(n,),
        in_specs=[
            pl.BlockSpec((1, 1, TOP_K * tc), lambda i: (i, 0, 0), memory_space=pltpu.SMEM),
            pl.BlockSpec((1, 1, TOP_K * tc), lambda i: (jnp.minimum(i + 1, n - 1), 0, 0),
                         memory_space=pltpu.SMEM),
            pl.BlockSpec(memory_space=pl.ANY),
            pl.BlockSpec((tc, D_MODEL), lambda i: (i, 0)),
            pl.BlockSpec((tc, LANES), lambda i: (i, 0)),
            pl.BlockSpec((1, D_MODEL), lambda i: (0, 0)),
        ],
        out_specs=pl.BlockSpec((tc, D_MODEL), lambda i: (i, 0)),
        out_shape=jax.ShapeDtypeStruct((t, D_MODEL), F32),
        scratch_shapes=[pltpu.VMEM((2, TOP_K, tc * ROW_SUBTILES, LANES), F32), pltpu.SemaphoreType.DMA((2,))],
        compiler_params=_cparams(("arbitrary",)),
        name="combine",
    )(pos3, pos3, outs_rt, h, meta, nw)


def _pad_lanes(v, fill=0.0):
    v = v.astype(F32).reshape(1, -1)
    return jnp.pad(v, ((0, 0), (0, LANES - v.shape[1])), constant_values=fill)


def kernel(x, norm_mix_w, w_in, conv_w, conv_b, dt_bias, a_log, d_skip, ssm_norm_w, lstm_i_bias,
           lstm_f_bias, lstm_norm_w, w_out, norm_ffn_w, w_router, b_router, w_gate_up, b_gate_up,
           w_down, b_down, norm_final_w):
    b, s, d = x.shape
    t = b * s
    x2 = x.reshape(t, d).astype(F32)
    depth = w_in.shape[0]
    assert depth == 1, "the combine kernel fuses the final norm, so exactly one layer is supported"
    for layer in range(depth):
        wi = w_in[layer]
        c0 = SSM_WIDTH
        c1 = c0 + CONV_DIM
        c2 = c1 + SSM_HEADS
        c3 = c2 + 4 * LSTM_WIDTH
        c4 = c3 + LSTM_HEADS
        padc = lambda w: jnp.pad(w, ((0, 0), (0, LANES - w.shape[1])))
        w_all = jnp.concatenate([wi[:, :c1], wi[:, c2:c3], padc(wi[:, c1:c2]), padc(wi[:, c3:c4]),
                                 padc(wi[:, c4:])], axis=1).astype(BF16)
        sel = (jnp.arange(LANES)[:, None] == (jnp.arange(SSM_WIDTH) // SSM_HEAD_DIM)[None, :]).astype(BF16)
        a_neg = _pad_lanes(-jnp.exp(a_log[layer].astype(F32)))
        dskip_x = jnp.repeat(d_skip[layer].astype(F32), SSM_HEAD_DIM).reshape(1, SSM_WIDTH)
        wr = jnp.pad(w_router[layer], ((0, 0), (0, LANES - N_EXPERTS))).astype(BF16)
        br = _pad_lanes(b_router[layer], fill=NEG_INF)

        z, xbc, q, k, v, o, gates = _inproj(x2, norm_mix_w[layer].reshape(1, d).astype(F32), w_all)
        y_ssd = _ssd(xbc, z, gates, conv_w[layer].astype(F32), conv_b[layer].reshape(1, -1).astype(F32),
                     _pad_lanes(dt_bias[layer]), a_neg, dskip_x,
                     ssm_norm_w[layer].reshape(1, -1).astype(F32), sel, b, s)
        y_lstm = _mlstm(q, k, v, o, gates, _pad_lanes(lstm_i_bias[layer]), _pad_lanes(lstm_f_bias[layer]),
                        lstm_norm_w[layer].reshape(1, -1).astype(F32), b, s)
        h, hn_rt, meta, cnt = _outproj(y_ssd, y_lstm, x2, w_out[layer].astype(BF16),
                                       norm_ffn_w[layer].reshape(1, d).astype(F32), wr, br)

        idx = meta[:, META_IDX:META_IDX + TOP_K].astype(jnp.int32)
        rank = meta[:, META_RANK:META_RANK + TOP_K].astype(jnp.int32)
        counts = cnt[0, :N_EXPERTS].astype(jnp.int32)
        n_blocks = -(-(t * TOP_K) // MOE_BLOCK) + N_EXPERTS
        padded = (counts + MOE_BLOCK - 1) // MOE_BLOCK * MOE_BLOCK
        padded_ends = jnp.cumsum(padded)
        padded_starts = padded_ends - padded
        onehot = idx[..., None] == jnp.arange(N_EXPERTS, dtype=jnp.int32)
        pos = (jnp.sum(jnp.where(onehot, padded_starts, 0), axis=-1) + rank).reshape(-1)
        block_start = jnp.arange(n_blocks, dtype=jnp.int32) * MOE_BLOCK
        block_expert = jnp.minimum(
            jnp.sum((padded_ends[None, :] <= block_start[:, None]).astype(jnp.int32), axis=1), N_EXPERTS - 1)
        n_used = padded_ends[-1:] // MOE_BLOCK
        pad_len = padded - counts
        misc = jnp.concatenate([jnp.sum(pad_len, keepdims=True), n_used,
                                jnp.full((1,), n_blocks, jnp.int32)])

        xs_rt = _dispatch(padded_starts + counts, pad_len, misc, pos, hn_rt, n_blocks * MOE_BLOCK)
        outs_rt = _experts(block_expert, n_used, xs_rt, w_gate_up[layer],
                           b_gate_up[layer].reshape(N_EXPERTS, 1, -1), w_down[layer],
                           b_down[layer].reshape(N_EXPERTS, 1, -1))
        x2 = _combine(pos, outs_rt, h, meta, norm_final_w.reshape(1, d).astype(F32))
    return x2.reshape(b, s, d).astype(x.dtype)
```

```python
import functools

import jax
import jax.numpy as jnp
import numpy as np
from jax import lax
from jax.experimental import pallas as pl
from jax.experimental.pallas import tpu as pltpu

F32 = jnp.float32
BF16 = jnp.bfloat16

D_MODEL = 1024
SSM_WIDTH = 1024
SSM_HEAD_DIM = 64
SSM_HEADS = 16
SSM_GROUPS = 2
SSM_STATE = 128
CONV_WIDTH = 4
CONV_DIM = SSM_WIDTH + 2 * SSM_GROUPS * SSM_STATE
LSTM_WIDTH = 1024
LSTM_HEAD_DIM = 128
LSTM_HEADS = 8
N_EXPERTS = 32
TOP_K = 4
D_FF = 1024
SWIGLU_LIMIT = 7.0
SWIGLU_ALPHA = 1.702
MOE_BLOCK = 256
RMS_EPS = 1e-6

LANES = 128
SUBLANES = 8
MIX_CHUNK = 256
ROW_TILE = 512
COMBINE_TILE = 256
VMEM_LIMIT = 56 * 1024 * 1024

GATE_COLS = 3 * LANES
NEG_INF = float("-inf")


def _cparams(sem):
    return pltpu.CompilerParams(dimension_semantics=sem, vmem_limit_bytes=VMEM_LIMIT)


def _rms(x, w):
    return x * lax.rsqrt(jnp.mean(x * x, axis=-1, keepdims=True) + RMS_EPS) * w


def _sigmoid(x):
    return 1.0 / (1.0 + jnp.exp(-x))


def _softplus(x):
    return jnp.maximum(x, 0.0) + jnp.log(1.0 + jnp.exp(-jnp.abs(x)))


def _split3(a):
    hi = a.astype(BF16)
    r = a - hi.astype(F32)
    mid = r.astype(BF16)
    lo = (r - mid.astype(F32)).astype(BF16)
    return hi, mid, lo


def _dot(a, b):
    return jnp.dot(a, b, preferred_element_type=F32)


def _dot_nt(a, b):
    return lax.dot_general(a, b, (((1,), (1,)), ((), ())), preferred_element_type=F32)


def _dot_tn(a, b):
    return lax.dot_general(a, b, (((0,), (0,)), ((), ())), preferred_element_type=F32)


def _sel_dot(sel_bf, a):
    hi, mid, lo = _split3(a)
    return _dot(sel_bf, hi) + _dot(sel_bf, mid) + _dot(sel_bf, lo)


def _expand(a, sel_bf):
    hi, mid, lo = _split3(a)
    return _dot(hi, sel_bf) + _dot(mid, sel_bf) + _dot(lo, sel_bf)


ROW_SUBTILES = D_MODEL // LANES
DMA_PRIORITIES = 2


def _store_row_tiles(ref, x):
    n = x.shape[0]
    for s in range(ROW_SUBTILES):
        ref[pl.ds(s, n, stride=ROW_SUBTILES), :] = x[:, s * LANES:(s + 1) * LANES]


def _load_row_tile_cols(ref, n):
    return [ref[pl.ds(s, n, stride=ROW_SUBTILES), :] for s in range(ROW_SUBTILES)]


def _row_copy(src, src_row, dst, dst_row, sem):
    return pltpu.make_async_copy(
        src.at[pl.ds(pl.multiple_of(src_row * ROW_SUBTILES, ROW_SUBTILES), ROW_SUBTILES), :],
        dst.at[pl.ds(pl.multiple_of(dst_row * ROW_SUBTILES, ROW_SUBTILES), ROW_SUBTILES), :], sem)


def _wait_row_copies(src, dst, n_rows, sem):
    size = n_rows * ROW_SUBTILES
    pltpu.make_async_copy(src.at[pl.ds(0, size), :], dst.at[pl.ds(0, size), :], sem).wait()


_INPROJ_WIDTHS = (SSM_WIDTH, CONV_DIM, LSTM_WIDTH, LSTM_WIDTH, LSTM_WIDTH, LSTM_WIDTH, GATE_COLS)


def _inproj_kernel(x_ref, nw_ref, w_ref, z_ref, xbc_ref, q_ref, k_ref, v_ref, o_ref, g_ref):
    xb = _rms(x_ref[...], nw_ref[...]).astype(BF16)
    off = 0
    for ref, width in zip((z_ref, xbc_ref, q_ref, k_ref, v_ref, o_ref, g_ref), _INPROJ_WIDTHS):
        ref[...] = _dot(xb, w_ref[:, off:off + width]).astype(ref.dtype)
        off += width


def _inproj(x2, nw, w_all):
    t = x2.shape[0]
    tm = min(ROW_TILE, t)
    ncol = w_all.shape[1]
    out_shape = [jax.ShapeDtypeStruct((t, w), BF16) for w in _INPROJ_WIDTHS[:-1]]
    out_shape.append(jax.ShapeDtypeStruct((t, GATE_COLS), F32))
    return pl.pallas_call(
        _inproj_kernel,
        grid=(t // tm,),
        in_specs=[
            pl.BlockSpec((tm, D_MODEL), lambda i: (i, 0)),
            pl.BlockSpec((1, D_MODEL), lambda i: (0, 0)),
            pl.BlockSpec((D_MODEL, ncol), lambda i: (0, 0), pipeline_mode=pl.Buffered(1)),
        ],
        out_specs=[pl.BlockSpec((tm, w), lambda i: (i, 0)) for w in _INPROJ_WIDTHS],
        out_shape=out_shape,
        compiler_params=_cparams(("arbitrary",)),
        name="inproj",
    )(x2, nw, w_all)


def _ssd_kernel(xbc_ref, z_ref, g_ref, convw_ref, convb_ref, dtb_ref, aneg_ref, dskip_ref, nw_ref,
                sel_ref, y_ref, u_scr, st_scr):
    L = xbc_ref.shape[0]
    gw = SSM_WIDTH // SSM_GROUPS

    @pl.when(pl.program_id(1) == 0)
    def _():
        u_scr[0:SUBLANES, :] = jnp.zeros((SUBLANES, CONV_DIM), F32)
        st_scr[...] = jnp.zeros_like(st_scr)

    u_scr[SUBLANES:SUBLANES + L, :] = xbc_ref[...].astype(F32)
    acc = jnp.broadcast_to(convb_ref[...], (L, CONV_DIM))
    for j in range(CONV_WIDTH):
        back = CONV_WIDTH - 1 - j
        acc = acc + convw_ref[j:j + 1, :] * u_scr[SUBLANES - back:SUBLANES - back + L, :]
    u_scr[0:SUBLANES, :] = u_scr[L:L + SUBLANES, :]
    xbc = acc * _sigmoid(acc)
    xs = xbc[:, :SSM_WIDTH]
    xs_bf = xs.astype(BF16)
    bm = xbc[:, SSM_WIDTH:SSM_WIDTH + SSM_GROUPS * SSM_STATE].astype(BF16)
    cm = xbc[:, SSM_WIDTH + SSM_GROUPS * SSM_STATE:].astype(BF16)

    row = lax.broadcasted_iota(jnp.int32, (L, L), 0)
    col = lax.broadcasted_iota(jnp.int32, (L, L), 1)
    causal = row >= col
    tril_bf = causal.astype(F32).astype(BF16)

    lane = lax.broadcasted_iota(jnp.int32, (1, LANES), 1)
    dt = jnp.where(lane < SSM_HEADS, _softplus(g_ref[...] + dtb_ref[...]), 0.0)
    acs = _sel_dot(tril_bf, dt * aneg_ref[...])
    acs_t = acs.T
    dt_t = dt.T
    sel = sel_ref[...]
    exp_acs = jnp.exp(acs)
    acs_last = acs[L - 1:L, :]
    dt_x = _expand(dt, sel)
    ea_x = _expand(exp_acs, sel)
    de_x = _expand(jnp.exp(acs_last - acs), sel)
    xw = (xs * dt_x * de_x).astype(BF16)

    pair_lane = lax.broadcasted_iota(jnp.int32, (1, LANES), 1)
    ydiag = []
    for j in range(SSM_HEADS // 2):
        g = (2 * j) // (SSM_HEADS // SSM_GROUPS)
        cb = _dot_nt(cm[:, g * SSM_STATE:(g + 1) * SSM_STATE], bm[:, g * SSM_STATE:(g + 1) * SSM_STATE])
        ms = []
        for h in (2 * j, 2 * j + 1):
            seg = acs[:, h:h + 1] - acs_t[h:h + 1, :]
            dec = jnp.exp(jnp.where(causal, seg, NEG_INF))
            ms.append((cb * dec * dt_t[h:h + 1, :]).astype(BF16))
        xpair = xs_bf[:, j * LANES:(j + 1) * LANES]
        zero = jnp.zeros_like(xpair)
        rhs = jnp.concatenate([jnp.where(pair_lane < SSM_HEAD_DIM, xpair, zero),
                               jnp.where(pair_lane >= SSM_HEAD_DIM, xpair, zero)], axis=0)
        ydiag.append(_dot(jnp.concatenate(ms, axis=1), rhs))
    y = jnp.concatenate(ydiag, axis=1)

    yoff = []
    for g in range(SSM_GROUPS):
        st = st_scr[g]
        yoff.append(_dot(cm[:, g * SSM_STATE:(g + 1) * SSM_STATE], st.astype(BF16)))
        upd = _dot_tn(bm[:, g * SSM_STATE:(g + 1) * SSM_STATE], xw[:, g * gw:(g + 1) * gw])
        st_scr[g] = st * ea_x[L - 1:L, g * gw:(g + 1) * gw] + upd
    y = y + jnp.concatenate(yoff, axis=1) * ea_x + dskip_ref[...] * xs

    zz = z_ref[...].astype(F32)
    y = y * (zz * _sigmoid(zz))
    y_ref[...] = _rms(y, nw_ref[...]).astype(BF16)


def _ssd(xbc, z, gates, convw, convb, dtb, aneg, dskip_x, nw, sel, b, s):
    L = min(MIX_CHUNK, s)
    nc = s // L
    tok = lambda bi, ci: (bi * nc + ci, 0)
    const = lambda bi, ci: (0, 0)
    return pl.pallas_call(
        _ssd_kernel,
        grid=(b, nc),
        in_specs=[
            pl.BlockSpec((L, CONV_DIM), tok),
            pl.BlockSpec((L, SSM_WIDTH), tok),
            pl.BlockSpec((L, LANES), tok),
            pl.BlockSpec((CONV_WIDTH, CONV_DIM), const),
            pl.BlockSpec((1, CONV_DIM), const),
            pl.BlockSpec((1, LANES), const),
            pl.BlockSpec((1, LANES), const),
            pl.BlockSpec((1, SSM_WIDTH), const),
            pl.BlockSpec((1, SSM_WIDTH), const),
            pl.BlockSpec((LANES, SSM_WIDTH), const),
        ],
        out_specs=pl.BlockSpec((L, SSM_WIDTH), tok),
        out_shape=jax.ShapeDtypeStruct((b * s, SSM_WIDTH), BF16),
        scratch_shapes=[pltpu.VMEM((L + 2 * SUBLANES, CONV_DIM), F32),
                        pltpu.VMEM((SSM_GROUPS, SSM_STATE, SSM_WIDTH // SSM_GROUPS), F32)],
        compiler_params=_cparams(("arbitrary", "arbitrary")),
        name="ssd",
    )(xbc, z, gates, convw, convb, dtb, aneg, dskip_x, nw, sel)


def _mlstm_kernel(q_ref, k_ref, v_ref, o_ref, gi_ref, gf_ref, ib_ref, fb_ref, nw_ref, y_ref,
                  st_scr, m_scr):
    L = q_ref.shape[0]
    dh = LSTM_HEAD_DIM
    scale = dh ** -0.5

    @pl.when(pl.program_id(1) == 0)
    def _():
        st_scr[...] = jnp.zeros_like(st_scr)
        m_scr[...] = jnp.zeros_like(m_scr)

    row = lax.broadcasted_iota(jnp.int32, (L, L), 0)
    col = lax.broadcasted_iota(jnp.int32, (L, L), 1)
    causal = row >= col
    tril_bf = causal.astype(F32).astype(BF16)
    lane = lax.broadcasted_iota(jnp.int32, (1, LANES), 1)
    live = lane < LSTM_HEADS

    ii = jnp.where(live, gi_ref[...] + ib_ref[...], 0.0)
    logf = jnp.where(live, -_softplus(-(gf_ref[...] + fb_ref[...])), 0.0)
    cumf = _sel_dot(tril_bf, logf)
    g = ii - cumf
    rid = lax.broadcasted_iota(jnp.int32, (L, LANES), 0)
    cmx = g
    step = 1
    while step < L:
        cmx = jnp.maximum(cmx, jnp.where(rid >= step, pltpu.roll(cmx, step, axis=0), NEG_INF))
        step *= 2
    m_prev = m_scr[...]
    mx = jnp.maximum(m_prev, cmx)
    w_inter = jnp.exp(m_prev - mx)
    enm = jnp.exp(-(cumf + mx))
    g_t = g.T
    m_last = mx[L - 1:L, :]
    wk = jnp.exp(g - m_last) * scale
    sc = jnp.exp(m_prev - m_last)
    m_scr[...] = cumf[L - 1:L, :] + m_last

    ones_bf = jnp.ones((L, dh), BF16)
    for h in range(LSTM_HEADS):
        hs = slice(h * dh, (h + 1) * dh)
        qh = q_ref[:, hs]
        kh = k_ref[:, hs]
        vaug = jnp.concatenate([v_ref[:, hs], ones_bf], axis=1)
        dm = jnp.exp(jnp.where(causal, g_t[h:h + 1, :] - mx[:, h:h + 1], NEG_INF))
        p = (_dot_nt(qh, kh) * scale * dm).astype(BF16)
        r1 = _dot(p, vaug)
        st = st_scr[h]
        r2 = _dot(qh, st.astype(BF16))
        wcol = w_inter[:, h:h + 1]
        num = r1[:, :dh] + r2[:, :dh] * wcol
        den = r1[:, dh:] + r2[:, dh:] * wcol
        den = jnp.maximum(jnp.abs(den), enm[:, h:h + 1])
        hh = _rms(num / den, nw_ref[:, hs])
        oo = o_ref[:, hs].astype(F32)
        y_ref[:, hs] = (_sigmoid(oo) * hh).astype(BF16)
        kw = (kh.astype(F32) * wk[:, h:h + 1]).astype(BF16)
        st_scr[h] = st * sc[:, h:h + 1] + _dot_tn(kw, vaug)


def _mlstm(q, k, v, o, gates, ib, fb, nw, b, s):
    L = min(MIX_CHUNK, s)
    nc = s // L
    tok = lambda bi, ci: (bi * nc + ci, 0)
    const = lambda bi, ci: (0, 0)
    big = pl.BlockSpec((L, LSTM_WIDTH), tok)
    return pl.pallas_call(
        _mlstm_kernel,
        grid=(b, nc),
        in_specs=[big, big, big, big,
                  pl.BlockSpec((L, LANES), lambda bi, ci: (bi * nc + ci, 1)),
                  pl.BlockSpec((L, LANES), lambda bi, ci: (bi * nc + ci, 2)),
                  pl.BlockSpec((1, LANES), const),
                  pl.BlockSpec((1, LANES), const),
                  pl.BlockSpec((1, LSTM_WIDTH), const)],
        out_specs=big,
        out_shape=jax.ShapeDtypeStruct((b * s, LSTM_WIDTH), BF16),
        scratch_shapes=[pltpu.VMEM((LSTM_HEADS, LSTM_HEAD_DIM, 2 * LSTM_HEAD_DIM), F32),
                        pltpu.VMEM((1, LANES), F32)],
        compiler_params=_cparams(("arbitrary", "arbitrary")),
        name="mlstm",
    )(q, k, v, o, gates, gates, ib, fb, nw)


META_IDX, META_GATE, META_RANK = 0, TOP_K, 2 * TOP_K


def _outproj_kernel(ys_ref, yl_ref, x_ref, wo_ref, nw_ref, wr_ref, br_ref, h_ref, hn_ref, meta_ref,
                    cnt_ref, cnt_scr):
    tm = x_ref.shape[0]

    @pl.when(pl.program_id(0) == 0)
    def _():
        cnt_scr[...] = jnp.zeros_like(cnt_scr)

    h = (x_ref[...] + _dot(ys_ref[...], wo_ref[:SSM_WIDTH, :]) + _dot(yl_ref[...], wo_ref[SSM_WIDTH:, :]))
    h_ref[...] = h
    hn = _rms(h, nw_ref[...])
    _store_row_tiles(hn_ref, hn)
    vals = _dot(hn.astype(BF16), wr_ref[...]) + br_ref[...]

    lane = lax.broadcasted_iota(jnp.int32, (tm, LANES), 1)
    member = jnp.zeros((tm, LANES), F32)
    tops, idxs, sels = [], [], []
    for _ in range(TOP_K):
        m = jnp.max(vals, axis=-1, keepdims=True)
        idx = jnp.min(jnp.where(vals == m, lane, LANES), axis=-1, keepdims=True)
        sel = lane == idx
        vals = jnp.where(sel, NEG_INF, vals)
        member = member + sel.astype(F32)
        tops.append(m)
        idxs.append(idx)
        sels.append(sel)
    es = [jnp.exp(t - tops[0]) for t in tops]
    inv = 1.0 / (es[0] + es[1] + es[2] + es[3])

    r = lax.broadcasted_iota(jnp.int32, (tm, tm), 0)
    c = lax.broadcasted_iota(jnp.int32, (tm, tm), 1)
    strict = (r > c).astype(F32).astype(BF16)
    carry = cnt_scr[0:1, :]
    rank_all = _dot(strict, member.astype(BF16)) + carry
    total = carry + jnp.sum(member, axis=0, keepdims=True)
    cnt_scr[...] = jnp.broadcast_to(total, cnt_scr.shape)
    cnt_ref[...] = jnp.broadcast_to(total, cnt_ref.shape)

    meta = jnp.zeros((tm, LANES), F32)
    for kk in range(TOP_K):
        rank = jnp.sum(jnp.where(sels[kk], rank_all, 0.0), axis=-1, keepdims=True)
        meta = jnp.where(lane == META_IDX + kk, idxs[kk].astype(F32), meta)
        meta = jnp.where(lane == META_GATE + kk, es[kk] * inv, meta)
        meta = jnp.where(lane == META_RANK + kk, rank, meta)
    meta_ref[...] = meta


def _outproj(ys, yl, x2, wo, nw, wr, br):
    t = x2.shape[0]
    tm = min(ROW_TILE, t)
    tokspec = lambda w: pl.BlockSpec((tm, w), lambda i: (i, 0))
    const = lambda i: (0, 0)
    return pl.pallas_call(
        _outproj_kernel,
        grid=(t // tm,),
        in_specs=[tokspec(SSM_WIDTH), tokspec(LSTM_WIDTH), tokspec(D_MODEL),
                  pl.BlockSpec((SSM_WIDTH + LSTM_WIDTH, D_MODEL), const, pipeline_mode=pl.Buffered(1)),
                  pl.BlockSpec((1, D_MODEL), const),
                  pl.BlockSpec((D_MODEL, LANES), const),
                  pl.BlockSpec((1, LANES), const)],
        out_specs=[tokspec(D_MODEL), pl.BlockSpec((tm * ROW_SUBTILES, LANES), lambda i: (i, 0)), tokspec(LANES),
                   pl.BlockSpec((SUBLANES, LANES), const)],
        out_shape=[jax.ShapeDtypeStruct((t, D_MODEL), F32), jax.ShapeDtypeStruct((t * ROW_SUBTILES, LANES), F32),
                   jax.ShapeDtypeStruct((t, LANES), F32), jax.ShapeDtypeStruct((SUBLANES, LANES), F32)],
        scratch_shapes=[pltpu.VMEM((SUBLANES, LANES), F32)],
        compiler_params=_cparams(("arbitrary",)),
        name="outproj_router",
    )(ys, yl, x2, wo, nw, wr, br)


DISPATCH_TILE = 512
DISPATCH_GROUP = 4
PAD_CHUNKS = (128, 64, 32, 16, 8, 4, 2, 1)


def _dispatch_kernel(padstart_ref, padlen_ref, misc_ref, pos_ref, hn_ref, xs_hbm, sem, zsem, zeros_scr):
    i = pl.program_id(0)
    td = pos_ref.shape[-1] // TOP_K

    @pl.when(i == 0)
    def _():
        zeros_scr[...] = jnp.zeros_like(zeros_scr)
        for e in range(N_EXPERTS):
            row = padstart_ref[e]
            nrow = padlen_ref[e]
            for chunk in PAD_CHUNKS:
                @pl.when((nrow & chunk) != 0)
                def _(row=row, chunk=chunk):
                    pltpu.make_async_copy(
                        zeros_scr.at[pl.ds(0, chunk * ROW_SUBTILES), :],
                        xs_hbm.at[pl.ds(pl.multiple_of(row * ROW_SUBTILES, ROW_SUBTILES), chunk * ROW_SUBTILES), :],
                        zsem).start()
                row = row + (nrow & chunk)

        def tail_copy(blk):
            return pltpu.make_async_copy(
                zeros_scr,
                xs_hbm.at[pl.ds(pl.multiple_of(blk * (MOE_BLOCK * ROW_SUBTILES), ROW_SUBTILES),
                                MOE_BLOCK * ROW_SUBTILES), :], zsem)

        n_used, n_blocks = misc_ref[1], misc_ref[2]

        def start_tail(blk, carry):
            tail_copy(blk).start()
            return carry
        lax.fori_loop(n_used, n_blocks, start_tail, 0)

        def drain_row(j, carry):
            _wait_row_copies(zeros_scr, xs_hbm, 1, zsem)
            return carry
        lax.fori_loop(0, misc_ref[0], drain_row, 0)

        def drain_tail(blk, carry):
            tail_copy(blk).wait()
            return carry
        lax.fori_loop(n_used, n_blocks, drain_tail, 0)

    def body(g, carry):
        r0 = g * DISPATCH_GROUP
        slots = [pos_ref[0, 0, r0 * TOP_K + j] for j in range(DISPATCH_GROUP * TOP_K)]
        for j, slot in enumerate(slots):
            _row_copy(hn_ref, r0 + j // TOP_K, xs_hbm, slot, sem).start(priority=j % DMA_PRIORITIES)
        return carry
    lax.fori_loop(0, td // DISPATCH_GROUP, body, 0)

    for _ in range(TOP_K):
        _wait_row_copies(hn_ref, xs_hbm, td, sem)


def _dispatch(pad_start, pad_len, misc, pos_flat, hn_rt, n_rows):
    t = pos_flat.shape[0] // TOP_K
    td = min(DISPATCH_TILE, t)
    n = t // td
    grid_spec = pltpu.PrefetchScalarGridSpec(
        num_scalar_prefetch=3,
        grid=(n,),
        in_specs=[pl.BlockSpec((1, 1, TOP_K * td), lambda i, *_: (i, 0, 0), memory_space=pltpu.SMEM),
                  pl.BlockSpec((td * ROW_SUBTILES, LANES), lambda i, *_: (i, 0))],
        out_specs=pl.BlockSpec(memory_space=pl.ANY),
        scratch_shapes=[pltpu.SemaphoreType.DMA(()), pltpu.SemaphoreType.DMA(()),
                        pltpu.VMEM((MOE_BLOCK * ROW_SUBTILES, LANES), F32)],
    )
    return pl.pallas_call(
        _dispatch_kernel,
        grid_spec=grid_spec,
        out_shape=jax.ShapeDtypeStruct((n_rows * ROW_SUBTILES, LANES), F32),
        compiler_params=_cparams(("arbitrary",)),
        name="dispatch",
    )(pad_start, pad_len, misc, pos_flat.reshape(n, 1, TOP_K * td), hn_rt)


def _experts_kernel(be_ref, nused_ref, x_ref, wgu_ref, bgu_ref, wdn_ref, bdn_ref, out_ref, wgu_bf, wdn_bf):
    i = pl.program_id(0)
    n_used = nused_ref[0]

    @pl.when(i < n_used)
    def _():
        @pl.when(jnp.logical_or(i == 0, be_ref[i] != be_ref[jnp.maximum(i - 1, 0)]))
        def _():
            wgu_bf[...] = wgu_ref[...].astype(BF16)
            wdn_bf[...] = wdn_ref[...].astype(BF16)

        xb = jnp.concatenate(_load_row_tile_cols(x_ref, MOE_BLOCK), axis=1).astype(BF16)
        hh = _dot(xb, wgu_bf[...]) + bgu_ref[...]
        gh = jnp.minimum(hh[:, :D_FF], SWIGLU_LIMIT)
        uh = jnp.clip(hh[:, D_FF:], -SWIGLU_LIMIT, SWIGLU_LIMIT)
        act = (uh + 1.0) * (gh * _sigmoid(SWIGLU_ALPHA * gh))
        _store_row_tiles(out_ref, _dot(act.astype(BF16), wdn_bf[...]) + bdn_ref[...])

    @pl.when(i >= n_used)
    def _():
        out_ref[...] = jnp.zeros_like(out_ref)


def _experts(block_expert, n_used, xs_rt, wgu, bgu3, wdn, bdn3):
    n_blocks = block_expert.shape[0]
    blk = MOE_BLOCK * ROW_SUBTILES
    grid_spec = pltpu.PrefetchScalarGridSpec(
        num_scalar_prefetch=2,
        grid=(n_blocks,),
        in_specs=[
            pl.BlockSpec((blk, LANES), lambda i, be, nu: (i, 0)),
            pl.BlockSpec((None, D_MODEL, 2 * D_FF), lambda i, be, nu: (be[i], 0, 0)),
            pl.BlockSpec((None, 1, 2 * D_FF), lambda i, be, nu: (be[i], 0, 0)),
            pl.BlockSpec((None, D_FF, D_MODEL), lambda i, be, nu: (be[i], 0, 0)),
            pl.BlockSpec((None, 1, D_MODEL), lambda i, be, nu: (be[i], 0, 0)),
        ],
        out_specs=pl.BlockSpec((blk, LANES), lambda i, be, nu: (i, 0)),
        scratch_shapes=[pltpu.VMEM((D_MODEL, 2 * D_FF), BF16),
                        pltpu.VMEM((D_FF, D_MODEL), BF16)],
    )
    return pl.pallas_call(
        _experts_kernel,
        grid_spec=grid_spec,
        out_shape=jax.ShapeDtypeStruct((n_blocks * blk, LANES), F32),
        compiler_params=_cparams(("arbitrary",)),
        name="experts",
    )(block_expert, n_used, xs_rt, wgu, bgu3, wdn, bdn3)


def _combine_kernel(pos_cur, pos_next, outs_hbm, h_ref, meta_ref, nw_ref, y_ref, buf, sem):
    i = pl.program_id(0)
    n = pl.num_programs(0)
    tc = h_ref.shape[0]
    slot = i % 2

    def issue(pos_ref, s):
        def body(r, carry):
            for kk in range(TOP_K):
                _row_copy(outs_hbm, pos_ref[0, 0, r * TOP_K + kk], buf.at[s, kk], r,
                          sem.at[s]).start(priority=kk % DMA_PRIORITIES)
            return carry
        lax.fori_loop(0, tc, body, 0, unroll=4)

    @pl.when(i == 0)
    def _():
        issue(pos_cur, 0)

    @pl.when(i + 1 < n)
    def _():
        issue(pos_next, 1 - slot)

    for kk in range(TOP_K):
        _wait_row_copies(outs_hbm, buf.at[slot, kk], tc, sem.at[slot])
    meta = meta_ref[...]
    gates = [meta[:, META_GATE + kk:META_GATE + kk + 1] for kk in range(TOP_K)]
    rows = [_load_row_tile_cols(buf.at[slot, kk], tc) for kk in range(TOP_K)]
    cols = []
    for s in range(ROW_SUBTILES):
        acc = h_ref[:, s * LANES:(s + 1) * LANES]
        for kk in range(TOP_K):
            acc = acc + gates[kk] * rows[kk][s]
        cols.append(acc)
    y_ref[...] = _rms(jnp.concatenate(cols, axis=1), nw_ref[...])


def _combine(pos_flat, outs_rt, h, meta, nw):
    t = h.shape[0]
    tc = min(COMBINE_TILE, t)
    n = t // tc
    pos3 = pos_flat.reshape(n, 1, TOP_K * tc)
    return pl.pallas_call(
        _combine_kernel,
        grid=(n,),
        in_specs=[
            pl.BlockSpec((1, 1, TOP_K * tc), lambda i: (i, 0, 0), memory_space=pltpu.SMEM),
            pl.BlockSpec((1, 1, TOP_K * tc), lambda i: (jnp.minimum(i + 1, n - 1), 0, 0),
                         memory_space=pltpu.SMEM),
            pl.BlockSpec(memory_space=pl.ANY),
            pl.BlockSpec((tc, D_MODEL), lambda i: (i, 0)),
            pl.BlockSpec((tc, LANES), lambda i: (i, 0)),
            pl.BlockSpec((1, D_MODEL), lambda i: (0, 0)),
        ],
        out_specs=pl.BlockSpec((tc, D_MODEL), lambda i: (i, 0)),
        out_shape=jax.ShapeDtypeStruct((t, D_MODEL), F32),
        scratch_shapes=[pltpu.VMEM((2, TOP_K, tc * ROW_SUBTILES, LANES), F32), pltpu.SemaphoreType.DMA((2,))],
        compiler_params=_cparams(("arbitrary",)),
        name="combine",
    )(pos3, pos3, outs_rt, h, meta, nw)


def _pad_lanes(v, fill=0.0):
    v = v.astype(F32).reshape(1, -1)
    return jnp.pad(v, ((0, 0), (0, LANES - v.shape[1])), constant_values=fill)


def kernel(x, norm_mix_w, w_in, conv_w, conv_b, dt_bias, a_log, d_skip, ssm_norm_w, lstm_i_bias,
           lstm_f_bias, lstm_norm_w, w_out, norm_ffn_w, w_router, b_router, w_gate_up, b_gate_up,
           w_down, b_down, norm_final_w):
    b, s, d = x.shape
    t = b * s
    x2 = x.reshape(t, d).astype(F32)
    depth = w_in.shape[0]
    assert depth == 1, "the combine kernel fuses the final norm, so exactly one layer is supported"
    for layer in range(depth):
        wi = w_in[layer]
        c0 = SSM_WIDTH
        c1 = c0 + CONV_DIM
        c2 = c1 + SSM_HEADS
        c3 = c2 + 4 * LSTM_WIDTH
        c4 = c3 + LSTM_HEADS
        padc = lambda w: jnp.pad(w, ((0, 0), (0, LANES - w.shape[1])))
        w_all = jnp.concatenate([wi[:, :c1], wi[:, c2:c3], padc(wi[:, c1:c2]), padc(wi[:, c3:c4]),
                                 padc(wi[:, c4:])], axis=1).astype(BF16)
        sel = (jnp.arange(LANES)[:, None] == (jnp.arange(SSM_WIDTH) // SSM_HEAD_DIM)[None, :]).astype(BF16)
        a_neg = _pad_lanes(-jnp.exp(a_log[layer].astype(F32)))
        dskip_x = jnp.repeat(d_skip[layer].astype(F32), SSM_HEAD_DIM).reshape(1, SSM_WIDTH)
        wr = jnp.pad(w_router[layer], ((0, 0), (0, LANES - N_EXPERTS))).astype(BF16)
        br = _pad_lanes(b_router[layer], fill=NEG_INF)

        z, xbc, q, k, v, o, gates = _inproj(x2, norm_mix_w[layer].reshape(1, d).astype(F32), w_all)
        y_ssd = _ssd(xbc, z, gates, conv_w[layer].astype(F32), conv_b[layer].reshape(1, -1).astype(F32),
                     _pad_lanes(dt_bias[layer]), a_neg, dskip_x,
                     ssm_norm_w[layer].reshape(1, -1).astype(F32), sel, b, s)
        y_lstm = _mlstm(q, k, v, o, gates, _pad_lanes(lstm_i_bias[layer]), _pad_lanes(lstm_f_bias[layer]),
                        lstm_norm_w[layer].reshape(1, -1).astype(F32), b, s)
        h, hn_rt, meta, cnt = _outproj(y_ssd, y_lstm, x2, w_out[layer].astype(BF16),
                                       norm_ffn_w[layer].reshape(1, d).astype(F32), wr, br)

        idx = meta[:, META_IDX:META_IDX + TOP_K].astype(jnp.int32)
        rank = meta[:, META_RANK:META_RANK + TOP_K].astype(jnp.int32)
        counts = cnt[0, :N_EXPERTS].astype(jnp.int32)
        n_blocks = -(-(t * TOP_K) // MOE_BLOCK) + N_EXPERTS
        padded = (counts + MOE_BLOCK - 1) // MOE_BLOCK * MOE_BLOCK
        padded_ends = jnp.cumsum(padded)
        padded_starts = padded_ends - padded
        onehot = idx[..., None] == jnp.arange(N_EXPERTS, dtype=jnp.int32)
        pos = (jnp.sum(jnp.where(onehot, padded_starts, 0), axis=-1) + rank).reshape(-1)
        block_start = jnp.arange(n_blocks, dtype=jnp.int32) * MOE_BLOCK
        block_expert = jnp.minimum(
            jnp.sum((padded_ends[None, :] <= block_start[:, None]).astype(jnp.int32), axis=1), N_EXPERTS - 1)
        n_used = padded_ends[-1:] // MOE_BLOCK
        pad_len = padded - counts
        misc = jnp.concatenate([jnp.sum(pad_len, keepdims=True), n_used,
                                jnp.full((1,), n_blocks, jnp.int32)])

        xs_rt = _dispatch(padded_starts + counts, pad_len, misc, pos, hn_rt, n_blocks * MOE_BLOCK)
        outs_rt = _experts(block_expert, n_used, xs_rt, w_gate_up[layer],
                           b_gate_up[layer].reshape(N_EXPERTS, 1, -1), w_down[layer],
                           b_down[layer].reshape(N_EXPERTS, 1, -1))
        x2 = _combine(pos, outs_rt, h, meta, norm_final_w.reshape(1, d).astype(F32))
    return x2.reshape(b, s, d).astype(x.dtype)
```

```python
import functools

import jax
import jax.numpy as jnp
import numpy as np
from jax import lax
from jax.experimental import pallas as pl
from jax.experimental.pallas import tpu as pltpu

F32 = jnp.float32
BF16 = jnp.bfloat16

D_MODEL = 1024
SSM_WIDTH = 1024
SSM_HEAD_DIM = 64
SSM_HEADS = 16
SSM_GROUPS = 2
SSM_STATE = 128
CONV_WIDTH = 4
CONV_DIM = SSM_WIDTH + 2 * SSM_GROUPS * SSM_STATE
LSTM_WIDTH = 1024
LSTM_HEAD_DIM = 128
LSTM_HEADS = 8
N_EXPERTS = 32
TOP_K = 4
D_FF = 1024
SWIGLU_LIMIT = 7.0
SWIGLU_ALPHA = 1.702
MOE_BLOCK = 256
RMS_EPS = 1e-6

LANES = 128
SUBLANES = 8
MIX_CHUNK = 256
ROW_TILE = 512
COMBINE_TILE = 256
VMEM_LIMIT = 56 * 1024 * 1024

GATE_COLS = 3 * LANES
NEG_INF = float("-inf")


def _cparams(sem):
    return pltpu.CompilerParams(dimension_semantics=sem, vmem_limit_bytes=VMEM_LIMIT)


def _rms(x, w):
    return x * lax.rsqrt(jnp.mean(x * x, axis=-1, keepdims=True) + RMS_EPS) * w


LOG2E = 1.4426950408889634


def _sigmoid(x):
    return 1.0 / (1.0 + jnp.exp2(x * -LOG2E))


def _softplus(x):
    return jnp.maximum(x, 0.0) + jnp.log(1.0 + jnp.exp(-jnp.abs(x)))


def _split3(a):
    hi = a.astype(BF16)
    r = a - hi.astype(F32)
    mid = r.astype(BF16)
    lo = (r - mid.astype(F32)).astype(BF16)
    return hi, mid, lo


def _dot(a, b):
    return jnp.dot(a, b, preferred_element_type=F32)


def _dot_nt(a, b):
    return lax.dot_general(a, b, (((1,), (1,)), ((), ())), preferred_element_type=F32)


def _dot_tn(a, b):
    return lax.dot_general(a, b, (((0,), (0,)), ((), ())), preferred_element_type=F32)


def _sel_dot(sel_bf, a):
    hi, mid, lo = _split3(a)
    return _dot(sel_bf, hi) + _dot(sel_bf, mid) + _dot(sel_bf, lo)


def _expand(a, sel_bf, terms=3):
    out = None
    for piece in _split3(a)[:terms]:
        d = _dot(piece, sel_bf)
        out = d if out is None else out + d
    return out


ROW_SUBTILES = D_MODEL // LANES
DMA_PRIORITIES = 2


def _store_row_tiles(ref, x):
    n = x.shape[0]
    for s in range(ROW_SUBTILES):
        ref[pl.ds(s, n, stride=ROW_SUBTILES), :] = x[:, s * LANES:(s + 1) * LANES]


def _load_row_tile_cols(ref, n):
    return [ref[pl.ds(s, n, stride=ROW_SUBTILES), :] for s in range(ROW_SUBTILES)]


def _row_copy(src, src_row, dst, dst_row, sem):
    return pltpu.make_async_copy(
        src.at[pl.ds(pl.multiple_of(src_row * ROW_SUBTILES, ROW_SUBTILES), ROW_SUBTILES), :],
        dst.at[pl.ds(pl.multiple_of(dst_row * ROW_SUBTILES, ROW_SUBTILES), ROW_SUBTILES), :], sem)


def _wait_row_copies(src, dst, n_rows, sem):
    size = n_rows * ROW_SUBTILES
    pltpu.make_async_copy(src.at[pl.ds(0, size), :], dst.at[pl.ds(0, size), :], sem).wait()


_INPROJ_WIDTHS = (SSM_WIDTH, CONV_DIM, LSTM_WIDTH, LSTM_WIDTH, LSTM_WIDTH, LSTM_WIDTH, GATE_COLS)


def _inproj_kernel(x_ref, nw_ref, w_ref, z_ref, xbc_ref, q_ref, k_ref, v_ref, o_ref, g_ref):
    xb = _rms(x_ref[...], nw_ref[...]).astype(BF16)
    off = 0
    for ref, width in zip((z_ref, xbc_ref, q_ref, k_ref, v_ref, o_ref, g_ref), _INPROJ_WIDTHS):
        ref[...] = _dot(xb, w_ref[:, off:off + width]).astype(ref.dtype)
        off += width


def _inproj(x2, nw, w_all):
    t = x2.shape[0]
    tm = min(ROW_TILE, t)
    ncol = w_all.shape[1]
    out_shape = [jax.ShapeDtypeStruct((t, w), BF16) for w in _INPROJ_WIDTHS[:-1]]
    out_shape.append(jax.ShapeDtypeStruct((t, GATE_COLS), F32))
    return pl.pallas_call(
        _inproj_kernel,
        grid=(t // tm,),
        in_specs=[
            pl.BlockSpec((tm, D_MODEL), lambda i: (i, 0)),
            pl.BlockSpec((1, D_MODEL), lambda i: (0, 0)),
            pl.BlockSpec((D_MODEL, ncol), lambda i: (0, 0), pipeline_mode=pl.Buffered(1)),
        ],
        out_specs=[pl.BlockSpec((tm, w), lambda i: (i, 0)) for w in _INPROJ_WIDTHS],
        out_shape=out_shape,
        compiler_params=_cparams(("arbitrary",)),
        name="inproj",
    )(x2, nw, w_all)


def _ssd_kernel(xbc_ref, z_ref, g_ref, convw_ref, convb_ref, dtb_ref, aneg_ref, dskip_ref, nw_ref,
                sel_ref, y_ref, u_scr, st_scr):
    L = xbc_ref.shape[0]
    gw = SSM_WIDTH // SSM_GROUPS

    @pl.when(pl.program_id(1) == 0)
    def _():
        u_scr[...] = jnp.zeros_like(u_scr)
        st_scr[...] = jnp.zeros_like(st_scr)

    row = lax.broadcasted_iota(jnp.int32, (L, L), 0)
    col = lax.broadcasted_iota(jnp.int32, (L, L), 1)
    causal = row >= col

    xin_bf = xbc_ref[...]
    xin = xin_bf.astype(F32)
    halo = u_scr[...]
    row8 = lax.broadcasted_iota(jnp.int32, (SUBLANES, 1), 0)
    acc = convb_ref[...] + convw_ref[CONV_WIDTH - 1:CONV_WIDTH, :] * xin
    for back in range(1, CONV_WIDTH):
        shifted = _dot((row - col == back).astype(F32).astype(BF16), xin_bf)
        head = shifted[:SUBLANES] + jnp.where(row8 < back, pltpu.roll(halo, back, axis=0), 0.0)
        shifted = jnp.concatenate([head, shifted[SUBLANES:]], axis=0)
        acc = acc + convw_ref[CONV_WIDTH - 1 - back:CONV_WIDTH - back, :] * shifted
    u_scr[...] = xin[L - SUBLANES:, :]
    xbc = acc * _sigmoid(acc)
    xs = xbc[:, :SSM_WIDTH]
    bm = xbc[:, SSM_WIDTH:SSM_WIDTH + SSM_GROUPS * SSM_STATE].astype(BF16)
    cm = xbc[:, SSM_WIDTH + SSM_GROUPS * SSM_STATE:].astype(BF16)

    tril_bf = causal.astype(F32).astype(BF16)

    lane = lax.broadcasted_iota(jnp.int32, (1, LANES), 1)
    dt = jnp.where(lane < SSM_HEADS, _softplus(g_ref[...] + dtb_ref[...]), 0.0)
    acs = _sel_dot(tril_bf, dt * aneg_ref[...]) * LOG2E
    acs_t = acs.T
    sel = sel_ref[...]
    acs_last = acs[L - 1:L, :]
    ea_x = _expand(jnp.exp2(acs), sel, terms=2)
    xd = xs * _expand(dt, sel, terms=1)
    xd_bf = xd.astype(BF16)
    xw = (xd * _expand(jnp.exp2(acs_last - acs), sel, terms=1)).astype(BF16)

    cbs = [_dot_nt(cm[:, g * SSM_STATE:(g + 1) * SSM_STATE], bm[:, g * SSM_STATE:(g + 1) * SSM_STATE])
           for g in range(SSM_GROUPS)]
    yoff = [_dot(cm[:, g * SSM_STATE:(g + 1) * SSM_STATE], st_scr[g].astype(BF16)) for g in range(SSM_GROUPS)]
    upd = [_dot_tn(bm[:, g * SSM_STATE:(g + 1) * SSM_STATE], xw[:, g * gw:(g + 1) * gw])
           for g in range(SSM_GROUPS)]
    for g in range(SSM_GROUPS):
        st_scr[g] = st_scr[g] * ea_x[L - 1:L, g * gw:(g + 1) * gw] + upd[g]

    pair_lane = lax.broadcasted_iota(jnp.int32, (1, LANES), 1)
    heads_per_group = SSM_HEADS // SSM_GROUPS
    ms = [(cbs[h // heads_per_group]
           * jnp.exp2(jnp.where(causal, acs[:, h:h + 1] - acs_t[h:h + 1, :], NEG_INF))).astype(BF16)
          for h in range(SSM_HEADS)]
    rhs = []
    for j in range(SSM_HEADS // 2):
        xpair = xd_bf[:, j * LANES:(j + 1) * LANES]
        zero = jnp.zeros_like(xpair)
        rhs.append(jnp.concatenate([jnp.where(pair_lane < SSM_HEAD_DIM, xpair, zero),
                                    jnp.where(pair_lane >= SSM_HEAD_DIM, xpair, zero)], axis=0))
    ydiag = [_dot(jnp.concatenate([ms[2 * j], ms[2 * j + 1]], axis=1), rhs[j]) for j in range(SSM_HEADS // 2)]
    y = jnp.concatenate(ydiag, axis=1) + jnp.concatenate(yoff, axis=1) * ea_x + dskip_ref[...] * xs

    zz = z_ref[...].astype(F32)
    y = y * (zz * _sigmoid(zz))
    y_ref[...] = _rms(y, nw_ref[...]).astype(BF16)


def _ssd(xbc, z, gates, convw, convb, dtb, aneg, dskip_x, nw, sel, b, s):
    L = min(MIX_CHUNK, s)
    nc = s // L
    tok = lambda bi, ci: (bi * nc + ci, 0)
    const = lambda bi, ci: (0, 0)
    return pl.pallas_call(
        _ssd_kernel,
        grid=(b, nc),
        in_specs=[
            pl.BlockSpec((L, CONV_DIM), tok),
            pl.BlockSpec((L, SSM_WIDTH), tok),
            pl.BlockSpec((L, LANES), tok),
            pl.BlockSpec((CONV_WIDTH, CONV_DIM), const),
            pl.BlockSpec((1, CONV_DIM), const),
            pl.BlockSpec((1, LANES), const),
            pl.BlockSpec((1, LANES), const),
            pl.BlockSpec((1, SSM_WIDTH), const),
            pl.BlockSpec((1, SSM_WIDTH), const),
            pl.BlockSpec((LANES, SSM_WIDTH), const),
        ],
        out_specs=pl.BlockSpec((L, SSM_WIDTH), tok),
        out_shape=jax.ShapeDtypeStruct((b * s, SSM_WIDTH), BF16),
        scratch_shapes=[pltpu.VMEM((SUBLANES, CONV_DIM), F32),
                        pltpu.VMEM((SSM_GROUPS, SSM_STATE, SSM_WIDTH // SSM_GROUPS), F32)],
        compiler_params=_cparams(("arbitrary", "arbitrary")),
        name="ssd",
    )(xbc, z, gates, convw, convb, dtb, aneg, dskip_x, nw, sel)


def _mlstm_kernel(q_ref, k_ref, v_ref, o_ref, gi_ref, gf_ref, ib_ref, fb_ref, nw_ref, y_ref,
                  st_scr, m_scr):
    L = q_ref.shape[0]
    dh = LSTM_HEAD_DIM
    scale = dh ** -0.5

    @pl.when(pl.program_id(1) == 0)
    def _():
        st_scr[...] = jnp.zeros_like(st_scr)
        m_scr[...] = jnp.zeros_like(m_scr)

    row = lax.broadcasted_iota(jnp.int32, (L, L), 0)
    col = lax.broadcasted_iota(jnp.int32, (L, L), 1)
    causal = row >= col
    tril_bf = causal.astype(F32).astype(BF16)
    lane = lax.broadcasted_iota(jnp.int32, (1, LANES), 1)
    live = lane < LSTM_HEADS

    ii = jnp.where(live, gi_ref[...] + ib_ref[...], 0.0)
    logf = jnp.where(live, -_softplus(-(gf_ref[...] + fb_ref[...])), 0.0)
    cumf = _sel_dot(tril_bf, logf)
    g = ii - cumf
    rid = lax.broadcasted_iota(jnp.int32, (L, LANES), 0)
    cmx = g
    step = 1
    while step < L:
        cmx = jnp.maximum(cmx, jnp.where(rid >= step, pltpu.roll(cmx, step, axis=0), NEG_INF))
        step *= 2
    m_prev = m_scr[...]
    mx = jnp.maximum(m_prev, cmx)
    w_inter = jnp.exp(m_prev - mx)
    enm = jnp.exp(-(cumf + mx))
    g2_t = (g * LOG2E).T
    mx2 = mx * LOG2E - np.log2(scale)
    m_last = mx[L - 1:L, :]
    wk = jnp.exp(g - m_last) * scale
    sc = jnp.exp(m_prev - m_last)
    m_scr[...] = cumf[L - 1:L, :] + m_last

    ones_bf = jnp.ones((L, dh), BF16)
    heads = range(LSTM_HEADS)
    hsl = [slice(h * dh, (h + 1) * dh) for h in heads]
    vaug = [jnp.concatenate([v_ref[:, hsl[h]], ones_bf], axis=1) for h in heads]
    r2 = [_dot(q_ref[:, hsl[h]], st_scr[h].astype(BF16)) for h in heads]
    upd = [_dot_tn((k_ref[:, hsl[h]].astype(F32) * wk[:, h:h + 1]).astype(BF16), vaug[h]) for h in heads]
    for h in heads:
        st_scr[h] = st_scr[h] * sc[:, h:h + 1] + upd[h]
    s_qk = [_dot_nt(q_ref[:, hsl[h]], k_ref[:, hsl[h]]) for h in heads]
    p = [(s_qk[h] * jnp.exp2(jnp.where(causal, g2_t[h:h + 1, :] - mx2[:, h:h + 1], NEG_INF))).astype(BF16)
         for h in heads]
    r1 = [_dot(p[h], vaug[h]) for h in heads]
    hh = []
    for h in heads:
        wcol = w_inter[:, h:h + 1]
        num = r1[h][:, :dh] + r2[h][:, :dh] * wcol
        den = r1[h][:, dh:] + r2[h][:, dh:] * wcol
        hh.append(num / jnp.maximum(jnp.abs(den), enm[:, h:h + 1]))
    inv = [lax.rsqrt(jnp.mean(hh[h] * hh[h], axis=-1, keepdims=True) + RMS_EPS) for h in heads]
    for h in heads:
        oo = o_ref[:, hsl[h]].astype(F32)
        y_ref[:, hsl[h]] = (_sigmoid(oo) * (hh[h] * inv[h] * nw_ref[:, hsl[h]])).astype(BF16)


def _mlstm(q, k, v, o, gates, ib, fb, nw, b, s):
    L = min(MIX_CHUNK, s)
    nc = s // L
    tok = lambda bi, ci: (bi * nc + ci, 0)
    const = lambda bi, ci: (0, 0)
    big = pl.BlockSpec((L, LSTM_WIDTH), tok)
    return pl.pallas_call(
        _mlstm_kernel,
        grid=(b, nc),
        in_specs=[big, big, big, big,
                  pl.BlockSpec((L, LANES), lambda bi, ci: (bi * nc + ci, 1)),
                  pl.BlockSpec((L, LANES), lambda bi, ci: (bi * nc + ci, 2)),
                  pl.BlockSpec((1, LANES), const),
                  pl.BlockSpec((1, LANES), const),
                  pl.BlockSpec((1, LSTM_WIDTH), const)],
        out_specs=big,
        out_shape=jax.ShapeDtypeStruct((b * s, LSTM_WIDTH), BF16),
        scratch_shapes=[pltpu.VMEM((LSTM_HEADS, LSTM_HEAD_DIM, 2 * LSTM_HEAD_DIM), F32),
                        pltpu.VMEM((1, LANES), F32)],
        compiler_params=_cparams(("arbitrary", "arbitrary")),
        name="mlstm",
    )(q, k, v, o, gates, gates, ib, fb, nw)


META_IDX, META_GATE, META_RANK = 0, TOP_K, 2 * TOP_K


def _outproj_kernel(ys_ref, yl_ref, x_ref, wo_ref, nw_ref, wr_ref, br_ref, h_ref, hn_ref, meta_ref,
                    cnt_ref, cnt_scr):
    tm = x_ref.shape[0]

    @pl.when(pl.program_id(0) == 0)
    def _():
        cnt_scr[...] = jnp.zeros_like(cnt_scr)

    h = (x_ref[...] + _dot(ys_ref[...], wo_ref[:SSM_WIDTH, :]) + _dot(yl_ref[...], wo_ref[SSM_WIDTH:, :]))
    h_ref[...] = h
    hn = _rms(h, nw_ref[...])
    _store_row_tiles(hn_ref, hn)
    vals = _dot(hn.astype(BF16), wr_ref[...]) + br_ref[...]

    lane = lax.broadcasted_iota(jnp.int32, (tm, LANES), 1)
    member = jnp.zeros((tm, LANES), F32)
    tops, idxs, sels = [], [], []
    for _ in range(TOP_K):
        m = jnp.max(vals, axis=-1, keepdims=True)
        idx = jnp.min(jnp.where(vals == m, lane, LANES), axis=-1, keepdims=True)
        sel = lane == idx
        vals = jnp.where(sel, NEG_INF, vals)
        member = member + sel.astype(F32)
        tops.append(m)
        idxs.append(idx)
        sels.append(sel)
    es = [jnp.exp(t - tops[0]) for t in tops]
    inv = 1.0 / (es[0] + es[1] + es[2] + es[3])

    r = lax.broadcasted_iota(jnp.int32, (tm, tm), 0)
    c = lax.broadcasted_iota(jnp.int32, (tm, tm), 1)
    strict = (r > c).astype(F32).astype(BF16)
    carry = cnt_scr[0:1, :]
    rank_all = _dot(strict, member.astype(BF16)) + carry
    total = carry + jnp.sum(member, axis=0, keepdims=True)
    cnt_scr[...] = jnp.broadcast_to(total, cnt_scr.shape)
    cnt_ref[...] = jnp.broadcast_to(total, cnt_ref.shape)

    meta = jnp.zeros((tm, LANES), F32)
    for kk in range(TOP_K):
        rank = jnp.sum(jnp.where(sels[kk], rank_all, 0.0), axis=-1, keepdims=True)
        meta = jnp.where(lane == META_IDX + kk, idxs[kk].astype(F32), meta)
        meta = jnp.where(lane == META_GATE + kk, es[kk] * inv, meta)
        meta = jnp.where(lane == META_RANK + kk, rank, meta)
    meta_ref[...] = meta


def _outproj(ys, yl, x2, wo, nw, wr, br):
    t = x2.shape[0]
    tm = min(ROW_TILE, t)
    tokspec = lambda w: pl.BlockSpec((tm, w), lambda i: (i, 0))
    const = lambda i: (0, 0)
    return pl.pallas_call(
        _outproj_kernel,
        grid=(t // tm,),
        in_specs=[tokspec(SSM_WIDTH), tokspec(LSTM_WIDTH), tokspec(D_MODEL),
                  pl.BlockSpec((SSM_WIDTH + LSTM_WIDTH, D_MODEL), const, pipeline_mode=pl.Buffered(1)),
                  pl.BlockSpec((1, D_MODEL), const),
                  pl.BlockSpec((D_MODEL, LANES), const),
                  pl.BlockSpec((1, LANES), const)],
        out_specs=[tokspec(D_MODEL), pl.BlockSpec((tm * ROW_SUBTILES, LANES), lambda i: (i, 0)), tokspec(LANES),
                   pl.BlockSpec((SUBLANES, LANES), const)],
        out_shape=[jax.ShapeDtypeStruct((t, D_MODEL), F32), jax.ShapeDtypeStruct((t * ROW_SUBTILES, LANES), F32),
                   jax.ShapeDtypeStruct((t, LANES), F32), jax.ShapeDtypeStruct((SUBLANES, LANES), F32)],
        scratch_shapes=[pltpu.VMEM((SUBLANES, LANES), F32)],
        compiler_params=_cparams(("arbitrary",)),
        name="outproj_router",
    )(ys, yl, x2, wo, nw, wr, br)


DISPATCH_TILE = 512
DISPATCH_GROUP = 4
PAD_CHUNKS = (128, 64, 32, 16, 8, 4, 2, 1)


def _dispatch_kernel(padstart_ref, padlen_ref, misc_ref, pos_ref, hn_ref, xs_hbm, sem, zsem, zeros_scr):
    i = pl.program_id(0)
    td = pos_ref.shape[-1] // TOP_K

    @pl.when(i == 0)
    def _():
        zeros_scr[...] = jnp.zeros_like(zeros_scr)
        for e in range(N_EXPERTS):
            row = padstart_ref[e]
            nrow = padlen_ref[e]
            for chunk in PAD_CHUNKS:
                @pl.when((nrow & chunk) != 0)
                def _(row=row, chunk=chunk):
                    pltpu.make_async_copy(
                        zeros_scr.at[pl.ds(0, chunk * ROW_SUBTILES), :],
                        xs_hbm.at[pl.ds(pl.multiple_of(row * ROW_SUBTILES, ROW_SUBTILES), chunk * ROW_SUBTILES), :],
                        zsem).start()
                row = row + (nrow & chunk)

        def tail_copy(blk):
            return pltpu.make_async_copy(
                zeros_scr,
                xs_hbm.at[pl.ds(pl.multiple_of(blk * (MOE_BLOCK * ROW_SUBTILES), ROW_SUBTILES),
                                MOE_BLOCK * ROW_SUBTILES), :], zsem)

        n_used, n_blocks = misc_ref[1], misc_ref[2]

        def start_tail(blk, carry):
            tail_copy(blk).start()
            return carry
        lax.fori_loop(n_used, n_blocks, start_tail, 0)

        def drain_row(j, carry):
            _wait_row_copies(zeros_scr, xs_hbm, 1, zsem)
            return carry
        lax.fori_loop(0, misc_ref[0], drain_row, 0)

        def drain_tail(blk, carry):
            tail_copy(blk).wait()
            return carry
        lax.fori_loop(n_used, n_blocks, drain_tail, 0)

    def body(g, carry):
        r0 = g * DISPATCH_GROUP
        slots = [pos_ref[0, 0, r0 * TOP_K + j] for j in range(DISPATCH_GROUP * TOP_K)]
        for j, slot in enumerate(slots):
            _row_copy(hn_ref, r0 + j // TOP_K, xs_hbm, slot, sem).start(priority=j % DMA_PRIORITIES)
        return carry
    lax.fori_loop(0, td // DISPATCH_GROUP, body, 0)

    for _ in range(TOP_K):
        _wait_row_copies(hn_ref, xs_hbm, td, sem)


def _dispatch(pad_start, pad_len, misc, pos_flat, hn_rt, n_rows):
    t = pos_flat.shape[0] // TOP_K
    td = min(DISPATCH_TILE, t)
    n = t // td
    grid_spec = pltpu.PrefetchScalarGridSpec(
        num_scalar_prefetch=3,
        grid=(n,),
        in_specs=[pl.BlockSpec((1, 1, TOP_K * td), lambda i, *_: (i, 0, 0), memory_space=pltpu.SMEM),
                  pl.BlockSpec((td * ROW_SUBTILES, LANES), lambda i, *_: (i, 0))],
        out_specs=pl.BlockSpec(memory_space=pl.ANY),
        scratch_shapes=[pltpu.SemaphoreType.DMA(()), pltpu.SemaphoreType.DMA(()),
                        pltpu.VMEM((MOE_BLOCK * ROW_SUBTILES, LANES), F32)],
    )
    return pl.pallas_call(
        _dispatch_kernel,
        grid_spec=grid_spec,
        out_shape=jax.ShapeDtypeStruct((n_rows * ROW_SUBTILES, LANES), F32),
        compiler_params=_cparams(("arbitrary",)),
        name="dispatch",
    )(pad_start, pad_len, misc, pos_flat.reshape(n, 1, TOP_K * td), hn_rt)


def _experts_kernel(be_ref, nused_ref, x_ref, wgu_ref, bgu_ref, wdn_ref, bdn_ref, out_ref, wgu_bf, wdn_bf):
    i = pl.program_id(0)
    n_used = nused_ref[0]

    @pl.when(i < n_used)
    def _():
        @pl.when(jnp.logical_or(i == 0, be_ref[i] != be_ref[jnp.maximum(i - 1, 0)]))
        def _():
            wgu_bf[...] = wgu_ref[...].astype(BF16)
            wdn_bf[...] = wdn_ref[...].astype(BF16)

        xb = jnp.concatenate(_load_row_tile_cols(x_ref, MOE_BLOCK), axis=1).astype(BF16)
        hh = _dot(xb, wgu_bf[...]) + bgu_ref[...]
        gh = jnp.minimum(hh[:, :D_FF], SWIGLU_LIMIT)
        uh = jnp.clip(hh[:, D_FF:], -SWIGLU_LIMIT, SWIGLU_LIMIT)
        act = (uh + 1.0) * (gh * _sigmoid(SWIGLU_ALPHA * gh))
        _store_row_tiles(out_ref, _dot(act.astype(BF16), wdn_bf[...]) + bdn_ref[...])

    @pl.when(i >= n_used)
    def _():
        out_ref[...] = jnp.zeros_like(out_ref)


def _experts(block_expert, n_used, xs_rt, wgu, bgu3, wdn, bdn3):
    n_blocks = block_expert.shape[0]
    blk = MOE_BLOCK * ROW_SUBTILES
    grid_spec = pltpu.PrefetchScalarGridSpec(
        num_scalar_prefetch=2,
        grid=(n_blocks,),
        in_specs=[
            pl.BlockSpec((blk, LANES), lambda i, be, nu: (i, 0)),
            pl.BlockSpec((None, D_MODEL, 2 * D_FF), lambda i, be, nu: (be[i], 0, 0)),
            pl.BlockSpec((None, 1, 2 * D_FF), lambda i, be, nu: (be[i], 0, 0)),
            pl.BlockSpec((None, D_FF, D_MODEL), lambda i, be, nu: (be[i], 0, 0)),
            pl.BlockSpec((None, 1, D_MODEL), lambda i, be, nu: (be[i], 0, 0)),
        ],
        out_specs=pl.BlockSpec((blk, LANES), lambda i, be, nu: (i, 0)),
        scratch_shapes=[pltpu.VMEM((D_MODEL, 2 * D_FF), BF16),
                        pltpu.VMEM((D_FF, D_MODEL), BF16)],
    )
    return pl.pallas_call(
        _experts_kernel,
        grid_spec=grid_spec,
        out_shape=jax.ShapeDtypeStruct((n_blocks * blk, LANES), F32),
        compiler_params=_cparams(("arbitrary",)),
        name="experts",
    )(block_expert, n_used, xs_rt, wgu, bgu3, wdn, bdn3)


def _combine_kernel(pos_cur, pos_next, outs_hbm, h_ref, meta_ref, nw_ref, y_ref, buf, sem):
    i = pl.program_id(0)
    n = pl.num_programs(0)
    tc = h_ref.shape[0]
    slot = i % 2

    def issue(pos_ref, s):
        def body(r, carry):
            for kk in range(TOP_K):
                _row_copy(outs_hbm, pos_ref[0, 0, r * TOP_K + kk], buf.at[s, kk], r,
                          sem.at[s]).start(priority=kk % DMA_PRIORITIES)
            return carry
        lax.fori_loop(0, tc, body, 0, unroll=4)

    @pl.when(i == 0)
    def _():
        issue(pos_cur, 0)

    @pl.when(i + 1 < n)
    def _():
        issue(pos_next, 1 - slot)

    for kk in range(TOP_K):
        _wait_row_copies(outs_hbm, buf.at[slot, kk], tc, sem.at[slot])
    meta = meta_ref[...]
    gates = [meta[:, META_GATE + kk:META_GATE + kk + 1] for kk in range(TOP_K)]
    rows = [_load_row_tile_cols(buf.at[slot, kk], tc) for kk in range(TOP_K)]
    cols = []
    for s in range(ROW_SUBTILES):
        acc = h_ref[:, s * LANES:(s + 1) * LANES]
        for kk in range(TOP_K):
            acc = acc + gates[kk] * rows[kk][s]
        cols.append(acc)
    y_ref[...] = _rms(jnp.concatenate(cols, axis=1), nw_ref[...])


def _combine(pos_flat, outs_rt, h, meta, nw):
    t = h.shape[0]
    tc = min(COMBINE_TILE, t)
    n = t // tc
    pos3 = pos_flat.reshape(n, 1, TOP_K * tc)
    return pl.pallas_call(
        _combine_kernel,
        grid=(n,),
        in_specs=[
            pl.BlockSpec((1, 1, TOP_K * tc), lambda i: (i, 0, 0), memory_space=pltpu.SMEM),
            pl.BlockSpec((1, 1, TOP_K * tc), lambda i: (jnp.minimum(i + 1, n - 1), 0, 0),
                         memory_space=pltpu.SMEM),
            pl.BlockSpec(memory_space=pl.ANY),
            pl.BlockSpec((tc, D_MODEL), lambda i: (i, 0)),
            pl.BlockSpec((tc, LANES), lambda i: (i, 0)),
            pl.BlockSpec((1, D_MODEL), lambda i: (0, 0)),
        ],
        out_specs=pl.BlockSpec((tc, D_MODEL), lambda i: (i, 0)),
        out_shape=jax.ShapeDtypeStruct((t, D_MODEL), F32),
        scratch_shapes=[pltpu.VMEM((2, TOP_K, tc * ROW_SUBTILES, LANES), F32), pltpu.SemaphoreType.DMA((2,))],
        compiler_params=_cparams(("arbitrary",)),
        name="combine",
    )(pos3, pos3, outs_rt, h, meta, nw)


def _pad_lanes(v, fill=0.0):
    v = v.astype(F32).reshape(1, -1)
    return jnp.pad(v, ((0, 0), (0, LANES - v.shape[1])), constant_values=fill)


def kernel(x, norm_mix_w, w_in, conv_w, conv_b, dt_bias, a_log, d_skip, ssm_norm_w, lstm_i_bias,
           lstm_f_bias, lstm_norm_w, w_out, norm_ffn_w, w_router, b_router, w_gate_up, b_gate_up,
           w_down, b_down, norm_final_w):
    b, s, d = x.shape
    t = b * s
    x2 = x.reshape(t, d).astype(F32)
    depth = w_in.shape[0]
    assert depth == 1, "the combine kernel fuses the final norm, so exactly one layer is supported"
    for layer in range(depth):
        wi = w_in[layer]
        c0 = SSM_WIDTH
        c1 = c0 + CONV_DIM
        c2 = c1 + SSM_HEADS
        c3 = c2 + 4 * LSTM_WIDTH
        c4 = c3 + LSTM_HEADS
        padc = lambda w: jnp.pad(w, ((0, 0), (0, LANES - w.shape[1])))
        w_all = jnp.concatenate([wi[:, :c1], wi[:, c2:c3], padc(wi[:, c1:c2]), padc(wi[:, c3:c4]),
                                 padc(wi[:, c4:])], axis=1).astype(BF16)
        sel = (jnp.arange(LANES)[:, None] == (jnp.arange(SSM_WIDTH) // SSM_HEAD_DIM)[None, :]).astype(BF16)
        a_neg = _pad_lanes(-jnp.exp(a_log[layer].astype(F32)))
        dskip_x = jnp.repeat(d_skip[layer].astype(F32), SSM_HEAD_DIM).reshape(1, SSM_WIDTH)
        wr = jnp.pad(w_router[layer], ((0, 0), (0, LANES - N_EXPERTS))).astype(BF16)
        br = _pad_lanes(b_router[layer], fill=NEG_INF)

        z, xbc, q, k, v, o, gates = _inproj(x2, norm_mix_w[layer].reshape(1, d).astype(F32), w_all)
        y_ssd = _ssd(xbc, z, gates, conv_w[layer].astype(F32), conv_b[layer].reshape(1, -1).astype(F32),
                     _pad_lanes(dt_bias[layer]), a_neg, dskip_x,
                     ssm_norm_w[layer].reshape(1, -1).astype(F32), sel, b, s)
        y_lstm = _mlstm(q, k, v, o, gates, _pad_lanes(lstm_i_bias[layer]), _pad_lanes(lstm_f_bias[layer]),
                        lstm_norm_w[layer].reshape(1, -1).astype(F32), b, s)
        h, hn_rt, meta, cnt = _outproj(y_ssd, y_lstm, x2, w_out[layer].astype(BF16),
                                       norm_ffn_w[layer].reshape(1, d).astype(F32), wr, br)

        idx = meta[:, META_IDX:META_IDX + TOP_K].astype(jnp.int32)
        rank = meta[:, META_RANK:META_RANK + TOP_K].astype(jnp.int32)
        counts = cnt[0, :N_EXPERTS].astype(jnp.int32)
        n_blocks = -(-(t * TOP_K) // MOE_BLOCK) + N_EXPERTS
        padded = (counts + MOE_BLOCK - 1) // MOE_BLOCK * MOE_BLOCK
        padded_ends = jnp.cumsum(padded)
        padded_starts = padded_ends - padded
        onehot = idx[..., None] == jnp.arange(N_EXPERTS, dtype=jnp.int32)
        pos = (jnp.sum(jnp.where(onehot, padded_starts, 0), axis=-1) + rank).reshape(-1)
        block_start = jnp.arange(n_blocks, dtype=jnp.int32) * MOE_BLOCK
        block_expert = jnp.minimum(
            jnp.sum((padded_ends[None, :] <= block_start[:, None]).astype(jnp.int32), axis=1), N_EXPERTS - 1)
        n_used = padded_ends[-1:] // MOE_BLOCK
        pad_len = padded - counts
        misc = jnp.concatenate([jnp.sum(pad_len, keepdims=True), n_used,
                                jnp.full((1,), n_blocks, jnp.int32)])

        xs_rt = _dispatch(padded_starts + counts, pad_len, misc, pos, hn_rt, n_blocks * MOE_BLOCK)
        outs_rt = _experts(block_expert, n_used, xs_rt, w_gate_up[layer],
                           b_gate_up[layer].reshape(N_EXPERTS, 1, -1), w_down[layer],
                           b_down[layer].reshape(N_EXPERTS, 1, -1))
        x2 = _combine(pos, outs_rt, h, meta, norm_final_w.reshape(1, d).astype(F32))
    return x2.reshape(b, s, d).astype(x.dtype)
```

```python
import functools

import jax
import jax.numpy as jnp
import numpy as np
from jax import lax
from jax.experimental import pallas as pl
from jax.experimental.pallas import tpu as pltpu

F32 = jnp.float32
BF16 = jnp.bfloat16

D_MODEL = 1024
SSM_WIDTH = 1024
SSM_HEAD_DIM = 64
SSM_HEADS = 16
SSM_GROUPS = 2
SSM_STATE = 128
CONV_WIDTH = 4
CONV_DIM = SSM_WIDTH + 2 * SSM_GROUPS * SSM_STATE
LSTM_WIDTH = 1024
LSTM_HEAD_DIM = 128
LSTM_HEADS = 8
N_EXPERTS = 32
TOP_K = 4
D_FF = 1024
SWIGLU_LIMIT = 7.0
SWIGLU_ALPHA = 1.702
MOE_BLOCK = 256
RMS_EPS = 1e-6

LANES = 128
SUBLANES = 8
MIX_CHUNK = 256
ROW_TILE = 512
COMBINE_TILE = 256
VMEM_LIMIT = 56 * 1024 * 1024

GATE_COLS = 3 * LANES
NEG_INF = float("-inf")


def _cparams(sem):
    return pltpu.CompilerParams(dimension_semantics=sem, vmem_limit_bytes=VMEM_LIMIT)


def _rms(x, w):
    return x * lax.rsqrt(jnp.mean(x * x, axis=-1, keepdims=True) + RMS_EPS) * w


LOG2E = 1.4426950408889634


def _sigmoid(x):
    return 1.0 / (1.0 + jnp.exp2(x * -LOG2E))


def _softplus(x):
    return jnp.maximum(x, 0.0) + jnp.log(1.0 + jnp.exp(-jnp.abs(x)))


def _split3(a):
    hi = a.astype(BF16)
    r = a - hi.astype(F32)
    mid = r.astype(BF16)
    lo = (r - mid.astype(F32)).astype(BF16)
    return hi, mid, lo


def _dot(a, b):
    return jnp.dot(a, b, preferred_element_type=F32)


def _dot_nt(a, b):
    return lax.dot_general(a, b, (((1,), (1,)), ((), ())), preferred_element_type=F32)


def _dot_tn(a, b):
    return lax.dot_general(a, b, (((0,), (0,)), ((), ())), preferred_element_type=F32)


def _sel_dot(sel_bf, a):
    hi, mid, lo = _split3(a)
    return _dot(sel_bf, hi) + _dot(sel_bf, mid) + _dot(sel_bf, lo)


def _expand(a, sel_bf, terms=3):
    out = None
    for piece in _split3(a)[:terms]:
        d = _dot(piece, sel_bf)
        out = d if out is None else out + d
    return out


ROW_SUBTILES = D_MODEL // LANES
DMA_PRIORITIES = 2


def _store_row_tiles(ref, x):
    n = x.shape[0]
    for s in range(ROW_SUBTILES):
        ref[pl.ds(s, n, stride=ROW_SUBTILES), :] = x[:, s * LANES:(s + 1) * LANES]


def _load_row_tile_cols(ref, n):
    return [ref[pl.ds(s, n, stride=ROW_SUBTILES), :] for s in range(ROW_SUBTILES)]


def _row_copy(src, src_row, dst, dst_row, sem):
    return pltpu.make_async_copy(
        src.at[pl.ds(pl.multiple_of(src_row * ROW_SUBTILES, ROW_SUBTILES), ROW_SUBTILES), :],
        dst.at[pl.ds(pl.multiple_of(dst_row * ROW_SUBTILES, ROW_SUBTILES), ROW_SUBTILES), :], sem)


def _wait_row_copies(src, dst, n_rows, sem):
    size = n_rows * ROW_SUBTILES
    pltpu.make_async_copy(src.at[pl.ds(0, size), :], dst.at[pl.ds(0, size), :], sem).wait()


_INPROJ_WIDTHS = (SSM_WIDTH, CONV_DIM, LSTM_WIDTH, LSTM_WIDTH, LSTM_WIDTH, LSTM_WIDTH, GATE_COLS)


def _inproj_kernel(x_ref, nw_ref, w_ref, z_ref, xbc_ref, q_ref, k_ref, v_ref, o_ref, g_ref):
    xb = _rms(x_ref[...], nw_ref[...]).astype(BF16)
    off = 0
    for ref, width in zip((z_ref, xbc_ref, q_ref, k_ref, v_ref, o_ref, g_ref), _INPROJ_WIDTHS):
        ref[...] = _dot(xb, w_ref[:, off:off + width]).astype(ref.dtype)
        off += width


def _inproj(x2, nw, w_all):
    t = x2.shape[0]
    tm = min(ROW_TILE, t)
    ncol = w_all.shape[1]
    out_shape = [jax.ShapeDtypeStruct((t, w), BF16) for w in _INPROJ_WIDTHS[:-1]]
    out_shape.append(jax.ShapeDtypeStruct((t, GATE_COLS), F32))
    return pl.pallas_call(
        _inproj_kernel,
        grid=(t // tm,),
        in_specs=[
            pl.BlockSpec((tm, D_MODEL), lambda i: (i, 0)),
            pl.BlockSpec((1, D_MODEL), lambda i: (0, 0)),
            pl.BlockSpec((D_MODEL, ncol), lambda i: (0, 0), pipeline_mode=pl.Buffered(1)),
        ],
        out_specs=[pl.BlockSpec((tm, w), lambda i: (i, 0)) for w in _INPROJ_WIDTHS],
        out_shape=out_shape,
        compiler_params=_cparams(("arbitrary",)),
        name="inproj",
    )(x2, nw, w_all)


def _ssd_kernel(xbc_ref, z_ref, g_ref, convw_ref, convb_ref, dtb_ref, aneg_ref, dskip_ref, nw_ref,
                sel_ref, y_ref, u_scr, st_scr):
    L = xbc_ref.shape[0]
    gw = SSM_WIDTH // SSM_GROUPS

    @pl.when(pl.program_id(1) == 0)
    def _():
        u_scr[...] = jnp.zeros_like(u_scr)
        st_scr[...] = jnp.zeros_like(st_scr)

    row = lax.broadcasted_iota(jnp.int32, (L, L), 0)
    col = lax.broadcasted_iota(jnp.int32, (L, L), 1)
    causal = row >= col

    xin_bf = xbc_ref[...]
    xin = xin_bf.astype(F32)
    halo = u_scr[...]
    row8 = lax.broadcasted_iota(jnp.int32, (SUBLANES, 1), 0)
    acc = convb_ref[...] + convw_ref[CONV_WIDTH - 1:CONV_WIDTH, :] * xin
    for back in range(1, CONV_WIDTH):
        shifted = _dot((row - col == back).astype(F32).astype(BF16), xin_bf)
        head = shifted[:SUBLANES] + jnp.where(row8 < back, pltpu.roll(halo, back, axis=0), 0.0)
        shifted = jnp.concatenate([head, shifted[SUBLANES:]], axis=0)
        acc = acc + convw_ref[CONV_WIDTH - 1 - back:CONV_WIDTH - back, :] * shifted
    u_scr[...] = xin[L - SUBLANES:, :]
    xbc = acc * _sigmoid(acc)
    xs = xbc[:, :SSM_WIDTH]
    bm = xbc[:, SSM_WIDTH:SSM_WIDTH + SSM_GROUPS * SSM_STATE].astype(BF16)
    cm = xbc[:, SSM_WIDTH + SSM_GROUPS * SSM_STATE:].astype(BF16)

    tril_bf = causal.astype(F32).astype(BF16)

    lane = lax.broadcasted_iota(jnp.int32, (1, LANES), 1)
    dt = jnp.where(lane < SSM_HEADS, _softplus(g_ref[...] + dtb_ref[...]), 0.0)
    acs = _sel_dot(tril_bf, dt * aneg_ref[...]) * LOG2E
    acs_t = acs.T
    sel = sel_ref[...]
    acs_last = acs[L - 1:L, :]
    ea_x = _expand(jnp.exp2(acs), sel, terms=2)
    xd = xs * _expand(dt, sel, terms=1)
    xd_bf = xd.astype(BF16)
    xw = (xd * _expand(jnp.exp2(acs_last - acs), sel, terms=1)).astype(BF16)

    cbs = [_dot_nt(cm[:, g * SSM_STATE:(g + 1) * SSM_STATE], bm[:, g * SSM_STATE:(g + 1) * SSM_STATE])
           for g in range(SSM_GROUPS)]
    yoff = [_dot(cm[:, g * SSM_STATE:(g + 1) * SSM_STATE], st_scr[g].astype(BF16)) for g in range(SSM_GROUPS)]
    upd = [_dot_tn(bm[:, g * SSM_STATE:(g + 1) * SSM_STATE], xw[:, g * gw:(g + 1) * gw])
           for g in range(SSM_GROUPS)]
    for g in range(SSM_GROUPS):
        st_scr[g] = st_scr[g] * ea_x[L - 1:L, g * gw:(g + 1) * gw] + upd[g]

    pair_lane = lax.broadcasted_iota(jnp.int32, (1, LANES), 1)
    heads_per_group = SSM_HEADS // SSM_GROUPS
    ms = [(cbs[h // heads_per_group]
           * jnp.exp2(jnp.where(causal, acs[:, h:h + 1] - acs_t[h:h + 1, :], NEG_INF))).astype(BF16)
          for h in range(SSM_HEADS)]
    rhs = []
    for j in range(SSM_HEADS // 2):
        xpair = xd_bf[:, j * LANES:(j + 1) * LANES]
        zero = jnp.zeros_like(xpair)
        rhs.append(jnp.concatenate([jnp.where(pair_lane < SSM_HEAD_DIM, xpair, zero),
                                    jnp.where(pair_lane >= SSM_HEAD_DIM, xpair, zero)], axis=0))
    ydiag = [_dot(jnp.concatenate([ms[2 * j], ms[2 * j + 1]], axis=1), rhs[j]) for j in range(SSM_HEADS // 2)]
    y = jnp.concatenate(ydiag, axis=1) + jnp.concatenate(yoff, axis=1) * ea_x + dskip_ref[...] * xs

    zz = z_ref[...].astype(F32)
    y = y * (zz * _sigmoid(zz))
    y_ref[...] = _rms(y, nw_ref[...]).astype(BF16)


def _ssd(xbc, z, gates, convw, convb, dtb, aneg, dskip_x, nw, sel, b, s):
    L = min(MIX_CHUNK, s)
    nc = s // L
    tok = lambda bi, ci: (bi * nc + ci, 0)
    const = lambda bi, ci: (0, 0)
    return pl.pallas_call(
        _ssd_kernel,
        grid=(b, nc),
        in_specs=[
            pl.BlockSpec((L, CONV_DIM), tok),
            pl.BlockSpec((L, SSM_WIDTH), tok),
            pl.BlockSpec((L, LANES), tok),
            pl.BlockSpec((CONV_WIDTH, CONV_DIM), const),
            pl.BlockSpec((1, CONV_DIM), const),
            pl.BlockSpec((1, LANES), const),
            pl.BlockSpec((1, LANES), const),
            pl.BlockSpec((1, SSM_WIDTH), const),
            pl.BlockSpec((1, SSM_WIDTH), const),
            pl.BlockSpec((LANES, SSM_WIDTH), const),
        ],
        out_specs=pl.BlockSpec((L, SSM_WIDTH), tok),
        out_shape=jax.ShapeDtypeStruct((b * s, SSM_WIDTH), BF16),
        scratch_shapes=[pltpu.VMEM((SUBLANES, CONV_DIM), F32),
                        pltpu.VMEM((SSM_GROUPS, SSM_STATE, SSM_WIDTH // SSM_GROUPS), F32)],
        compiler_params=_cparams(("arbitrary", "arbitrary")),
        name="ssd",
    )(xbc, z, gates, convw, convb, dtb, aneg, dskip_x, nw, sel)


def _mlstm_kernel(q_ref, k_ref, v_ref, o_ref, gi_ref, gf_ref, ib_ref, fb_ref, nw_ref, y_ref,
                  st_scr, m_scr):
    L = q_ref.shape[0]
    dh = LSTM_HEAD_DIM
    scale = dh ** -0.5

    @pl.when(pl.program_id(1) == 0)
    def _():
        st_scr[...] = jnp.zeros_like(st_scr)
        m_scr[...] = jnp.zeros_like(m_scr)

    row = lax.broadcasted_iota(jnp.int32, (L, L), 0)
    col = lax.broadcasted_iota(jnp.int32, (L, L), 1)
    causal = row >= col
    tril_bf = causal.astype(F32).astype(BF16)
    lane = lax.broadcasted_iota(jnp.int32, (1, LANES), 1)
    live = lane < LSTM_HEADS

    ii = jnp.where(live, gi_ref[...] + ib_ref[...], 0.0)
    logf = jnp.where(live, -_softplus(-(gf_ref[...] + fb_ref[...])), 0.0)
    cumf = _sel_dot(tril_bf, logf)
    g = ii - cumf
    rid = lax.broadcasted_iota(jnp.int32, (L, LANES), 0)
    cmx = g
    step = 1
    while step < L:
        cmx = jnp.maximum(cmx, jnp.where(rid >= step, pltpu.roll(cmx, step, axis=0), NEG_INF))
        step *= 2
    m_prev = m_scr[...]
    mx = jnp.maximum(m_prev, cmx)
    w_inter = jnp.exp(m_prev - mx)
    enm = jnp.exp(-(cumf + mx))
    g2_t = (g * LOG2E).T
    mx2 = mx * LOG2E - np.log2(scale)
    m_last = mx[L - 1:L, :]
    wk = jnp.exp(g - m_last) * scale
    sc = jnp.exp(m_prev - m_last)
    m_scr[...] = cumf[L - 1:L, :] + m_last

    ones_bf = jnp.ones((L, dh), BF16)
    heads = range(LSTM_HEADS)
    hsl = [slice(h * dh, (h + 1) * dh) for h in heads]
    vaug = [jnp.concatenate([v_ref[:, hsl[h]], ones_bf], axis=1) for h in heads]
    r2 = [_dot(q_ref[:, hsl[h]], st_scr[h].astype(BF16)) for h in heads]
    upd = [_dot_tn((k_ref[:, hsl[h]].astype(F32) * wk[:, h:h + 1]).astype(BF16), vaug[h]) for h in heads]
    for h in heads:
        st_scr[h] = st_scr[h] * sc[:, h:h + 1] + upd[h]
    s_qk = [_dot_nt(q_ref[:, hsl[h]], k_ref[:, hsl[h]]) for h in heads]
    p = [(s_qk[h] * jnp.exp2(jnp.where(causal, g2_t[h:h + 1, :] - mx2[:, h:h + 1], NEG_INF))).astype(BF16)
         for h in heads]
    r1 = [_dot(p[h], vaug[h]) for h in heads]
    hh = []
    for h in heads:
        wcol = w_inter[:, h:h + 1]
        num = r1[h][:, :dh] + r2[h][:, :dh] * wcol
        den = r1[h][:, dh:] + r2[h][:, dh:] * wcol
        hh.append(num / jnp.maximum(jnp.abs(den), enm[:, h:h + 1]))
    inv = [lax.rsqrt(jnp.mean(hh[h] * hh[h], axis=-1, keepdims=True) + RMS_EPS) for h in heads]
    for h in heads:
        oo = o_ref[:, hsl[h]].astype(F32)
        y_ref[:, hsl[h]] = (_sigmoid(oo) * (hh[h] * inv[h] * nw_ref[:, hsl[h]])).astype(BF16)


def _mlstm(q, k, v, o, gates, ib, fb, nw, b, s):
    L = min(MIX_CHUNK, s)
    nc = s // L
    tok = lambda bi, ci: (bi * nc + ci, 0)
    const = lambda bi, ci: (0, 0)
    big = pl.BlockSpec((L, LSTM_WIDTH), tok)
    return pl.pallas_call(
        _mlstm_kernel,
        grid=(b, nc),
        in_specs=[big, big, big, big,
                  pl.BlockSpec((L, LANES), lambda bi, ci: (bi * nc + ci, 1)),
                  pl.BlockSpec((L, LANES), lambda bi, ci: (bi * nc + ci, 2)),
                  pl.BlockSpec((1, LANES), const),
                  pl.BlockSpec((1, LANES), const),
                  pl.BlockSpec((1, LSTM_WIDTH), const)],
        out_specs=big,
        out_shape=jax.ShapeDtypeStruct((b * s, LSTM_WIDTH), BF16),
        scratch_shapes=[pltpu.VMEM((LSTM_HEADS, LSTM_HEAD_DIM, 2 * LSTM_HEAD_DIM), F32),
                        pltpu.VMEM((1, LANES), F32)],
        compiler_params=_cparams(("arbitrary", "arbitrary")),
        name="mlstm",
    )(q, k, v, o, gates, gates, ib, fb, nw)


META_IDX, META_GATE, META_RANK = 0, TOP_K, 2 * TOP_K


def _outproj_kernel(ys_ref, yl_ref, x_ref, wo_ref, nw_ref, wr_ref, br_ref, h_ref, hn_ref, meta_ref,
                    cnt_ref, cnt_scr, logit_scr):
    tm = x_ref.shape[0]
    step = pl.program_id(0)

    @pl.when(step == 0)
    def _():
        cnt_scr[...] = jnp.zeros_like(cnt_scr)
        logit_scr[...] = jnp.zeros_like(logit_scr)

    vals = logit_scr[...]
    routed = (step > 0).astype(F32)

    h = (x_ref[...] + _dot(ys_ref[...], wo_ref[:SSM_WIDTH, :]) + _dot(yl_ref[...], wo_ref[SSM_WIDTH:, :]))
    h_ref[...] = h
    hn = _rms(h, nw_ref[...])
    _store_row_tiles(hn_ref, hn)
    logit_scr[...] = _dot(hn.astype(BF16), wr_ref[...]) + br_ref[...]

    lane = lax.broadcasted_iota(jnp.int32, (tm, LANES), 1)
    member = jnp.zeros((tm, LANES), F32)
    tops, idxs, sels = [], [], []
    for _ in range(TOP_K):
        m = jnp.max(vals, axis=-1, keepdims=True)
        idx = jnp.min(jnp.where(vals == m, lane, LANES), axis=-1, keepdims=True)
        sel = lane == idx
        vals = jnp.where(sel, NEG_INF, vals)
        member = member + sel.astype(F32)
        tops.append(m)
        idxs.append(idx)
        sels.append(sel)
    es = [jnp.exp(t - tops[0]) for t in tops]
    inv = 1.0 / (es[0] + es[1] + es[2] + es[3])
    member = member * routed

    r = lax.broadcasted_iota(jnp.int32, (tm, tm), 0)
    c = lax.broadcasted_iota(jnp.int32, (tm, tm), 1)
    strict = (r > c).astype(F32).astype(BF16)
    carry = cnt_scr[0:1, :]
    rank_all = _dot(strict, member.astype(BF16)) + carry
    total = carry + jnp.sum(member, axis=0, keepdims=True)
    cnt_scr[...] = jnp.broadcast_to(total, cnt_scr.shape)
    cnt_ref[...] = jnp.broadcast_to(total, cnt_ref.shape)

    meta = jnp.zeros((tm, LANES), F32)
    for kk in range(TOP_K):
        rank = jnp.sum(jnp.where(sels[kk], rank_all, 0.0), axis=-1, keepdims=True)
        meta = jnp.where(lane == META_IDX + kk, idxs[kk].astype(F32), meta)
        meta = jnp.where(lane == META_GATE + kk, es[kk] * inv, meta)
        meta = jnp.where(lane == META_RANK + kk, rank, meta)
    meta_ref[...] = meta


def _outproj(ys, yl, x2, wo, nw, wr, br):
    t = x2.shape[0]
    tm = min(ROW_TILE, t)
    n = t // tm
    projected = lambda i: (jnp.minimum(i, n - 1), 0)
    routed = lambda i: (jnp.maximum(i - 1, 0), 0)
    tokspec = lambda w: pl.BlockSpec((tm, w), projected)
    const = lambda i: (0, 0)
    return pl.pallas_call(
        _outproj_kernel,
        grid=(n + 1,),
        in_specs=[tokspec(SSM_WIDTH), tokspec(LSTM_WIDTH), tokspec(D_MODEL),
                  pl.BlockSpec((SSM_WIDTH + LSTM_WIDTH, D_MODEL), const, pipeline_mode=pl.Buffered(1)),
                  pl.BlockSpec((1, D_MODEL), const),
                  pl.BlockSpec((D_MODEL, LANES), const),
                  pl.BlockSpec((1, LANES), const)],
        out_specs=[tokspec(D_MODEL), pl.BlockSpec((tm * ROW_SUBTILES, LANES), projected),
                   pl.BlockSpec((tm, LANES), routed), pl.BlockSpec((SUBLANES, LANES), const)],
        out_shape=[jax.ShapeDtypeStruct((t, D_MODEL), F32), jax.ShapeDtypeStruct((t * ROW_SUBTILES, LANES), F32),
                   jax.ShapeDtypeStruct((t, LANES), F32), jax.ShapeDtypeStruct((SUBLANES, LANES), F32)],
        scratch_shapes=[pltpu.VMEM((SUBLANES, LANES), F32), pltpu.VMEM((tm, LANES), F32)],
        compiler_params=_cparams(("arbitrary",)),
        name="outproj_router",
    )(ys, yl, x2, wo, nw, wr, br)


DISPATCH_TILE = 512
DISPATCH_GROUP = 4
PAD_CHUNKS = (128, 64, 32, 16, 8, 4, 2, 1)


def _dispatch_kernel(padstart_ref, padlen_ref, misc_ref, pos_ref, hn_ref, xs_hbm, sem, zsem, zeros_scr):
    i = pl.program_id(0)
    td = pos_ref.shape[-1] // TOP_K

    @pl.when(i == 0)
    def _():
        zeros_scr[...] = jnp.zeros_like(zeros_scr)
        for e in range(N_EXPERTS):
            row = padstart_ref[e]
            nrow = padlen_ref[e]
            for chunk in PAD_CHUNKS:
                @pl.when((nrow & chunk) != 0)
                def _(row=row, chunk=chunk):
                    pltpu.make_async_copy(
                        zeros_scr.at[pl.ds(0, chunk * ROW_SUBTILES), :],
                        xs_hbm.at[pl.ds(pl.multiple_of(row * ROW_SUBTILES, ROW_SUBTILES), chunk * ROW_SUBTILES), :],
                        zsem).start()
                row = row + (nrow & chunk)

        def tail_copy(blk):
            return pltpu.make_async_copy(
                zeros_scr,
                xs_hbm.at[pl.ds(pl.multiple_of(blk * (MOE_BLOCK * ROW_SUBTILES), ROW_SUBTILES),
                                MOE_BLOCK * ROW_SUBTILES), :], zsem)

        n_used, n_blocks = misc_ref[1], misc_ref[2]

        def start_tail(blk, carry):
            tail_copy(blk).start()
            return carry
        lax.fori_loop(n_used, n_blocks, start_tail, 0)

        def drain_row(j, carry):
            _wait_row_copies(zeros_scr, xs_hbm, 1, zsem)
            return carry
        lax.fori_loop(0, misc_ref[0], drain_row, 0)

        def drain_tail(blk, carry):
            tail_copy(blk).wait()
            return carry
        lax.fori_loop(n_used, n_blocks, drain_tail, 0)

    def body(g, carry):
        r0 = g * DISPATCH_GROUP
        slots = [pos_ref[0, 0, r0 * TOP_K + j] for j in range(DISPATCH_GROUP * TOP_K)]
        for j, slot in enumerate(slots):
            _row_copy(hn_ref, r0 + j // TOP_K, xs_hbm, slot, sem).start(priority=j % DMA_PRIORITIES)
        return carry
    lax.fori_loop(0, td // DISPATCH_GROUP, body, 0)

    for _ in range(TOP_K):
        _wait_row_copies(hn_ref, xs_hbm, td, sem)


def _dispatch(pad_start, pad_len, misc, pos_flat, hn_rt, n_rows):
    t = pos_flat.shape[0] // TOP_K
    td = min(DISPATCH_TILE, t)
    n = t // td
    grid_spec = pltpu.PrefetchScalarGridSpec(
        num_scalar_prefetch=3,
        grid=(n,),
        in_specs=[pl.BlockSpec((1, 1, TOP_K * td), lambda i, *_: (i, 0, 0), memory_space=pltpu.SMEM),
                  pl.BlockSpec((td * ROW_SUBTILES, LANES), lambda i, *_: (i, 0))],
        out_specs=pl.BlockSpec(memory_space=pl.ANY),
        scratch_shapes=[pltpu.SemaphoreType.DMA(()), pltpu.SemaphoreType.DMA(()),
                        pltpu.VMEM((MOE_BLOCK * ROW_SUBTILES, LANES), F32)],
    )
    return pl.pallas_call(
        _dispatch_kernel,
        grid_spec=grid_spec,
        out_shape=jax.ShapeDtypeStruct((n_rows * ROW_SUBTILES, LANES), F32),
        compiler_params=_cparams(("arbitrary",)),
        name="dispatch",
    )(pad_start, pad_len, misc, pos_flat.reshape(n, 1, TOP_K * td), hn_rt)


def _experts_kernel(be_ref, nused_ref, next_ref, x_ref, wgu_hbm, bgu_ref, wdn_hbm, bdn_ref, out_ref,
                    wgu_bf, wdn_bf, wgu_st, wdn_st, wsem, slot_ref):
    i = pl.program_id(0)
    n_used = nused_ref[0]

    def weight_copies(e, slot):
        return (pltpu.make_async_copy(wgu_hbm.at[e], wgu_st.at[slot], wsem.at[0, slot]),
                pltpu.make_async_copy(wdn_hbm.at[e], wdn_st.at[slot], wsem.at[1, slot]))

    @pl.when(i == 0)
    def _():
        slot_ref[0] = 0
        for cp in weight_copies(be_ref[0], 0):
            cp.start()

    @pl.when(i < n_used)
    def _():
        @pl.when(jnp.logical_or(i == 0, be_ref[i] != be_ref[jnp.maximum(i - 1, 0)]))
        def _():
            e = be_ref[i]
            slot = slot_ref[0]
            for cp in weight_copies(e, slot):
                cp.wait()
            wgu_bf[...] = wgu_st[slot].astype(BF16)
            wdn_bf[...] = wdn_st[slot].astype(BF16)
            nxt = next_ref[e]

            @pl.when(nxt >= 0)
            def _():
                for cp in weight_copies(nxt, 1 - slot):
                    cp.start()
            slot_ref[0] = 1 - slot

        xb = jnp.concatenate(_load_row_tile_cols(x_ref, MOE_BLOCK), axis=1).astype(BF16)
        hh = _dot(xb, wgu_bf[...]) + bgu_ref[...]
        gh = jnp.minimum(hh[:, :D_FF], SWIGLU_LIMIT)
        uh = jnp.clip(hh[:, D_FF:], -SWIGLU_LIMIT, SWIGLU_LIMIT)
        act = (uh + 1.0) * (gh * _sigmoid(SWIGLU_ALPHA * gh))
        _store_row_tiles(out_ref, _dot(act.astype(BF16), wdn_bf[...]) + bdn_ref[...])

    @pl.when(i >= n_used)
    def _():
        out_ref[...] = jnp.zeros_like(out_ref)


def _experts(block_expert, n_used, next_expert, xs_rt, wgu, bgu3, wdn, bdn3):
    n_blocks = block_expert.shape[0]
    blk = MOE_BLOCK * ROW_SUBTILES
    grid_spec = pltpu.PrefetchScalarGridSpec(
        num_scalar_prefetch=3,
        grid=(n_blocks,),
        in_specs=[
            pl.BlockSpec((blk, LANES), lambda i, be, nu, nx: (i, 0)),
            pl.BlockSpec(memory_space=pl.ANY),
            pl.BlockSpec((None, 1, 2 * D_FF), lambda i, be, nu, nx: (be[i], 0, 0)),
            pl.BlockSpec(memory_space=pl.ANY),
            pl.BlockSpec((None, 1, D_MODEL), lambda i, be, nu, nx: (be[i], 0, 0)),
        ],
        out_specs=pl.BlockSpec((blk, LANES), lambda i, be, nu, nx: (i, 0)),
        scratch_shapes=[pltpu.VMEM((D_MODEL, 2 * D_FF), BF16),
                        pltpu.VMEM((D_FF, D_MODEL), BF16),
                        pltpu.VMEM((2, D_MODEL, 2 * D_FF), F32),
                        pltpu.VMEM((2, D_FF, D_MODEL), F32),
                        pltpu.SemaphoreType.DMA((2, 2)),
                        pltpu.SMEM((1,), jnp.int32)],
    )
    return pl.pallas_call(
        _experts_kernel,
        grid_spec=grid_spec,
        out_shape=jax.ShapeDtypeStruct((n_blocks * blk, LANES), F32),
        compiler_params=_cparams(("arbitrary",)),
        name="experts",
    )(block_expert, n_used, next_expert, xs_rt, wgu, bgu3, wdn, bdn3)


def _combine_kernel(pos_cur, pos_next, outs_hbm, h_ref, meta_ref, nw_ref, y_ref, buf, sem):
    i = pl.program_id(0)
    n = pl.num_programs(0)
    tc = h_ref.shape[0]
    slot = i % 2

    def issue(pos_ref, s):
        def body(r, carry):
            for kk in range(TOP_K):
                _row_copy(outs_hbm, pos_ref[0, 0, r * TOP_K + kk], buf.at[s, kk], r,
                          sem.at[s]).start(priority=kk % DMA_PRIORITIES)
            return carry
        lax.fori_loop(0, tc, body, 0, unroll=4)

    @pl.when(i == 0)
    def _():
        issue(pos_cur, 0)

    @pl.when(i + 1 < n)
    def _():
        issue(pos_next, 1 - slot)

    for kk in range(TOP_K):
        _wait_row_copies(outs_hbm, buf.at[slot, kk], tc, sem.at[slot])
    meta = meta_ref[...]
    gates = [meta[:, META_GATE + kk:META_GATE + kk + 1] for kk in range(TOP_K)]
    rows = [_load_row_tile_cols(buf.at[slot, kk], tc) for kk in range(TOP_K)]
    cols = []
    for s in range(ROW_SUBTILES):
        acc = h_ref[:, s * LANES:(s + 1) * LANES]
        for kk in range(TOP_K):
            acc = acc + gates[kk] * rows[kk][s]
        cols.append(acc)
    y_ref[...] = _rms(jnp.concatenate(cols, axis=1), nw_ref[...])


def _combine(pos_flat, outs_rt, h, meta, nw):
    t = h.shape[0]
    tc = min(COMBINE_TILE, t)
    n = t // tc
    pos3 = pos_flat.reshape(n, 1, TOP_K * tc)
    return pl.pallas_call(
        _combine_kernel,
        grid=(n,),
        in_specs=[
            pl.BlockSpec((1, 1, TOP_K * tc), lambda i: (i, 0, 0), memory_space=pltpu.SMEM),
            pl.BlockSpec((1, 1, TOP_K * tc), lambda i: (jnp.minimum(i + 1, n - 1), 0, 0),
                         memory_space=pltpu.SMEM),
            pl.BlockSpec(memory_space=pl.ANY),
            pl.BlockSpec((tc, D_MODEL), lambda i: (i, 0)),
            pl.BlockSpec((tc, LANES), lambda i: (i, 0)),
            pl.BlockSpec((1, D_MODEL), lambda i: (0, 0)),
        ],
        out_specs=pl.BlockSpec((tc, D_MODEL), lambda i: (i, 0)),
        out_shape=jax.ShapeDtypeStruct((t, D_MODEL), F32),
        scratch_shapes=[pltpu.VMEM((2, TOP_K, tc * ROW_SUBTILES, LANES), F32), pltpu.SemaphoreType.DMA((2,))],
        compiler_params=_cparams(("arbitrary",)),
        name="combine",
    )(pos3, pos3, outs_rt, h, meta, nw)


def _pad_lanes(v, fill=0.0):
    v = v.astype(F32).reshape(1, -1)
    return jnp.pad(v, ((0, 0), (0, LANES - v.shape[1])), constant_values=fill)


def kernel(x, norm_mix_w, w_in, conv_w, conv_b, dt_bias, a_log, d_skip, ssm_norm_w, lstm_i_bias,
           lstm_f_bias, lstm_norm_w, w_out, norm_ffn_w, w_router, b_router, w_gate_up, b_gate_up,
           w_down, b_down, norm_final_w):
    b, s, d = x.shape
    t = b * s
    x2 = x.reshape(t, d).astype(F32)
    depth = w_in.shape[0]
    assert depth == 1, "the combine kernel fuses the final norm, so exactly one layer is supported"
    for layer in range(depth):
        wi = w_in[layer]
        c0 = SSM_WIDTH
        c1 = c0 + CONV_DIM
        c2 = c1 + SSM_HEADS
        c3 = c2 + 4 * LSTM_WIDTH
        c4 = c3 + LSTM_HEADS
        padc = lambda w: jnp.pad(w, ((0, 0), (0, LANES - w.shape[1])))
        w_all = jnp.concatenate([wi[:, :c1], wi[:, c2:c3], padc(wi[:, c1:c2]), padc(wi[:, c3:c4]),
                                 padc(wi[:, c4:])], axis=1).astype(BF16)
        sel = (jnp.arange(LANES)[:, None] == (jnp.arange(SSM_WIDTH) // SSM_HEAD_DIM)[None, :]).astype(BF16)
        a_neg = _pad_lanes(-jnp.exp(a_log[layer].astype(F32)))
        dskip_x = jnp.repeat(d_skip[layer].astype(F32), SSM_HEAD_DIM).reshape(1, SSM_WIDTH)
        wr = jnp.pad(w_router[layer], ((0, 0), (0, LANES - N_EXPERTS))).astype(BF16)
        br = _pad_lanes(b_router[layer], fill=NEG_INF)

        z, xbc, q, k, v, o, gates = _inproj(x2, norm_mix_w[layer].reshape(1, d).astype(F32), w_all)
        y_ssd = _ssd(xbc, z, gates, conv_w[layer].astype(F32), conv_b[layer].reshape(1, -1).astype(F32),
                     _pad_lanes(dt_bias[layer]), a_neg, dskip_x,
                     ssm_norm_w[layer].reshape(1, -1).astype(F32), sel, b, s)
        y_lstm = _mlstm(q, k, v, o, gates, _pad_lanes(lstm_i_bias[layer]), _pad_lanes(lstm_f_bias[layer]),
                        lstm_norm_w[layer].reshape(1, -1).astype(F32), b, s)
        h, hn_rt, meta, cnt = _outproj(y_ssd, y_lstm, x2, w_out[layer].astype(BF16),
                                       norm_ffn_w[layer].reshape(1, d).astype(F32), wr, br)

        idx = meta[:, META_IDX:META_IDX + TOP_K].astype(jnp.int32)
        rank = meta[:, META_RANK:META_RANK + TOP_K].astype(jnp.int32)
        counts = cnt[0, :N_EXPERTS].astype(jnp.int32)
        n_blocks = -(-(t * TOP_K) // MOE_BLOCK) + N_EXPERTS
        padded = (counts + MOE_BLOCK - 1) // MOE_BLOCK * MOE_BLOCK
        padded_ends = jnp.cumsum(padded)
        padded_starts = padded_ends - padded
        onehot = idx[..., None] == jnp.arange(N_EXPERTS, dtype=jnp.int32)
        pos = (jnp.sum(jnp.where(onehot, padded_starts, 0), axis=-1) + rank).reshape(-1)
        block_start = jnp.arange(n_blocks, dtype=jnp.int32) * MOE_BLOCK
        block_expert = jnp.minimum(
            jnp.sum((padded_ends[None, :] <= block_start[:, None]).astype(jnp.int32), axis=1), N_EXPERTS - 1)
        n_used = padded_ends[-1:] // MOE_BLOCK
        pad_len = padded - counts
        misc = jnp.concatenate([jnp.sum(pad_len, keepdims=True), n_used,
                                jnp.full((1,), n_blocks, jnp.int32)])

        xs_rt = _dispatch(padded_starts + counts, pad_len, misc, pos, hn_rt, n_blocks * MOE_BLOCK)
        eids = jnp.arange(N_EXPERTS, dtype=jnp.int32)
        later_nonempty = (eids[None, :] > eids[:, None]) & (counts[None, :] > 0)
        next_expert = jnp.min(jnp.where(later_nonempty, eids[None, :], N_EXPERTS), axis=1)
        next_expert = jnp.where(next_expert < N_EXPERTS, next_expert, -1)
        outs_rt = _experts(block_expert, n_used, next_expert, xs_rt, w_gate_up[layer],
                           b_gate_up[layer].reshape(N_EXPERTS, 1, -1), w_down[layer],
                           b_down[layer].reshape(N_EXPERTS, 1, -1))
        x2 = _combine(pos, outs_rt, h, meta, norm_final_w.reshape(1, d).astype(F32))
    return x2.reshape(b, s, d).astype(x.dtype)
```

```python
import functools

import jax
import jax.numpy as jnp
import numpy as np
from jax import lax
from jax.experimental import pallas as pl
from jax.experimental.pallas import tpu as pltpu

F32 = jnp.float32
BF16 = jnp.bfloat16

D_MODEL = 1024
SSM_WIDTH = 1024
SSM_HEAD_DIM = 64
SSM_HEADS = 16
SSM_GROUPS = 2
SSM_STATE = 128
CONV_WIDTH = 4
CONV_DIM = SSM_WIDTH + 2 * SSM_GROUPS * SSM_STATE
LSTM_WIDTH = 1024
LSTM_HEAD_DIM = 128
LSTM_HEADS = 8
N_EXPERTS = 32
TOP_K = 4
D_FF = 1024
SWIGLU_LIMIT = 7.0
SWIGLU_ALPHA = 1.702
MOE_BLOCK = 256
RMS_EPS = 1e-6

LANES = 128
SUBLANES = 8
MIX_CHUNK = 256
ROW_TILE = 512
COMBINE_TILE = 512
VMEM_LIMIT = 56 * 1024 * 1024

GATE_COLS = 3 * LANES
NEG_INF = float("-inf")


def _cparams(sem):
    return pltpu.CompilerParams(dimension_semantics=sem, vmem_limit_bytes=VMEM_LIMIT)


def _rms(x, w):
    return x * lax.rsqrt(jnp.mean(x * x, axis=-1, keepdims=True) + RMS_EPS) * w


LOG2E = 1.4426950408889634


def _sigmoid(x):
    return 1.0 / (1.0 + jnp.exp2(x * -LOG2E))


def _softplus(x):
    return jnp.maximum(x, 0.0) + jnp.log(1.0 + jnp.exp(-jnp.abs(x)))


def _split3(a):
    hi = a.astype(BF16)
    r = a - hi.astype(F32)
    mid = r.astype(BF16)
    lo = (r - mid.astype(F32)).astype(BF16)
    return hi, mid, lo


def _dot(a, b):
    return jnp.dot(a, b, preferred_element_type=F32)


def _dot_nt(a, b):
    return lax.dot_general(a, b, (((1,), (1,)), ((), ())), preferred_element_type=F32)


def _dot_tn(a, b):
    return lax.dot_general(a, b, (((0,), (0,)), ((), ())), preferred_element_type=F32)


def _sel_dot(sel_bf, a):
    hi, mid, lo = _split3(a)
    return _dot(sel_bf, hi) + _dot(sel_bf, mid) + _dot(sel_bf, lo)


def _expand(a, sel_bf, terms=3):
    out = None
    for piece in _split3(a)[:terms]:
        d = _dot(piece, sel_bf)
        out = d if out is None else out + d
    return out


ROW_SUBTILES = D_MODEL // LANES
DMA_PRIORITIES = 2


def _store_row_tiles(ref, x):
    n = x.shape[0]
    for s in range(ROW_SUBTILES):
        ref[pl.ds(s, n, stride=ROW_SUBTILES), :] = x[:, s * LANES:(s + 1) * LANES]


def _load_row_tile_cols(ref, n):
    return [ref[pl.ds(s, n, stride=ROW_SUBTILES), :] for s in range(ROW_SUBTILES)]


def _row_copy(src, src_row, dst, dst_row, sem):
    return pltpu.make_async_copy(
        src.at[pl.ds(pl.multiple_of(src_row * ROW_SUBTILES, ROW_SUBTILES), ROW_SUBTILES), :],
        dst.at[pl.ds(pl.multiple_of(dst_row * ROW_SUBTILES, ROW_SUBTILES), ROW_SUBTILES), :], sem)


def _wait_row_copies(src, dst, n_rows, sem):
    size = n_rows * ROW_SUBTILES
    pltpu.make_async_copy(src.at[pl.ds(0, size), :], dst.at[pl.ds(0, size), :], sem).wait()


_INPROJ_WIDTHS = (SSM_WIDTH, CONV_DIM, LSTM_WIDTH, LSTM_WIDTH, LSTM_WIDTH, LSTM_WIDTH, GATE_COLS)


def _inproj_kernel(x_ref, nw_ref, w_ref, z_ref, xbc_ref, q_ref, k_ref, v_ref, o_ref, g_ref):
    xb = _rms(x_ref[...], nw_ref[...]).astype(BF16)
    off = 0
    for ref, width in zip((z_ref, xbc_ref, q_ref, k_ref, v_ref, o_ref, g_ref), _INPROJ_WIDTHS):
        ref[...] = _dot(xb, w_ref[:, off:off + width]).astype(ref.dtype)
        off += width


def _inproj(x2, nw, w_all):
    t = x2.shape[0]
    tm = min(ROW_TILE, t)
    ncol = w_all.shape[1]
    out_shape = [jax.ShapeDtypeStruct((t, w), BF16) for w in _INPROJ_WIDTHS[:-1]]
    out_shape.append(jax.ShapeDtypeStruct((t, GATE_COLS), F32))
    return pl.pallas_call(
        _inproj_kernel,
        grid=(t // tm,),
        in_specs=[
            pl.BlockSpec((tm, D_MODEL), lambda i: (i, 0)),
            pl.BlockSpec((1, D_MODEL), lambda i: (0, 0)),
            pl.BlockSpec((D_MODEL, ncol), lambda i: (0, 0), pipeline_mode=pl.Buffered(1)),
        ],
        out_specs=[pl.BlockSpec((tm, w), lambda i: (i, 0)) for w in _INPROJ_WIDTHS],
        out_shape=out_shape,
        compiler_params=_cparams(("arbitrary",)),
        name="inproj",
    )(x2, nw, w_all)


def _ssd_phases(xbc_ref, z_ref, g_ref, convw_ref, convb_ref, dtb_ref, aneg_ref, dskip_ref, nw_ref,
                sel_ref, y_ref, u_scr, st_scr):
    L = xbc_ref.shape[0]
    gw = SSM_WIDTH // SSM_GROUPS

    row = lax.broadcasted_iota(jnp.int32, (L, L), 0)
    col = lax.broadcasted_iota(jnp.int32, (L, L), 1)
    causal = row >= col

    xin_bf = xbc_ref[...]
    xin = xin_bf.astype(F32)
    halo = u_scr[...]
    row8 = lax.broadcasted_iota(jnp.int32, (SUBLANES, 1), 0)
    acc = convb_ref[...] + convw_ref[CONV_WIDTH - 1:CONV_WIDTH, :] * xin
    for back in range(1, CONV_WIDTH):
        shifted = _dot((row - col == back).astype(F32).astype(BF16), xin_bf)
        head = shifted[:SUBLANES] + jnp.where(row8 < back, pltpu.roll(halo, back, axis=0), 0.0)
        shifted = jnp.concatenate([head, shifted[SUBLANES:]], axis=0)
        acc = acc + convw_ref[CONV_WIDTH - 1 - back:CONV_WIDTH - back, :] * shifted
    u_scr[...] = xin[L - SUBLANES:, :]
    xbc = acc * _sigmoid(acc)
    xs = xbc[:, :SSM_WIDTH]
    bm = xbc[:, SSM_WIDTH:SSM_WIDTH + SSM_GROUPS * SSM_STATE].astype(BF16)
    cm = xbc[:, SSM_WIDTH + SSM_GROUPS * SSM_STATE:].astype(BF16)
    yield

    tril_bf = causal.astype(F32).astype(BF16)

    lane = lax.broadcasted_iota(jnp.int32, (1, LANES), 1)
    dt = jnp.where(lane < SSM_HEADS, _softplus(g_ref[...] + dtb_ref[...]), 0.0)
    acs = _sel_dot(tril_bf, dt * aneg_ref[...]) * LOG2E
    acs_t = acs.T
    sel = sel_ref[...]
    acs_last = acs[L - 1:L, :]
    ea_x = _expand(jnp.exp2(acs), sel, terms=2)
    xd = xs * _expand(dt, sel, terms=1)
    xd_bf = xd.astype(BF16)
    xw = (xd * _expand(jnp.exp2(acs_last - acs), sel, terms=1)).astype(BF16)
    yield

    cbs = [_dot_nt(cm[:, g * SSM_STATE:(g + 1) * SSM_STATE], bm[:, g * SSM_STATE:(g + 1) * SSM_STATE])
           for g in range(SSM_GROUPS)]
    yoff = [_dot(cm[:, g * SSM_STATE:(g + 1) * SSM_STATE], st_scr[g].astype(BF16)) for g in range(SSM_GROUPS)]
    upd = [_dot_tn(bm[:, g * SSM_STATE:(g + 1) * SSM_STATE], xw[:, g * gw:(g + 1) * gw])
           for g in range(SSM_GROUPS)]
    for g in range(SSM_GROUPS):
        st_scr[g] = st_scr[g] * ea_x[L - 1:L, g * gw:(g + 1) * gw] + upd[g]
    yield

    pair_lane = lax.broadcasted_iota(jnp.int32, (1, LANES), 1)
    heads_per_group = SSM_HEADS // SSM_GROUPS
    ms = []
    for h in range(SSM_HEADS):
        ms.append((cbs[h // heads_per_group]
                   * jnp.exp2(jnp.where(causal, acs[:, h:h + 1] - acs_t[h:h + 1, :], NEG_INF))).astype(BF16))
        if h % 4 == 3:
            yield
    rhs = []
    for j in range(SSM_HEADS // 2):
        xpair = xd_bf[:, j * LANES:(j + 1) * LANES]
        zero = jnp.zeros_like(xpair)
        rhs.append(jnp.concatenate([jnp.where(pair_lane < SSM_HEAD_DIM, xpair, zero),
                                    jnp.where(pair_lane >= SSM_HEAD_DIM, xpair, zero)], axis=0))
    ydiag = [_dot(jnp.concatenate([ms[2 * j], ms[2 * j + 1]], axis=1), rhs[j]) for j in range(SSM_HEADS // 2)]
    yield
    y = jnp.concatenate(ydiag, axis=1) + jnp.concatenate(yoff, axis=1) * ea_x + dskip_ref[...] * xs

    zz = z_ref[...].astype(F32)
    y = y * (zz * _sigmoid(zz))
    y_ref[...] = _rms(y, nw_ref[...]).astype(BF16)


def _ssd_kernel(*refs):
    u_scr, st_scr = refs[-2:]

    @pl.when(pl.program_id(1) == 0)
    def _():
        u_scr[...] = jnp.zeros_like(u_scr)
        st_scr[...] = jnp.zeros_like(st_scr)

    for _ in _ssd_phases(*refs):
        pass


def _ssd(xbc, z, gates, convw, convb, dtb, aneg, dskip_x, nw, sel, b, s):
    L = min(MIX_CHUNK, s)
    nc = s // L
    tok = lambda bi, ci: (bi * nc + ci, 0)
    const = lambda bi, ci: (0, 0)
    return pl.pallas_call(
        _ssd_kernel,
        grid=(b, nc),
        in_specs=[
            pl.BlockSpec((L, CONV_DIM), tok),
            pl.BlockSpec((L, SSM_WIDTH), tok),
            pl.BlockSpec((L, LANES), tok),
            pl.BlockSpec((CONV_WIDTH, CONV_DIM), const),
            pl.BlockSpec((1, CONV_DIM), const),
            pl.BlockSpec((1, LANES), const),
            pl.BlockSpec((1, LANES), const),
            pl.BlockSpec((1, SSM_WIDTH), const),
            pl.BlockSpec((1, SSM_WIDTH), const),
            pl.BlockSpec((LANES, SSM_WIDTH), const),
        ],
        out_specs=pl.BlockSpec((L, SSM_WIDTH), tok),
        out_shape=jax.ShapeDtypeStruct((b * s, SSM_WIDTH), BF16),
        scratch_shapes=[pltpu.VMEM((SUBLANES, CONV_DIM), F32),
                        pltpu.VMEM((SSM_GROUPS, SSM_STATE, SSM_WIDTH // SSM_GROUPS), F32)],
        compiler_params=_cparams(("arbitrary", "arbitrary")),
        name="ssd",
    )(xbc, z, gates, convw, convb, dtb, aneg, dskip_x, nw, sel)


def _mlstm_phases(q_ref, k_ref, v_ref, o_ref, gi_ref, gf_ref, ib_ref, fb_ref, nw_ref, y_ref,
                  st_scr, m_scr):
    L = q_ref.shape[0]
    dh = LSTM_HEAD_DIM
    scale = dh ** -0.5

    row = lax.broadcasted_iota(jnp.int32, (L, L), 0)
    col = lax.broadcasted_iota(jnp.int32, (L, L), 1)
    causal = row >= col
    tril_bf = causal.astype(F32).astype(BF16)
    lane = lax.broadcasted_iota(jnp.int32, (1, LANES), 1)
    live = lane < LSTM_HEADS

    ii = jnp.where(live, gi_ref[...] + ib_ref[...], 0.0)
    logf = jnp.where(live, -_softplus(-(gf_ref[...] + fb_ref[...])), 0.0)
    cumf = _sel_dot(tril_bf, logf)
    g = ii - cumf
    rid = lax.broadcasted_iota(jnp.int32, (L, LANES), 0)
    cmx = g
    step = 1
    while step < L:
        cmx = jnp.maximum(cmx, jnp.where(rid >= step, pltpu.roll(cmx, step, axis=0), NEG_INF))
        step *= 2
    m_prev = m_scr[...]
    mx = jnp.maximum(m_prev, cmx)
    w_inter = jnp.exp(m_prev - mx)
    enm = jnp.exp(-(cumf + mx))
    g2_t = (g * LOG2E).T
    mx2 = mx * LOG2E - np.log2(scale)
    m_last = mx[L - 1:L, :]
    wk = jnp.exp(g - m_last) * scale
    sc = jnp.exp(m_prev - m_last)
    m_scr[...] = cumf[L - 1:L, :] + m_last
    yield

    ones_bf = jnp.ones((L, dh), BF16)
    heads = range(LSTM_HEADS)
    hsl = [slice(h * dh, (h + 1) * dh) for h in heads]
    vaug = [jnp.concatenate([v_ref[:, hsl[h]], ones_bf], axis=1) for h in heads]
    r2 = [_dot(q_ref[:, hsl[h]], st_scr[h].astype(BF16)) for h in heads]
    upd = [_dot_tn((k_ref[:, hsl[h]].astype(F32) * wk[:, h:h + 1]).astype(BF16), vaug[h]) for h in heads]
    for h in heads:
        st_scr[h] = st_scr[h] * sc[:, h:h + 1] + upd[h]
    yield
    s_qk = [_dot_nt(q_ref[:, hsl[h]], k_ref[:, hsl[h]]) for h in heads]
    yield
    p = []
    for h in heads:
        p.append((s_qk[h] * jnp.exp2(jnp.where(causal, g2_t[h:h + 1, :] - mx2[:, h:h + 1], NEG_INF))).astype(BF16))
        if h % 2 == 1:
            yield
    r1 = [_dot(p[h], vaug[h]) for h in heads]
    yield
    hh = []
    for h in heads:
        wcol = w_inter[:, h:h + 1]
        num = r1[h][:, :dh] + r2[h][:, :dh] * wcol
        den = r1[h][:, dh:] + r2[h][:, dh:] * wcol
        hh.append(num / jnp.maximum(jnp.abs(den), enm[:, h:h + 1]))
    inv = [lax.rsqrt(jnp.mean(hh[h] * hh[h], axis=-1, keepdims=True) + RMS_EPS) for h in heads]
    for h in heads:
        oo = o_ref[:, hsl[h]].astype(F32)
        y_ref[:, hsl[h]] = (_sigmoid(oo) * (hh[h] * inv[h] * nw_ref[:, hsl[h]])).astype(BF16)


N_SSD_IN, N_LSTM_IN = 10, 9


def _mixer_kernel(*refs):
    ssd_in = refs[:N_SSD_IN]
    lstm_in = refs[N_SSD_IN:N_SSD_IN + N_LSTM_IN]
    y_ssd, y_lstm, u_scr, st_ssd, st_lstm, m_scr = refs[N_SSD_IN + N_LSTM_IN:]

    @pl.when(pl.program_id(1) == 0)
    def _():
        u_scr[...] = jnp.zeros_like(u_scr)
        st_ssd[...] = jnp.zeros_like(st_ssd)
        st_lstm[...] = jnp.zeros_like(st_lstm)
        m_scr[...] = jnp.zeros_like(m_scr)

    streams = [_ssd_phases(*ssd_in, y_ssd, u_scr, st_ssd), _mlstm_phases(*lstm_in, y_lstm, st_lstm, m_scr)]
    while streams:
        for stream in list(streams):
            if next(stream, StopIteration) is StopIteration:
                streams.remove(stream)


def _mixers(xbc, z, gates, convw, convb, dtb, aneg, dskip_x, nw_ssd, sel, q, k, v, o, ib, fb, nw_lstm, b, s):
    L = min(MIX_CHUNK, s)
    nc = s // L
    tok = lambda bi, ci: (bi * nc + ci, 0)
    const = lambda bi, ci: (0, 0)
    big = pl.BlockSpec((L, LSTM_WIDTH), tok)
    gate_block = lambda j: pl.BlockSpec((L, LANES), lambda bi, ci: (bi * nc + ci, j))
    vec = lambda w: pl.BlockSpec((1, w), const)
    in_specs = [pl.BlockSpec((L, CONV_DIM), tok), pl.BlockSpec((L, SSM_WIDTH), tok), gate_block(0),
                pl.BlockSpec((CONV_WIDTH, CONV_DIM), const), vec(CONV_DIM), vec(LANES), vec(LANES),
                vec(SSM_WIDTH), vec(SSM_WIDTH), pl.BlockSpec((LANES, SSM_WIDTH), const),
                big, big, big, big, gate_block(1), gate_block(2), vec(LANES), vec(LANES), vec(LSTM_WIDTH)]
    assert len(in_specs) == N_SSD_IN + N_LSTM_IN
    return pl.pallas_call(
        _mixer_kernel,
        grid=(b, nc),
        in_specs=in_specs,
        out_specs=[pl.BlockSpec((L, SSM_WIDTH), tok), big],
        out_shape=[jax.ShapeDtypeStruct((b * s, SSM_WIDTH), BF16), jax.ShapeDtypeStruct((b * s, LSTM_WIDTH), BF16)],
        scratch_shapes=[pltpu.VMEM((SUBLANES, CONV_DIM), F32),
                        pltpu.VMEM((SSM_GROUPS, SSM_STATE, SSM_WIDTH // SSM_GROUPS), F32),
                        pltpu.VMEM((LSTM_HEADS, LSTM_HEAD_DIM, 2 * LSTM_HEAD_DIM), F32),
                        pltpu.VMEM((1, LANES), F32)],
        compiler_params=_cparams(("arbitrary", "arbitrary")),
        name="mixers",
    )(xbc, z, gates, convw, convb, dtb, aneg, dskip_x, nw_ssd, sel, q, k, v, o, gates, gates, ib, fb, nw_lstm)


def _mlstm_kernel(*refs):
    st_scr, m_scr = refs[-2:]

    @pl.when(pl.program_id(1) == 0)
    def _():
        st_scr[...] = jnp.zeros_like(st_scr)
        m_scr[...] = jnp.zeros_like(m_scr)

    for _ in _mlstm_phases(*refs):
        pass


def _mlstm(q, k, v, o, gates, ib, fb, nw, b, s):
    L = min(MIX_CHUNK, s)
    nc = s // L
    tok = lambda bi, ci: (bi * nc + ci, 0)
    const = lambda bi, ci: (0, 0)
    big = pl.BlockSpec((L, LSTM_WIDTH), tok)
    return pl.pallas_call(
        _mlstm_kernel,
        grid=(b, nc),
        in_specs=[big, big, big, big,
                  pl.BlockSpec((L, LANES), lambda bi, ci: (bi * nc + ci, 1)),
                  pl.BlockSpec((L, LANES), lambda bi, ci: (bi * nc + ci, 2)),
                  pl.BlockSpec((1, LANES), const),
                  pl.BlockSpec((1, LANES), const),
                  pl.BlockSpec((1, LSTM_WIDTH), const)],
        out_specs=big,
        out_shape=jax.ShapeDtypeStruct((b * s, LSTM_WIDTH), BF16),
        scratch_shapes=[pltpu.VMEM((LSTM_HEADS, LSTM_HEAD_DIM, 2 * LSTM_HEAD_DIM), F32),
                        pltpu.VMEM((1, LANES), F32)],
        compiler_params=_cparams(("arbitrary", "arbitrary")),
        name="mlstm",
    )(q, k, v, o, gates, gates, ib, fb, nw)


META_IDX, META_GATE, META_RANK = 0, TOP_K, 2 * TOP_K


def _outproj_kernel(ys_ref, yl_ref, x_ref, wo_ref, nw_ref, wr_ref, br_ref, h_ref, hn_ref, meta_ref,
                    cnt_ref, cnt_scr, logit_scr):
    tm = x_ref.shape[0]
    step = pl.program_id(0)

    @pl.when(step == 0)
    def _():
        cnt_scr[...] = jnp.zeros_like(cnt_scr)
        logit_scr[...] = jnp.zeros_like(logit_scr)

    vals = logit_scr[...]
    routed = (step > 0).astype(F32)

    h = (x_ref[...] + _dot(ys_ref[...], wo_ref[:SSM_WIDTH, :]) + _dot(yl_ref[...], wo_ref[SSM_WIDTH:, :]))
    h_ref[...] = h
    hn = _rms(h, nw_ref[...])
    _store_row_tiles(hn_ref, hn)
    logit_scr[...] = _dot(hn.astype(BF16), wr_ref[...]) + br_ref[...]

    lane = lax.broadcasted_iota(jnp.int32, (tm, LANES), 1)
    member = jnp.zeros((tm, LANES), F32)
    tops, idxs, sels = [], [], []
    for _ in range(TOP_K):
        m = jnp.max(vals, axis=-1, keepdims=True)
        idx = jnp.min(jnp.where(vals == m, lane, LANES), axis=-1, keepdims=True)
        sel = lane == idx
        vals = jnp.where(sel, NEG_INF, vals)
        member = member + sel.astype(F32)
        tops.append(m)
        idxs.append(idx)
        sels.append(sel)
    es = [jnp.exp(t - tops[0]) for t in tops]
    inv = 1.0 / (es[0] + es[1] + es[2] + es[3])
    member = member * routed

    r = lax.broadcasted_iota(jnp.int32, (tm, tm), 0)
    c = lax.broadcasted_iota(jnp.int32, (tm, tm), 1)
    strict = (r > c).astype(F32).astype(BF16)
    carry = cnt_scr[0:1, :]
    rank_all = _dot(strict, member.astype(BF16)) + carry
    total = carry + jnp.sum(member, axis=0, keepdims=True)
    cnt_scr[...] = jnp.broadcast_to(total, cnt_scr.shape)
    cnt_ref[...] = jnp.broadcast_to(total, cnt_ref.shape)

    meta = jnp.zeros((tm, LANES), F32)
    for kk in range(TOP_K):
        rank = jnp.sum(jnp.where(sels[kk], rank_all, 0.0), axis=-1, keepdims=True)
        meta = jnp.where(lane == META_IDX + kk, idxs[kk].astype(F32), meta)
        meta = jnp.where(lane == META_GATE + kk, es[kk] * inv, meta)
        meta = jnp.where(lane == META_RANK + kk, rank, meta)
    meta_ref[...] = meta


def _outproj(ys, yl, x2, wo, nw, wr, br):
    t = x2.shape[0]
    tm = min(ROW_TILE, t)
    n = t // tm
    projected = lambda i: (jnp.minimum(i, n - 1), 0)
    routed = lambda i: (jnp.maximum(i - 1, 0), 0)
    tokspec = lambda w: pl.BlockSpec((tm, w), projected)
    const = lambda i: (0, 0)
    return pl.pallas_call(
        _outproj_kernel,
        grid=(n + 1,),
        in_specs=[tokspec(SSM_WIDTH), tokspec(LSTM_WIDTH), tokspec(D_MODEL),
                  pl.BlockSpec((SSM_WIDTH + LSTM_WIDTH, D_MODEL), const, pipeline_mode=pl.Buffered(1)),
                  pl.BlockSpec((1, D_MODEL), const),
                  pl.BlockSpec((D_MODEL, LANES), const),
                  pl.BlockSpec((1, LANES), const)],
        out_specs=[tokspec(D_MODEL), pl.BlockSpec((tm * ROW_SUBTILES, LANES), projected),
                   pl.BlockSpec((tm, LANES), routed), pl.BlockSpec((SUBLANES, LANES), const)],
        out_shape=[jax.ShapeDtypeStruct((t, D_MODEL), F32), jax.ShapeDtypeStruct((t * ROW_SUBTILES, LANES), F32),
                   jax.ShapeDtypeStruct((t, LANES), F32), jax.ShapeDtypeStruct((SUBLANES, LANES), F32)],
        scratch_shapes=[pltpu.VMEM((SUBLANES, LANES), F32), pltpu.VMEM((tm, LANES), F32)],
        compiler_params=_cparams(("arbitrary",)),
        name="outproj_router",
    )(ys, yl, x2, wo, nw, wr, br)


DISPATCH_TILE = 1024
DISPATCH_GROUP = 4
PAD_CHUNKS = (128, 64, 32, 16, 8, 4, 2, 1)


def _dispatch_kernel(padstart_ref, padlen_ref, misc_ref, pos_ref, hn_ref, xs_hbm, sem, zsem, zeros_scr):
    i = pl.program_id(0)
    td = pos_ref.shape[-1] // TOP_K

    @pl.when(i == 0)
    def _():
        zeros_scr[...] = jnp.zeros_like(zeros_scr)
        for e in range(N_EXPERTS):
            row = padstart_ref[e]
            nrow = padlen_ref[e]
            for chunk in PAD_CHUNKS:
                @pl.when((nrow & chunk) != 0)
                def _(row=row, chunk=chunk):
                    pltpu.make_async_copy(
                        zeros_scr.at[pl.ds(0, chunk * ROW_SUBTILES), :],
                        xs_hbm.at[pl.ds(pl.multiple_of(row * ROW_SUBTILES, ROW_SUBTILES), chunk * ROW_SUBTILES), :],
                        zsem).start()
                row = row + (nrow & chunk)

        def tail_copy(blk):
            return pltpu.make_async_copy(
                zeros_scr,
                xs_hbm.at[pl.ds(pl.multiple_of(blk * (MOE_BLOCK * ROW_SUBTILES), ROW_SUBTILES),
                                MOE_BLOCK * ROW_SUBTILES), :], zsem)

        n_used, n_blocks = misc_ref[1], misc_ref[2]

        def start_tail(blk, carry):
            tail_copy(blk).start()
            return carry
        lax.fori_loop(n_used, n_blocks, start_tail, 0)

        def drain_row(j, carry):
            _wait_row_copies(zeros_scr, xs_hbm, 1, zsem)
            return carry
        lax.fori_loop(0, misc_ref[0], drain_row, 0)

        def drain_tail(blk, carry):
            tail_copy(blk).wait()
            return carry
        lax.fori_loop(n_used, n_blocks, drain_tail, 0)

    def body(g, carry):
        r0 = g * DISPATCH_GROUP
        slots = [pos_ref[0, 0, r0 * TOP_K + j] for j in range(DISPATCH_GROUP * TOP_K)]
        for j, slot in enumerate(slots):
            _row_copy(hn_ref, r0 + j // TOP_K, xs_hbm, slot, sem).start(priority=j % DMA_PRIORITIES)
        return carry
    lax.fori_loop(0, td // DISPATCH_GROUP, body, 0)

    for _ in range(TOP_K):
        _wait_row_copies(hn_ref, xs_hbm, td, sem)


def _dispatch(pad_start, pad_len, misc, pos_flat, hn_rt, n_rows):
    t = pos_flat.shape[0] // TOP_K
    td = min(DISPATCH_TILE, t)
    n = t // td
    grid_spec = pltpu.PrefetchScalarGridSpec(
        num_scalar_prefetch=3,
        grid=(n,),
        in_specs=[pl.BlockSpec((1, 1, TOP_K * td), lambda i, *_: (i, 0, 0), memory_space=pltpu.SMEM),
                  pl.BlockSpec((td * ROW_SUBTILES, LANES), lambda i, *_: (i, 0))],
        out_specs=pl.BlockSpec(memory_space=pl.ANY),
        scratch_shapes=[pltpu.SemaphoreType.DMA(()), pltpu.SemaphoreType.DMA(()),
                        pltpu.VMEM((MOE_BLOCK * ROW_SUBTILES, LANES), F32)],
    )
    return pl.pallas_call(
        _dispatch_kernel,
        grid_spec=grid_spec,
        out_shape=jax.ShapeDtypeStruct((n_rows * ROW_SUBTILES, LANES), F32),
        compiler_params=_cparams(("arbitrary",)),
        name="dispatch",
    )(pad_start, pad_len, misc, pos_flat.reshape(n, 1, TOP_K * td), hn_rt)


def _experts_kernel(be_ref, nused_ref, next_ref, x_ref, wgu_hbm, bgu_ref, wdn_hbm, bdn_ref, out_ref,
                    wgu_bf, wdn_bf, wgu_st, wdn_st, wsem, slot_ref):
    i = pl.program_id(0)
    n_used = nused_ref[0]

    def weight_copies(e, slot):
        return (pltpu.make_async_copy(wgu_hbm.at[e], wgu_st.at[slot], wsem.at[0, slot]),
                pltpu.make_async_copy(wdn_hbm.at[e], wdn_st.at[slot], wsem.at[1, slot]))

    @pl.when(i == 0)
    def _():
        slot_ref[0] = 0
        for cp in weight_copies(be_ref[0], 0):
            cp.start()

    @pl.when(i < n_used)
    def _():
        @pl.when(jnp.logical_or(i == 0, be_ref[i] != be_ref[jnp.maximum(i - 1, 0)]))
        def _():
            e = be_ref[i]
            slot = slot_ref[0]
            for cp in weight_copies(e, slot):
                cp.wait()
            wgu_bf[...] = wgu_st[slot].astype(BF16)
            wdn_bf[...] = wdn_st[slot].astype(BF16)
            nxt = next_ref[e]

            @pl.when(nxt >= 0)
            def _():
                for cp in weight_copies(nxt, 1 - slot):
                    cp.start()
            slot_ref[0] = 1 - slot

        xb = jnp.concatenate(_load_row_tile_cols(x_ref, MOE_BLOCK), axis=1).astype(BF16)
        hh = _dot(xb, wgu_bf[...]) + bgu_ref[...]
        gh = jnp.minimum(hh[:, :D_FF], SWIGLU_LIMIT)
        uh = jnp.clip(hh[:, D_FF:], -SWIGLU_LIMIT, SWIGLU_LIMIT)
        act = (uh + 1.0) * (gh * _sigmoid(SWIGLU_ALPHA * gh))
        _store_row_tiles(out_ref, _dot(act.astype(BF16), wdn_bf[...]) + bdn_ref[...])

    @pl.when(i >= n_used)
    def _():
        out_ref[...] = jnp.zeros_like(out_ref)


def _experts(block_expert, n_used, next_expert, xs_rt, wgu, bgu3, wdn, bdn3):
    n_blocks = block_expert.shape[0]
    blk = MOE_BLOCK * ROW_SUBTILES
    grid_spec = pltpu.PrefetchScalarGridSpec(
        num_scalar_prefetch=3,
        grid=(n_blocks,),
        in_specs=[
            pl.BlockSpec((blk, LANES), lambda i, be, nu, nx: (i, 0)),
            pl.BlockSpec(memory_space=pl.ANY),
            pl.BlockSpec((None, 1, 2 * D_FF), lambda i, be, nu, nx: (be[i], 0, 0)),
            pl.BlockSpec(memory_space=pl.ANY),
            pl.BlockSpec((None, 1, D_MODEL), lambda i, be, nu, nx: (be[i], 0, 0)),
        ],
        out_specs=pl.BlockSpec((blk, LANES), lambda i, be, nu, nx: (i, 0)),
        scratch_shapes=[pltpu.VMEM((D_MODEL, 2 * D_FF), BF16),
                        pltpu.VMEM((D_FF, D_MODEL), BF16),
                        pltpu.VMEM((2, D_MODEL, 2 * D_FF), F32),
                        pltpu.VMEM((2, D_FF, D_MODEL), F32),
                        pltpu.SemaphoreType.DMA((2, 2)),
                        pltpu.SMEM((1,), jnp.int32)],
    )
    return pl.pallas_call(
        _experts_kernel,
        grid_spec=grid_spec,
        out_shape=jax.ShapeDtypeStruct((n_blocks * blk, LANES), F32),
        compiler_params=_cparams(("arbitrary",)),
        name="experts",
    )(block_expert, n_used, next_expert, xs_rt, wgu, bgu3, wdn, bdn3)


def _combine_kernel(pos_cur, pos_next, outs_hbm, h_ref, meta_ref, nw_ref, y_ref, buf, sem):
    i = pl.program_id(0)
    n = pl.num_programs(0)
    tc = h_ref.shape[0]
    slot = i % 2

    def issue(pos_ref, s):
        def body(r, carry):
            for kk in range(TOP_K):
                _row_copy(outs_hbm, pos_ref[0, 0, r * TOP_K + kk], buf.at[s, kk], r,
                          sem.at[s]).start(priority=kk % DMA_PRIORITIES)
            return carry
        lax.fori_loop(0, tc, body, 0, unroll=4)

    @pl.when(i == 0)
    def _():
        issue(pos_cur, 0)

    @pl.when(i + 1 < n)
    def _():
        issue(pos_next, 1 - slot)

    for kk in range(TOP_K):
        _wait_row_copies(outs_hbm, buf.at[slot, kk], tc, sem.at[slot])
    meta = meta_ref[...]
    gates = [meta[:, META_GATE + kk:META_GATE + kk + 1] for kk in range(TOP_K)]
    rows = [_load_row_tile_cols(buf.at[slot, kk], tc) for kk in range(TOP_K)]
    cols = []
    for s in range(ROW_SUBTILES):
        acc = h_ref[:, s * LANES:(s + 1) * LANES]
        for kk in range(TOP_K):
            acc = acc + gates[kk] * rows[kk][s]
        cols.append(acc)
    y_ref[...] = _rms(jnp.concatenate(cols, axis=1), nw_ref[...])


def _combine(pos_flat, outs_rt, h, meta, nw):
    t = h.shape[0]
    tc = min(COMBINE_TILE, t)
    n = t // tc
    pos3 = pos_flat.reshape(n, 1, TOP_K * tc)
    return pl.pallas_call(
        _combine_kernel,
        grid=(n,),
        in_specs=[
            pl.BlockSpec((1, 1, TOP_K * tc), lambda i: (i, 0, 0), memory_space=pltpu.SMEM),
            pl.BlockSpec((1, 1, TOP_K * tc), lambda i: (jnp.minimum(i + 1, n - 1), 0, 0),
                         memory_space=pltpu.SMEM),
            pl.BlockSpec(memory_space=pl.ANY),
            pl.BlockSpec((tc, D_MODEL), lambda i: (i, 0)),
            pl.BlockSpec((tc, LANES), lambda i: (i, 0)),
            pl.BlockSpec((1, D_MODEL), lambda i: (0, 0)),
        ],
        out_specs=pl.BlockSpec((tc, D_MODEL), lambda i: (i, 0)),
        out_shape=jax.ShapeDtypeStruct((t, D_MODEL), F32),
        scratch_shapes=[pltpu.VMEM((2, TOP_K, tc * ROW_SUBTILES, LANES), F32), pltpu.SemaphoreType.DMA((2,))],
        compiler_params=_cparams(("arbitrary",)),
        name="combine",
    )(pos3, pos3, outs_rt, h, meta, nw)


def _pad_lanes(v, fill=0.0):
    v = v.astype(F32).reshape(1, -1)
    return jnp.pad(v, ((0, 0), (0, LANES - v.shape[1])), constant_values=fill)


def kernel(x, norm_mix_w, w_in, conv_w, conv_b, dt_bias, a_log, d_skip, ssm_norm_w, lstm_i_bias,
           lstm_f_bias, lstm_norm_w, w_out, norm_ffn_w, w_router, b_router, w_gate_up, b_gate_up,
           w_down, b_down, norm_final_w):
    b, s, d = x.shape
    t = b * s
    x2 = x.reshape(t, d).astype(F32)
    depth = w_in.shape[0]
    assert depth == 1, "the combine kernel fuses the final norm, so exactly one layer is supported"
    for layer in range(depth):
        wi = w_in[layer]
        c0 = SSM_WIDTH
        c1 = c0 + CONV_DIM
        c2 = c1 + SSM_HEADS
        c3 = c2 + 4 * LSTM_WIDTH
        c4 = c3 + LSTM_HEADS
        padc = lambda w: jnp.pad(w, ((0, 0), (0, LANES - w.shape[1])))
        w_all = jnp.concatenate([wi[:, :c1], wi[:, c2:c3], padc(wi[:, c1:c2]), padc(wi[:, c3:c4]),
                                 padc(wi[:, c4:])], axis=1).astype(BF16)
        sel = (jnp.arange(LANES)[:, None] == (jnp.arange(SSM_WIDTH) // SSM_HEAD_DIM)[None, :]).astype(BF16)
        a_neg = _pad_lanes(-jnp.exp(a_log[layer].astype(F32)))
        dskip_x = jnp.repeat(d_skip[layer].astype(F32), SSM_HEAD_DIM).reshape(1, SSM_WIDTH)
        wr = jnp.pad(w_router[layer], ((0, 0), (0, LANES - N_EXPERTS))).astype(BF16)
        br = _pad_lanes(b_router[layer], fill=NEG_INF)

        z, xbc, q, k, v, o, gates = _inproj(x2, norm_mix_w[layer].reshape(1, d).astype(F32), w_all)
        y_ssd, y_lstm = _mixers(
            xbc, z, gates, conv_w[layer].astype(F32), conv_b[layer].reshape(1, -1).astype(F32),
            _pad_lanes(dt_bias[layer]), a_neg, dskip_x, ssm_norm_w[layer].reshape(1, -1).astype(F32), sel,
            q, k, v, o, _pad_lanes(lstm_i_bias[layer]), _pad_lanes(lstm_f_bias[layer]),
            lstm_norm_w[layer].reshape(1, -1).astype(F32), b, s)
        h, hn_rt, meta, cnt = _outproj(y_ssd, y_lstm, x2, w_out[layer].astype(BF16),
                                       norm_ffn_w[layer].reshape(1, d).astype(F32), wr, br)

        idx = meta[:, META_IDX:META_IDX + TOP_K].astype(jnp.int32)
        rank = meta[:, META_RANK:META_RANK + TOP_K].astype(jnp.int32)
        counts = cnt[0, :N_EXPERTS].astype(jnp.int32)
        n_blocks = -(-(t * TOP_K) // MOE_BLOCK) + N_EXPERTS
        padded = (counts + MOE_BLOCK - 1) // MOE_BLOCK * MOE_BLOCK
        padded_ends = jnp.cumsum(padded)
        padded_starts = padded_ends - padded
        onehot = idx[..., None] == jnp.arange(N_EXPERTS, dtype=jnp.int32)
        pos = (jnp.sum(jnp.where(onehot, padded_starts, 0), axis=-1) + rank).reshape(-1)
        block_start = jnp.arange(n_blocks, dtype=jnp.int32) * MOE_BLOCK
        block_expert = jnp.minimum(
            jnp.sum((padded_ends[None, :] <= block_start[:, None]).astype(jnp.int32), axis=1), N_EXPERTS - 1)
        n_used = padded_ends[-1:] // MOE_BLOCK
        pad_len = padded - counts
        misc = jnp.concatenate([jnp.sum(pad_len, keepdims=True), n_used,
                                jnp.full((1,), n_blocks, jnp.int32)])

        xs_rt = _dispatch(padded_starts + counts, pad_len, misc, pos, hn_rt, n_blocks * MOE_BLOCK)
        eids = jnp.arange(N_EXPERTS, dtype=jnp.int32)
        later_nonempty = (eids[None, :] > eids[:, None]) & (counts[None, :] > 0)
        next_expert = jnp.min(jnp.where(later_nonempty, eids[None, :], N_EXPERTS), axis=1)
        next_expert = jnp.where(next_expert < N_EXPERTS, next_expert, -1)
        outs_rt = _experts(block_expert, n_used, next_expert, xs_rt, w_gate_up[layer],
                           b_gate_up[layer].reshape(N_EXPERTS, 1, -1), w_down[layer],
                           b_down[layer].reshape(N_EXPERTS, 1, -1))
        x2 = _combine(pos, outs_rt, h, meta, norm_final_w.reshape(1, d).astype(F32))
    return x2.reshape(b, s, d).astype(x.dtype)
```

```python
import functools

import jax
import jax.numpy as jnp
import numpy as np
from jax import lax
from jax.experimental import pallas as pl
from jax.experimental.pallas import tpu as pltpu

F32 = jnp.float32
BF16 = jnp.bfloat16

D_MODEL = 1024
SSM_WIDTH = 1024
SSM_HEAD_DIM = 64
SSM_HEADS = 16
SSM_GROUPS = 2
SSM_STATE = 128
CONV_WIDTH = 4
CONV_DIM = SSM_WIDTH + 2 * SSM_GROUPS * SSM_STATE
LSTM_WIDTH = 1024
LSTM_HEAD_DIM = 128
LSTM_HEADS = 8
N_EXPERTS = 32
TOP_K = 4
D_FF = 1024
SWIGLU_LIMIT = 7.0
SWIGLU_ALPHA = 1.702
MOE_BLOCK = 256
RMS_EPS = 1e-6

LANES = 128
SUBLANES = 8
MIX_CHUNK = 256
ROW_TILE = 512
COMBINE_TILE = 512
VMEM_LIMIT = 56 * 1024 * 1024

GATE_COLS = 3 * LANES
NEG_INF = float("-inf")


def _cparams(sem):
    return pltpu.CompilerParams(dimension_semantics=sem, vmem_limit_bytes=VMEM_LIMIT)


def _rms(x, w):
    return x * lax.rsqrt(jnp.mean(x * x, axis=-1, keepdims=True) + RMS_EPS) * w


LOG2E = 1.4426950408889634


def _sigmoid(x):
    return 1.0 / (1.0 + jnp.exp2(x * -LOG2E))


def _softplus(x):
    return jnp.maximum(x, 0.0) + jnp.log(1.0 + jnp.exp(-jnp.abs(x)))


def _split3(a):
    hi = a.astype(BF16)
    r = a - hi.astype(F32)
    mid = r.astype(BF16)
    lo = (r - mid.astype(F32)).astype(BF16)
    return hi, mid, lo


def _dot(a, b):
    return jnp.dot(a, b, preferred_element_type=F32)


def _dot_nt(a, b):
    return lax.dot_general(a, b, (((1,), (1,)), ((), ())), preferred_element_type=F32)


def _dot_tn(a, b):
    return lax.dot_general(a, b, (((0,), (0,)), ((), ())), preferred_element_type=F32)


def _sel_dot(sel_bf, a):
    hi, mid, lo = _split3(a)
    return _dot(sel_bf, hi) + _dot(sel_bf, mid) + _dot(sel_bf, lo)


def _expand(a, sel_bf, terms=3):
    out = None
    for piece in _split3(a)[:terms]:
        d = _dot(piece, sel_bf)
        out = d if out is None else out + d
    return out


ROW_SUBTILES = D_MODEL // LANES
DMA_PRIORITIES = 2


def _store_row_tiles(ref, x):
    n = x.shape[0]
    for s in range(ROW_SUBTILES):
        ref[pl.ds(s, n, stride=ROW_SUBTILES), :] = x[:, s * LANES:(s + 1) * LANES]


def _load_row_tile_cols(ref, n):
    return [ref[pl.ds(s, n, stride=ROW_SUBTILES), :] for s in range(ROW_SUBTILES)]


def _row_copy(src, src_row, dst, dst_row, sem):
    return pltpu.make_async_copy(
        src.at[pl.ds(pl.multiple_of(src_row * ROW_SUBTILES, ROW_SUBTILES), ROW_SUBTILES), :],
        dst.at[pl.ds(pl.multiple_of(dst_row * ROW_SUBTILES, ROW_SUBTILES), ROW_SUBTILES), :], sem)


def _wait_row_copies(src, dst, n_rows, sem):
    size = n_rows * ROW_SUBTILES
    pltpu.make_async_copy(src.at[pl.ds(0, size), :], dst.at[pl.ds(0, size), :], sem).wait()


_INPROJ_WIDTHS = (SSM_WIDTH, CONV_DIM, LSTM_WIDTH, LSTM_WIDTH, LSTM_WIDTH, LSTM_WIDTH, GATE_COLS)


def _inproj_kernel(x_ref, nw_ref, w_ref, z_ref, xbc_ref, q_ref, k_ref, v_ref, o_ref, g_ref):
    xb = _rms(x_ref[...], nw_ref[...]).astype(BF16)
    off = 0
    for ref, width in zip((z_ref, xbc_ref, q_ref, k_ref, v_ref, o_ref, g_ref), _INPROJ_WIDTHS):
        ref[...] = _dot(xb, w_ref[:, off:off + width]).astype(ref.dtype)
        off += width


def _inproj(x2, nw, w_all):
    t = x2.shape[0]
    tm = min(ROW_TILE, t)
    ncol = w_all.shape[1]
    out_shape = [jax.ShapeDtypeStruct((t, w), BF16) for w in _INPROJ_WIDTHS[:-1]]
    out_shape.append(jax.ShapeDtypeStruct((t, GATE_COLS), F32))
    return pl.pallas_call(
        _inproj_kernel,
        grid=(t // tm,),
        in_specs=[
            pl.BlockSpec((tm, D_MODEL), lambda i: (i, 0)),
            pl.BlockSpec((1, D_MODEL), lambda i: (0, 0)),
            pl.BlockSpec((D_MODEL, ncol), lambda i: (0, 0), pipeline_mode=pl.Buffered(1)),
        ],
        out_specs=[pl.BlockSpec((tm, w), lambda i: (i, 0)) for w in _INPROJ_WIDTHS],
        out_shape=out_shape,
        compiler_params=_cparams(("arbitrary",)),
        name="inproj",
    )(x2, nw, w_all)


SSD_PAIR_GROUP = 2
LSTM_HEAD_GROUP = 2


def _ssd_phases(xbc_ref, z_ref, g_ref, convw_ref, convb_ref, dtb_ref, aneg_ref, dskip_ref, nw_ref,
                sel_ref, y_ref, u_scr, st_scr):
    L = xbc_ref.shape[0]
    gw = SSM_WIDTH // SSM_GROUPS

    row = lax.broadcasted_iota(jnp.int32, (L, L), 0)
    col = lax.broadcasted_iota(jnp.int32, (L, L), 1)
    causal = row >= col

    xin_bf = xbc_ref[...]
    xin = xin_bf.astype(F32)
    halo = u_scr[...]
    row8 = lax.broadcasted_iota(jnp.int32, (SUBLANES, 1), 0)
    acc = convb_ref[...] + convw_ref[CONV_WIDTH - 1:CONV_WIDTH, :] * xin
    for back in range(1, CONV_WIDTH):
        shifted = _dot((row - col == back).astype(F32).astype(BF16), xin_bf)
        head = shifted[:SUBLANES] + jnp.where(row8 < back, pltpu.roll(halo, back, axis=0), 0.0)
        shifted = jnp.concatenate([head, shifted[SUBLANES:]], axis=0)
        acc = acc + convw_ref[CONV_WIDTH - 1 - back:CONV_WIDTH - back, :] * shifted
    u_scr[...] = xin[L - SUBLANES:, :]
    xbc = acc * _sigmoid(acc)
    xs = xbc[:, :SSM_WIDTH]
    bm = xbc[:, SSM_WIDTH:SSM_WIDTH + SSM_GROUPS * SSM_STATE].astype(BF16)
    cm = xbc[:, SSM_WIDTH + SSM_GROUPS * SSM_STATE:].astype(BF16)
    yield

    tril_bf = causal.astype(F32).astype(BF16)

    lane = lax.broadcasted_iota(jnp.int32, (1, LANES), 1)
    dt = jnp.where(lane < SSM_HEADS, _softplus(g_ref[...] + dtb_ref[...]), 0.0)
    acs = _sel_dot(tril_bf, dt * aneg_ref[...]) * LOG2E
    acs_t = acs.T
    sel = sel_ref[...]
    acs_last = acs[L - 1:L, :]
    ea_x = _expand(jnp.exp2(acs), sel, terms=2)
    xd = xs * _expand(dt, sel, terms=1)
    xd_bf = xd.astype(BF16)
    xw = (xd * _expand(jnp.exp2(acs_last - acs), sel, terms=1)).astype(BF16)
    yield

    cbs = [_dot_nt(cm[:, g * SSM_STATE:(g + 1) * SSM_STATE], bm[:, g * SSM_STATE:(g + 1) * SSM_STATE])
           for g in range(SSM_GROUPS)]
    yoff = [_dot(cm[:, g * SSM_STATE:(g + 1) * SSM_STATE], st_scr[g].astype(BF16)) for g in range(SSM_GROUPS)]
    upd = [_dot_tn(bm[:, g * SSM_STATE:(g + 1) * SSM_STATE], xw[:, g * gw:(g + 1) * gw])
           for g in range(SSM_GROUPS)]
    for g in range(SSM_GROUPS):
        st_scr[g] = st_scr[g] * ea_x[L - 1:L, g * gw:(g + 1) * gw] + upd[g]
    yield

    pair_lane = lax.broadcasted_iota(jnp.int32, (1, LANES), 1)
    heads_per_group = SSM_HEADS // SSM_GROUPS
    rhs = []
    for j in range(SSM_HEADS // 2):
        xpair = xd_bf[:, j * LANES:(j + 1) * LANES]
        zero = jnp.zeros_like(xpair)
        rhs.append(jnp.concatenate([jnp.where(pair_lane < SSM_HEAD_DIM, xpair, zero),
                                    jnp.where(pair_lane >= SSM_HEAD_DIM, xpair, zero)], axis=0))
    ydiag = []
    for j0 in range(0, SSM_HEADS // 2, SSD_PAIR_GROUP):
        pairs = range(j0, j0 + SSD_PAIR_GROUP)
        ms = {h: (cbs[h // heads_per_group]
                  * jnp.exp2(jnp.where(causal, acs[:, h:h + 1] - acs_t[h:h + 1, :], NEG_INF))).astype(BF16)
              for j in pairs for h in (2 * j, 2 * j + 1)}
        yield
        ydiag += [_dot(jnp.concatenate([ms[2 * j], ms[2 * j + 1]], axis=1), rhs[j]) for j in pairs]
        yield
    y = jnp.concatenate(ydiag, axis=1) + jnp.concatenate(yoff, axis=1) * ea_x + dskip_ref[...] * xs

    zz = z_ref[...].astype(F32)
    y = y * (zz * _sigmoid(zz))
    y_ref[...] = _rms(y, nw_ref[...]).astype(BF16)


def _mlstm_phases(q_ref, k_ref, v_ref, o_ref, gi_ref, gf_ref, ib_ref, fb_ref, nw_ref, y_ref,
                  st_scr, m_scr):
    L = q_ref.shape[0]
    dh = LSTM_HEAD_DIM
    scale = dh ** -0.5

    row = lax.broadcasted_iota(jnp.int32, (L, L), 0)
    col = lax.broadcasted_iota(jnp.int32, (L, L), 1)
    causal = row >= col
    tril_bf = causal.astype(F32).astype(BF16)
    lane = lax.broadcasted_iota(jnp.int32, (1, LANES), 1)
    live = lane < LSTM_HEADS

    ii = jnp.where(live, gi_ref[...] + ib_ref[...], 0.0)
    logf = jnp.where(live, -_softplus(-(gf_ref[...] + fb_ref[...])), 0.0)
    cumf = _sel_dot(tril_bf, logf)
    g = ii - cumf
    rid = lax.broadcasted_iota(jnp.int32, (L, LANES), 0)
    cmx = g
    step = 1
    while step < L:
        cmx = jnp.maximum(cmx, jnp.where(rid >= step, pltpu.roll(cmx, step, axis=0), NEG_INF))
        step *= 2
    m_prev = m_scr[...]
    mx = jnp.maximum(m_prev, cmx)
    w_inter = jnp.exp(m_prev - mx)
    enm = jnp.exp(-(cumf + mx))
    g2_t = (g * LOG2E).T
    mx2 = mx * LOG2E - np.log2(scale)
    m_last = mx[L - 1:L, :]
    wk = jnp.exp(g - m_last) * scale
    sc = jnp.exp(m_prev - m_last)
    m_scr[...] = cumf[L - 1:L, :] + m_last
    yield

    ones_bf = jnp.ones((L, dh), BF16)
    heads = range(LSTM_HEADS)
    hsl = [slice(h * dh, (h + 1) * dh) for h in heads]
    vaug = [jnp.concatenate([v_ref[:, hsl[h]], ones_bf], axis=1) for h in heads]
    r2 = [_dot(q_ref[:, hsl[h]], st_scr[h].astype(BF16)) for h in heads]
    upd = [_dot_tn((k_ref[:, hsl[h]].astype(F32) * wk[:, h:h + 1]).astype(BF16), vaug[h]) for h in heads]
    for h in heads:
        st_scr[h] = st_scr[h] * sc[:, h:h + 1] + upd[h]
    yield
    r1 = []
    for h0 in range(0, LSTM_HEADS, LSTM_HEAD_GROUP):
        group = range(h0, h0 + LSTM_HEAD_GROUP)
        s_qk = {h: _dot_nt(q_ref[:, hsl[h]], k_ref[:, hsl[h]]) for h in group}
        yield
        p = {h: (s_qk[h] * jnp.exp2(jnp.where(causal, g2_t[h:h + 1, :] - mx2[:, h:h + 1], NEG_INF))).astype(BF16)
             for h in group}
        yield
        r1 += [_dot(p[h], vaug[h]) for h in group]
        yield
    hh = []
    for h in heads:
        wcol = w_inter[:, h:h + 1]
        num = r1[h][:, :dh] + r2[h][:, :dh] * wcol
        den = r1[h][:, dh:] + r2[h][:, dh:] * wcol
        hh.append(num / jnp.maximum(jnp.abs(den), enm[:, h:h + 1]))
    inv = [lax.rsqrt(jnp.mean(hh[h] * hh[h], axis=-1, keepdims=True) + RMS_EPS) for h in heads]
    for h in heads:
        oo = o_ref[:, hsl[h]].astype(F32)
        y_ref[:, hsl[h]] = (_sigmoid(oo) * (hh[h] * inv[h] * nw_ref[:, hsl[h]])).astype(BF16)


N_SSD_IN, N_LSTM_IN = 10, 9


def _mixer_kernel(*refs):
    ssd_in = refs[:N_SSD_IN]
    lstm_in = refs[N_SSD_IN:N_SSD_IN + N_LSTM_IN]
    y_ssd, y_lstm, u_scr, st_ssd, st_lstm, m_scr = refs[N_SSD_IN + N_LSTM_IN:]

    @pl.when(pl.program_id(1) == 0)
    def _():
        u_scr[...] = jnp.zeros_like(u_scr)
        st_ssd[...] = jnp.zeros_like(st_ssd)
        st_lstm[...] = jnp.zeros_like(st_lstm)
        m_scr[...] = jnp.zeros_like(m_scr)

    streams = [_ssd_phases(*ssd_in, y_ssd, u_scr, st_ssd), _mlstm_phases(*lstm_in, y_lstm, st_lstm, m_scr)]
    while streams:
        for stream in list(streams):
            if next(stream, StopIteration) is StopIteration:
                streams.remove(stream)


def _mixers(xbc, z, gates, convw, convb, dtb, aneg, dskip_x, nw_ssd, sel, q, k, v, o, ib, fb, nw_lstm, b, s):
    L = min(MIX_CHUNK, s)
    nc = s // L
    tok = lambda bi, ci: (bi * nc + ci, 0)
    const = lambda bi, ci: (0, 0)
    big = pl.BlockSpec((L, LSTM_WIDTH), tok)
    gate_block = lambda j: pl.BlockSpec((L, LANES), lambda bi, ci: (bi * nc + ci, j))
    vec = lambda w: pl.BlockSpec((1, w), const)
    in_specs = [pl.BlockSpec((L, CONV_DIM), tok), pl.BlockSpec((L, SSM_WIDTH), tok), gate_block(0),
                pl.BlockSpec((CONV_WIDTH, CONV_DIM), const), vec(CONV_DIM), vec(LANES), vec(LANES),
                vec(SSM_WIDTH), vec(SSM_WIDTH), pl.BlockSpec((LANES, SSM_WIDTH), const),
                big, big, big, big, gate_block(1), gate_block(2), vec(LANES), vec(LANES), vec(LSTM_WIDTH)]
    assert len(in_specs) == N_SSD_IN + N_LSTM_IN
    return pl.pallas_call(
        _mixer_kernel,
        grid=(b, nc),
        in_specs=in_specs,
        out_specs=[pl.BlockSpec((L, SSM_WIDTH), tok), big],
        out_shape=[jax.ShapeDtypeStruct((b * s, SSM_WIDTH), BF16), jax.ShapeDtypeStruct((b * s, LSTM_WIDTH), BF16)],
        scratch_shapes=[pltpu.VMEM((SUBLANES, CONV_DIM), F32),
                        pltpu.VMEM((SSM_GROUPS, SSM_STATE, SSM_WIDTH // SSM_GROUPS), F32),
                        pltpu.VMEM((LSTM_HEADS, LSTM_HEAD_DIM, 2 * LSTM_HEAD_DIM), F32),
                        pltpu.VMEM((1, LANES), F32)],
        compiler_params=_cparams(("arbitrary", "arbitrary")),
        name="mixers",
    )(xbc, z, gates, convw, convb, dtb, aneg, dskip_x, nw_ssd, sel, q, k, v, o, gates, gates, ib, fb, nw_lstm)


META_IDX, META_GATE, META_RANK = 0, TOP_K, 2 * TOP_K


def _outproj_kernel(ys_ref, yl_ref, x_ref, wo_ref, nw_ref, wr_ref, br_ref, h_ref, hn_ref, meta_ref,
                    cnt_ref, cnt_scr, logit_scr):
    tm = x_ref.shape[0]
    step = pl.program_id(0)

    @pl.when(step == 0)
    def _():
        cnt_scr[...] = jnp.zeros_like(cnt_scr)
        logit_scr[...] = jnp.zeros_like(logit_scr)

    vals = logit_scr[...]
    routed = (step > 0).astype(F32)

    h = (x_ref[...] + _dot(ys_ref[...], wo_ref[:SSM_WIDTH, :]) + _dot(yl_ref[...], wo_ref[SSM_WIDTH:, :]))
    h_ref[...] = h
    hn = _rms(h, nw_ref[...])
    _store_row_tiles(hn_ref, hn)
    logit_scr[...] = _dot(hn.astype(BF16), wr_ref[...]) + br_ref[...]

    lane = lax.broadcasted_iota(jnp.int32, (tm, LANES), 1)
    member = jnp.zeros((tm, LANES), F32)
    tops, idxs, sels = [], [], []
    for _ in range(TOP_K):
        m = jnp.max(vals, axis=-1, keepdims=True)
        idx = jnp.min(jnp.where(vals == m, lane, LANES), axis=-1, keepdims=True)
        sel = lane == idx
        vals = jnp.where(sel, NEG_INF, vals)
        member = member + sel.astype(F32)
        tops.append(m)
        idxs.append(idx)
        sels.append(sel)
    es = [jnp.exp(t - tops[0]) for t in tops]
    inv = 1.0 / (es[0] + es[1] + es[2] + es[3])
    member = member * routed

    r = lax.broadcasted_iota(jnp.int32, (tm, tm), 0)
    c = lax.broadcasted_iota(jnp.int32, (tm, tm), 1)
    strict = (r > c).astype(F32).astype(BF16)
    carry = cnt_scr[0:1, :]
    rank_all = _dot(strict, member.astype(BF16)) + carry
    total = carry + jnp.sum(member, axis=0, keepdims=True)
    cnt_scr[...] = jnp.broadcast_to(total, cnt_scr.shape)
    cnt_ref[...] = jnp.broadcast_to(total, cnt_ref.shape)

    meta = jnp.zeros((tm, LANES), F32)
    for kk in range(TOP_K):
        rank = jnp.sum(jnp.where(sels[kk], rank_all, 0.0), axis=-1, keepdims=True)
        meta = jnp.where(lane == META_IDX + kk, idxs[kk].astype(F32), meta)
        meta = jnp.where(lane == META_GATE + kk, es[kk] * inv, meta)
        meta = jnp.where(lane == META_RANK + kk, rank, meta)
    meta_ref[...] = meta


def _outproj(ys, yl, x2, wo, nw, wr, br):
    t = x2.shape[0]
    tm = min(ROW_TILE, t)
    n = t // tm
    projected = lambda i: (jnp.minimum(i, n - 1), 0)
    routed = lambda i: (jnp.maximum(i - 1, 0), 0)
    tokspec = lambda w: pl.BlockSpec((tm, w), projected)
    const = lambda i: (0, 0)
    return pl.pallas_call(
        _outproj_kernel,
        grid=(n + 1,),
        in_specs=[tokspec(SSM_WIDTH), tokspec(LSTM_WIDTH), tokspec(D_MODEL),
                  pl.BlockSpec((SSM_WIDTH + LSTM_WIDTH, D_MODEL), const, pipeline_mode=pl.Buffered(1)),
                  pl.BlockSpec((1, D_MODEL), const),
                  pl.BlockSpec((D_MODEL, LANES), const),
                  pl.BlockSpec((1, LANES), const)],
        out_specs=[tokspec(D_MODEL), pl.BlockSpec((tm * ROW_SUBTILES, LANES), projected),
                   pl.BlockSpec((tm, LANES), routed), pl.BlockSpec((SUBLANES, LANES), const)],
        out_shape=[jax.ShapeDtypeStruct((t, D_MODEL), F32), jax.ShapeDtypeStruct((t * ROW_SUBTILES, LANES), F32),
                   jax.ShapeDtypeStruct((t, LANES), F32), jax.ShapeDtypeStruct((SUBLANES, LANES), F32)],
        scratch_shapes=[pltpu.VMEM((SUBLANES, LANES), F32), pltpu.VMEM((tm, LANES), F32)],
        compiler_params=_cparams(("arbitrary",)),
        name="outproj_router",
    )(ys, yl, x2, wo, nw, wr, br)


DISPATCH_TILE = 1024
DISPATCH_GROUP = 4
PAD_CHUNKS = (128, 64, 32, 16, 8, 4, 2, 1)


def _dispatch_kernel(padstart_ref, padlen_ref, misc_ref, pos_ref, hn_ref, xs_hbm, sem, zsem, zeros_scr):
    i = pl.program_id(0)
    td = pos_ref.shape[-1] // TOP_K

    @pl.when(i == 0)
    def _():
        zeros_scr[...] = jnp.zeros_like(zeros_scr)
        for e in range(N_EXPERTS):
            row = padstart_ref[e]
            nrow = padlen_ref[e]
            for chunk in PAD_CHUNKS:
                @pl.when((nrow & chunk) != 0)
                def _(row=row, chunk=chunk):
                    pltpu.make_async_copy(
                        zeros_scr.at[pl.ds(0, chunk * ROW_SUBTILES), :],
                        xs_hbm.at[pl.ds(pl.multiple_of(row * ROW_SUBTILES, ROW_SUBTILES), chunk * ROW_SUBTILES), :],
                        zsem).start()
                row = row + (nrow & chunk)

        def tail_copy(blk):
            return pltpu.make_async_copy(
                zeros_scr,
                xs_hbm.at[pl.ds(pl.multiple_of(blk * (MOE_BLOCK * ROW_SUBTILES), ROW_SUBTILES),
                                MOE_BLOCK * ROW_SUBTILES), :], zsem)

        n_used, n_blocks = misc_ref[1], misc_ref[2]

        def start_tail(blk, carry):
            tail_copy(blk).start()
            return carry
        lax.fori_loop(n_used, n_blocks, start_tail, 0)

        def drain_row(j, carry):
            _wait_row_copies(zeros_scr, xs_hbm, 1, zsem)
            return carry
        lax.fori_loop(0, misc_ref[0], drain_row, 0)

        def drain_tail(blk, carry):
            tail_copy(blk).wait()
            return carry
        lax.fori_loop(n_used, n_blocks, drain_tail, 0)

    def body(g, carry):
        r0 = g * DISPATCH_GROUP
        slots = [pos_ref[0, 0, r0 * TOP_K + j] for j in range(DISPATCH_GROUP * TOP_K)]
        for j, slot in enumerate(slots):
            _row_copy(hn_ref, r0 + j // TOP_K, xs_hbm, slot, sem).start(priority=j % DMA_PRIORITIES)
        return carry
    lax.fori_loop(0, td // DISPATCH_GROUP, body, 0)

    for _ in range(TOP_K):
        _wait_row_copies(hn_ref, xs_hbm, td, sem)


def _dispatch(pad_start, pad_len, misc, pos_flat, hn_rt, n_rows):
    t = pos_flat.shape[0] // TOP_K
    td = min(DISPATCH_TILE, t)
    n = t // td
    grid_spec = pltpu.PrefetchScalarGridSpec(
        num_scalar_prefetch=3,
        grid=(n,),
        in_specs=[pl.BlockSpec((1, 1, TOP_K * td), lambda i, *_: (i, 0, 0), memory_space=pltpu.SMEM),
                  pl.BlockSpec((td * ROW_SUBTILES, LANES), lambda i, *_: (i, 0))],
        out_specs=pl.BlockSpec(memory_space=pl.ANY),
        scratch_shapes=[pltpu.SemaphoreType.DMA(()), pltpu.SemaphoreType.DMA(()),
                        pltpu.VMEM((MOE_BLOCK * ROW_SUBTILES, LANES), F32)],
    )
    return pl.pallas_call(
        _dispatch_kernel,
        grid_spec=grid_spec,
        out_shape=jax.ShapeDtypeStruct((n_rows * ROW_SUBTILES, LANES), F32),
        compiler_params=_cparams(("arbitrary",)),
        name="dispatch",
    )(pad_start, pad_len, misc, pos_flat.reshape(n, 1, TOP_K * td), hn_rt)


def _experts_kernel(be_ref, nused_ref, next_ref, x_ref, wgu_hbm, bgu_ref, wdn_hbm, bdn_ref, out_ref,
                    wgu_bf, wdn_bf, wgu_st, wdn_st, wsem, slot_ref):
    i = pl.program_id(0)
    n_used = nused_ref[0]

    def weight_copies(e, slot):
        return (pltpu.make_async_copy(wgu_hbm.at[e], wgu_st.at[slot], wsem.at[0, slot]),
                pltpu.make_async_copy(wdn_hbm.at[e], wdn_st.at[slot], wsem.at[1, slot]))

    @pl.when(i == 0)
    def _():
        slot_ref[0] = 0
        for cp in weight_copies(be_ref[0], 0):
            cp.start()

    @pl.when(i < n_used)
    def _():
        @pl.when(jnp.logical_or(i == 0, be_ref[i] != be_ref[jnp.maximum(i - 1, 0)]))
        def _():
            e = be_ref[i]
            slot = slot_ref[0]
            for cp in weight_copies(e, slot):
                cp.wait()
            wgu_bf[...] = wgu_st[slot].astype(BF16)
            wdn_bf[...] = wdn_st[slot].astype(BF16)
            nxt = next_ref[e]

            @pl.when(nxt >= 0)
            def _():
                for cp in weight_copies(nxt, 1 - slot):
                    cp.start()
            slot_ref[0] = 1 - slot

        xb = jnp.concatenate(_load_row_tile_cols(x_ref, MOE_BLOCK), axis=1).astype(BF16)
        hh = _dot(xb, wgu_bf[...]) + bgu_ref[...]
        gh = jnp.minimum(hh[:, :D_FF], SWIGLU_LIMIT)
        uh = jnp.clip(hh[:, D_FF:], -SWIGLU_LIMIT, SWIGLU_LIMIT)
        act = (uh + 1.0) * (gh * _sigmoid(SWIGLU_ALPHA * gh))
        _store_row_tiles(out_ref, _dot(act.astype(BF16), wdn_bf[...]) + bdn_ref[...])

    @pl.when(i >= n_used)
    def _():
        out_ref[...] = jnp.zeros_like(out_ref)


def _experts(block_expert, n_used, next_expert, xs_rt, wgu, bgu3, wdn, bdn3):
    n_blocks = block_expert.shape[0]
    blk = MOE_BLOCK * ROW_SUBTILES
    grid_spec = pltpu.PrefetchScalarGridSpec(
        num_scalar_prefetch=3,
        grid=(n_blocks,),
        in_specs=[
            pl.BlockSpec((blk, LANES), lambda i, be, nu, nx: (i, 0)),
            pl.BlockSpec(memory_space=pl.ANY),
            pl.BlockSpec((None, 1, 2 * D_FF), lambda i, be, nu, nx: (be[i], 0, 0)),
            pl.BlockSpec(memory_space=pl.ANY),
            pl.BlockSpec((None, 1, D_MODEL), lambda i, be, nu, nx: (be[i], 0, 0)),
        ],
        out_specs=pl.BlockSpec((blk, LANES), lambda i, be, nu, nx: (i, 0)),
        scratch_shapes=[pltpu.VMEM((D_MODEL, 2 * D_FF), BF16),
                        pltpu.VMEM((D_FF, D_MODEL), BF16),
                        pltpu.VMEM((2, D_MODEL, 2 * D_FF), F32),
                        pltpu.VMEM((2, D_FF, D_MODEL), F32),
                        pltpu.SemaphoreType.DMA((2, 2)),
                        pltpu.SMEM((1,), jnp.int32)],
    )
    return pl.pallas_call(
        _experts_kernel,
        grid_spec=grid_spec,
        out_shape=jax.ShapeDtypeStruct((n_blocks * blk, LANES), F32),
        compiler_params=_cparams(("arbitrary",)),
        name="experts",
    )(block_expert, n_used, next_expert, xs_rt, wgu, bgu3, wdn, bdn3)


def _combine_kernel(pos_cur, pos_next, outs_hbm, h_ref, meta_ref, nw_ref, y_ref, buf, sem):
    i = pl.program_id(0)
    n = pl.num_programs(0)
    tc = h_ref.shape[0]
    slot = i % 2

    def issue(pos_ref, s):
        def body(r, carry):
            for kk in range(TOP_K):
                _row_copy(outs_hbm, pos_ref[0, 0, r * TOP_K + kk], buf.at[s, kk], r,
                          sem.at[s]).start(priority=kk % DMA_PRIORITIES)
            return carry
        lax.fori_loop(0, tc, body, 0, unroll=4)

    @pl.when(i == 0)
    def _():
        issue(pos_cur, 0)

    @pl.when(i + 1 < n)
    def _():
        issue(pos_next, 1 - slot)

    for kk in range(TOP_K):
        _wait_row_copies(outs_hbm, buf.at[slot, kk], tc, sem.at[slot])
    meta = meta_ref[...]
    gates = [meta[:, META_GATE + kk:META_GATE + kk + 1] for kk in range(TOP_K)]
    rows = [_load_row_tile_cols(buf.at[slot, kk], tc) for kk in range(TOP_K)]
    cols = []
    for s in range(ROW_SUBTILES):
        acc = h_ref[:, s * LANES:(s + 1) * LANES]
        for kk in range(TOP_K):
            acc = acc + gates[kk] * rows[kk][s]
        cols.append(acc)
    y_ref[...] = _rms(jnp.concatenate(cols, axis=1), nw_ref[...])


def _combine(pos_flat, outs_rt, h, meta, nw):
    t = h.shape[0]
    tc = min(COMBINE_TILE, t)
    n = t // tc
    pos3 = pos_flat.reshape(n, 1, TOP_K * tc)
    return pl.pallas_call(
        _combine_kernel,
        grid=(n,),
        in_specs=[
            pl.BlockSpec((1, 1, TOP_K * tc), lambda i: (i, 0, 0), memory_space=pltpu.SMEM),
            pl.BlockSpec((1, 1, TOP_K * tc), lambda i: (jnp.minimum(i + 1, n - 1), 0, 0),
                         memory_space=pltpu.SMEM),
            pl.BlockSpec(memory_space=pl.ANY),
            pl.BlockSpec((tc, D_MODEL), lambda i: (i, 0)),
            pl.BlockSpec((tc, LANES), lambda i: (i, 0)),
            pl.BlockSpec((1, D_MODEL), lambda i: (0, 0)),
        ],
        out_specs=pl.BlockSpec((tc, D_MODEL), lambda i: (i, 0)),
        out_shape=jax.ShapeDtypeStruct((t, D_MODEL), F32),
        scratch_shapes=[pltpu.VMEM((2, TOP_K, tc * ROW_SUBTILES, LANES), F32), pltpu.SemaphoreType.DMA((2,))],
        compiler_params=_cparams(("arbitrary",)),
        name="combine",
    )(pos3, pos3, outs_rt, h, meta, nw)


def _pad_lanes(v, fill=0.0):
    v = v.astype(F32).reshape(1, -1)
    return jnp.pad(v, ((0, 0), (0, LANES - v.shape[1])), constant_values=fill)


def kernel(x, norm_mix_w, w_in, conv_w, conv_b, dt_bias, a_log, d_skip, ssm_norm_w, lstm_i_bias,
           lstm_f_bias, lstm_norm_w, w_out, norm_ffn_w, w_router, b_router, w_gate_up, b_gate_up,
           w_down, b_down, norm_final_w):
    b, s, d = x.shape
    t = b * s
    x2 = x.reshape(t, d).astype(F32)
    depth = w_in.shape[0]
    assert depth == 1, "the combine kernel fuses the final norm, so exactly one layer is supported"
    for layer in range(depth):
        wi = w_in[layer]
        c0 = SSM_WIDTH
        c1 = c0 + CONV_DIM
        c2 = c1 + SSM_HEADS
        c3 = c2 + 4 * LSTM_WIDTH
        c4 = c3 + LSTM_HEADS
        padc = lambda w: jnp.pad(w, ((0, 0), (0, LANES - w.shape[1])))
        w_all = jnp.concatenate([wi[:, :c1], wi[:, c2:c3], padc(wi[:, c1:c2]), padc(wi[:, c3:c4]),
                                 padc(wi[:, c4:])], axis=1).astype(BF16)
        sel = (jnp.arange(LANES)[:, None] == (jnp.arange(SSM_WIDTH) // SSM_HEAD_DIM)[None, :]).astype(BF16)
        a_neg = _pad_lanes(-jnp.exp(a_log[layer].astype(F32)))
        dskip_x = jnp.repeat(d_skip[layer].astype(F32), SSM_HEAD_DIM).reshape(1, SSM_WIDTH)
        wr = jnp.pad(w_router[layer], ((0, 0), (0, LANES - N_EXPERTS))).astype(BF16)
        br = _pad_lanes(b_router[layer], fill=NEG_INF)

        z, xbc, q, k, v, o, gates = _inproj(x2, norm_mix_w[layer].reshape(1, d).astype(F32), w_all)
        y_ssd, y_lstm = _mixers(
            xbc, z, gates, conv_w[layer].astype(F32), conv_b[layer].reshape(1, -1).astype(F32),
            _pad_lanes(dt_bias[layer]), a_neg, dskip_x, ssm_norm_w[layer].reshape(1, -1).astype(F32), sel,
            q, k, v, o, _pad_lanes(lstm_i_bias[layer]), _pad_lanes(lstm_f_bias[layer]),
            lstm_norm_w[layer].reshape(1, -1).astype(F32), b, s)
        h, hn_rt, meta, cnt = _outproj(y_ssd, y_lstm, x2, w_out[layer].astype(BF16),
                                       norm_ffn_w[layer].reshape(1, d).astype(F32), wr, br)

        idx = meta[:, META_IDX:META_IDX + TOP_K].astype(jnp.int32)
        rank = meta[:, META_RANK:META_RANK + TOP_K].astype(jnp.int32)
        counts = cnt[0, :N_EXPERTS].astype(jnp.int32)
        n_blocks = -(-(t * TOP_K) // MOE_BLOCK) + N_EXPERTS
        padded = (counts + MOE_BLOCK - 1) // MOE_BLOCK * MOE_BLOCK
        padded_ends = jnp.cumsum(padded)
        padded_starts = padded_ends - padded
        onehot = idx[..., None] == jnp.arange(N_EXPERTS, dtype=jnp.int32)
        pos = (jnp.sum(jnp.where(onehot, padded_starts, 0), axis=-1) + rank).reshape(-1)
        block_start = jnp.arange(n_blocks, dtype=jnp.int32) * MOE_BLOCK
        block_expert = jnp.minimum(
            jnp.sum((padded_ends[None, :] <= block_start[:, None]).astype(jnp.int32), axis=1), N_EXPERTS - 1)
        n_used = padded_ends[-1:] // MOE_BLOCK
        pad_len = padded - counts
        misc = jnp.concatenate([jnp.sum(pad_len, keepdims=True), n_used,
                                jnp.full((1,), n_blocks, jnp.int32)])

        xs_rt = _dispatch(padded_starts + counts, pad_len, misc, pos, hn_rt, n_blocks * MOE_BLOCK)
        eids = jnp.arange(N_EXPERTS, dtype=jnp.int32)
        later_nonempty = (eids[None, :] > eids[:, None]) & (counts[None, :] > 0)
        next_expert = jnp.min(jnp.where(later_nonempty, eids[None, :], N_EXPERTS), axis=1)
        next_expert = jnp.where(next_expert < N_EXPERTS, next_expert, -1)
        outs_rt = _experts(block_expert, n_used, next_expert, xs_rt, w_gate_up[layer],
                           b_gate_up[layer].reshape(N_EXPERTS, 1, -1), w_down[layer],
                           b_down[layer].reshape(N_EXPERTS, 1, -1))
        x2 = _combine(pos, outs_rt, h, meta, norm_final_w.reshape(1, d).astype(F32))
    return x2.reshape(b, s, d).astype(x.dtype)
```

```python
import functools

import jax
import jax.numpy as jnp
import numpy as np
from jax import lax
from jax.experimental import pallas as pl
from jax.experimental.pallas import tpu as pltpu

F32 = jnp.float32
BF16 = jnp.bfloat16

D_MODEL = 1024
SSM_WIDTH = 1024
SSM_HEAD_DIM = 64
SSM_HEADS = 16
SSM_GROUPS = 2
SSM_STATE = 128
CONV_WIDTH = 4
CONV_DIM = SSM_WIDTH + 2 * SSM_GROUPS * SSM_STATE
LSTM_WIDTH = 1024
LSTM_HEAD_DIM = 128
LSTM_HEADS = 8
N_EXPERTS = 32
TOP_K = 4
D_FF = 1024
SWIGLU_LIMIT = 7.0
SWIGLU_ALPHA = 1.702
MOE_BLOCK = 256
RMS_EPS = 1e-6

LANES = 128
SUBLANES = 8
MIX_CHUNK = 256
ROW_TILE = 512
COMBINE_TILE = 512
VMEM_LIMIT = 56 * 1024 * 1024

GATE_COLS = LANES
GATE_LIVE = SSM_HEADS + 2 * LSTM_HEADS
NEG_INF = float("-inf")


def _cparams(sem):
    return pltpu.CompilerParams(dimension_semantics=sem, vmem_limit_bytes=VMEM_LIMIT)


def _rms(x, w):
    return x * lax.rsqrt(jnp.mean(x * x, axis=-1, keepdims=True) + RMS_EPS) * w


LOG2E = 1.4426950408889634


def _sigmoid(x):
    return 1.0 / (1.0 + jnp.exp2(x * -LOG2E))


def _softplus(x):
    return jnp.maximum(x, 0.0) + jnp.log(1.0 + jnp.exp(-jnp.abs(x)))


def _split3(a):
    hi = a.astype(BF16)
    r = a - hi.astype(F32)
    mid = r.astype(BF16)
    lo = (r - mid.astype(F32)).astype(BF16)
    return hi, mid, lo


def _dot(a, b):
    return jnp.dot(a, b, preferred_element_type=F32)


def _dot_nt(a, b):
    return lax.dot_general(a, b, (((1,), (1,)), ((), ())), preferred_element_type=F32)


def _dot_tn(a, b):
    return lax.dot_general(a, b, (((0,), (0,)), ((), ())), preferred_element_type=F32)


def _sel_dot(sel_bf, a):
    hi, mid, lo = _split3(a)
    return _dot(sel_bf, hi) + _dot(sel_bf, mid) + _dot(sel_bf, lo)


def _expand(a, sel_bf, terms=3):
    out = None
    for piece in _split3(a)[:terms]:
        d = _dot(piece, sel_bf)
        out = d if out is None else out + d
    return out


ROW_SUBTILES = D_MODEL // LANES
DMA_PRIORITIES = 2


def _store_row_tiles(ref, x):
    n = x.shape[0]
    for s in range(ROW_SUBTILES):
        ref[pl.ds(s, n, stride=ROW_SUBTILES), :] = x[:, s * LANES:(s + 1) * LANES]


def _load_row_tile_cols(ref, n):
    return [ref[pl.ds(s, n, stride=ROW_SUBTILES), :] for s in range(ROW_SUBTILES)]


def _row_copy(src, src_row, dst, dst_row, sem):
    return pltpu.make_async_copy(
        src.at[pl.ds(pl.multiple_of(src_row * ROW_SUBTILES, ROW_SUBTILES), ROW_SUBTILES), :],
        dst.at[pl.ds(pl.multiple_of(dst_row * ROW_SUBTILES, ROW_SUBTILES), ROW_SUBTILES), :], sem)


def _wait_row_copies(src, dst, n_rows, sem):
    size = n_rows * ROW_SUBTILES
    pltpu.make_async_copy(src.at[pl.ds(0, size), :], dst.at[pl.ds(0, size), :], sem).wait()


_INPROJ_WIDTHS = (SSM_WIDTH, CONV_DIM, LSTM_WIDTH, LSTM_WIDTH, LSTM_WIDTH, LSTM_WIDTH, GATE_COLS)
_C_XBC_END = SSM_WIDTH + CONV_DIM
_C_DT_END = _C_XBC_END + SSM_HEADS
_C_O_END = _C_DT_END + 4 * LSTM_WIDTH
_C_I_END = _C_O_END + LSTM_HEADS
IN_PROJ_DIM = _C_I_END + LSTM_HEADS
REGROUP_ROWS = 128


def _regroup_kernel(w_ref, o_ref):
    rows = w_ref.shape[0]
    big = _C_XBC_END + 4 * LSTM_WIDTH
    o_ref[:, :_C_XBC_END] = w_ref[:, :_C_XBC_END].astype(BF16)
    o_ref[:, _C_XBC_END:big] = w_ref[:, _C_DT_END:_C_O_END].astype(BF16)
    gate_w = jnp.concatenate([w_ref[:, _C_XBC_END:_C_DT_END], w_ref[:, _C_O_END:IN_PROJ_DIM],
                              jnp.zeros((rows, GATE_COLS - GATE_LIVE), F32)], axis=1)
    o_ref[:, big:] = gate_w.astype(BF16)


def _regroup_weights(w_in):
    k = w_in.shape[0]
    ncol = sum(_INPROJ_WIDTHS)
    return pl.pallas_call(
        _regroup_kernel,
        grid=(k // REGROUP_ROWS,),
        in_specs=[pl.BlockSpec((REGROUP_ROWS, IN_PROJ_DIM), lambda i: (i, 0))],
        out_specs=pl.BlockSpec((REGROUP_ROWS, ncol), lambda i: (i, 0)),
        out_shape=jax.ShapeDtypeStruct((k, ncol), BF16),
        compiler_params=_cparams(("arbitrary",)),
        name="regroup_weights",
    )(w_in)


def _inproj_kernel(x_ref, nw_ref, w_ref, z_ref, xbc_ref, q_ref, k_ref, v_ref, o_ref, g_ref):
    xb = _rms(x_ref[...], nw_ref[...]).astype(BF16)
    off = 0
    for ref, width in zip((z_ref, xbc_ref, q_ref, k_ref, v_ref, o_ref, g_ref), _INPROJ_WIDTHS):
        ref[...] = _dot(xb, w_ref[:, off:off + width]).astype(ref.dtype)
        off += width


def _inproj(x2, nw, w_all):
    t = x2.shape[0]
    tm = min(ROW_TILE, t)
    ncol = w_all.shape[1]
    out_shape = [jax.ShapeDtypeStruct((t, w), BF16) for w in _INPROJ_WIDTHS[:-1]]
    out_shape.append(jax.ShapeDtypeStruct((t, GATE_COLS), F32))
    return pl.pallas_call(
        _inproj_kernel,
        grid=(t // tm,),
        in_specs=[
            pl.BlockSpec((tm, D_MODEL), lambda i: (i, 0)),
            pl.BlockSpec((1, D_MODEL), lambda i: (0, 0)),
            pl.BlockSpec((D_MODEL, ncol), lambda i: (0, 0), pipeline_mode=pl.Buffered(1)),
        ],
        out_specs=[pl.BlockSpec((tm, w), lambda i: (i, 0)) for w in _INPROJ_WIDTHS],
        out_shape=out_shape,
        compiler_params=_cparams(("arbitrary",)),
        name="inproj",
    )(x2, nw, w_all)


SSD_PAIR_GROUP = 2
LSTM_HEAD_GROUP = 2


def _ssd_phases(xbc_ref, z_ref, g_ref, convw_ref, convb_ref, dtb_ref, aneg_ref, dskip_ref, nw_ref,
                sel_ref, y_ref, u_scr, st_scr):
    L = xbc_ref.shape[0]
    gw = SSM_WIDTH // SSM_GROUPS

    row = lax.broadcasted_iota(jnp.int32, (L, L), 0)
    col = lax.broadcasted_iota(jnp.int32, (L, L), 1)
    causal = row >= col

    xin_bf = xbc_ref[...]
    xin = xin_bf.astype(F32)
    halo = u_scr[...]
    row8 = lax.broadcasted_iota(jnp.int32, (SUBLANES, 1), 0)
    acc = convb_ref[...] + convw_ref[CONV_WIDTH - 1:CONV_WIDTH, :] * xin
    for back in range(1, CONV_WIDTH):
        shifted = _dot((row - col == back).astype(F32).astype(BF16), xin_bf)
        head = shifted[:SUBLANES] + jnp.where(row8 < back, pltpu.roll(halo, back, axis=0), 0.0)
        shifted = jnp.concatenate([head, shifted[SUBLANES:]], axis=0)
        acc = acc + convw_ref[CONV_WIDTH - 1 - back:CONV_WIDTH - back, :] * shifted
    u_scr[...] = xin[L - SUBLANES:, :]
    xbc = acc * _sigmoid(acc)
    xs = xbc[:, :SSM_WIDTH]
    bm = xbc[:, SSM_WIDTH:SSM_WIDTH + SSM_GROUPS * SSM_STATE].astype(BF16)
    cm = xbc[:, SSM_WIDTH + SSM_GROUPS * SSM_STATE:].astype(BF16)
    yield

    tril_bf = causal.astype(F32).astype(BF16)

    lane = lax.broadcasted_iota(jnp.int32, (1, LANES), 1)
    dt = jnp.where(lane < SSM_HEADS, _softplus(g_ref[...] + dtb_ref[...]), 0.0)
    acs = _sel_dot(tril_bf, dt * aneg_ref[...]) * LOG2E
    acs_t = acs.T
    sel = sel_ref[...]
    acs_last = acs[L - 1:L, :]
    ea_x = _expand(jnp.exp2(acs), sel, terms=2)
    xd = xs * _expand(dt, sel, terms=1)
    xd_bf = xd.astype(BF16)
    xw = (xd * _expand(jnp.exp2(acs_last - acs), sel, terms=1)).astype(BF16)
    yield

    cbs = [_dot_nt(cm[:, g * SSM_STATE:(g + 1) * SSM_STATE], bm[:, g * SSM_STATE:(g + 1) * SSM_STATE])
           for g in range(SSM_GROUPS)]
    yoff = [_dot(cm[:, g * SSM_STATE:(g + 1) * SSM_STATE], st_scr[g].astype(BF16)) for g in range(SSM_GROUPS)]
    upd = [_dot_tn(bm[:, g * SSM_STATE:(g + 1) * SSM_STATE], xw[:, g * gw:(g + 1) * gw])
           for g in range(SSM_GROUPS)]
    for g in range(SSM_GROUPS):
        st_scr[g] = st_scr[g] * ea_x[L - 1:L, g * gw:(g + 1) * gw] + upd[g]
    yield

    pair_lane = lax.broadcasted_iota(jnp.int32, (1, LANES), 1)
    heads_per_group = SSM_HEADS // SSM_GROUPS
    rhs = []
    for j in range(SSM_HEADS // 2):
        xpair = xd_bf[:, j * LANES:(j + 1) * LANES]
        zero = jnp.zeros_like(xpair)
        rhs.append(jnp.concatenate([jnp.where(pair_lane < SSM_HEAD_DIM, xpair, zero),
                                    jnp.where(pair_lane >= SSM_HEAD_DIM, xpair, zero)], axis=0))
    ydiag = []
    for j0 in range(0, SSM_HEADS // 2, SSD_PAIR_GROUP):
        pairs = range(j0, j0 + SSD_PAIR_GROUP)
        ms = {h: (cbs[h // heads_per_group]
                  * jnp.exp2(jnp.where(causal, acs[:, h:h + 1] - acs_t[h:h + 1, :], NEG_INF))).astype(BF16)
              for j in pairs for h in (2 * j, 2 * j + 1)}
        yield
        ydiag += [_dot(jnp.concatenate([ms[2 * j], ms[2 * j + 1]], axis=1), rhs[j]) for j in pairs]
        yield
    y = jnp.concatenate(ydiag, axis=1) + jnp.concatenate(yoff, axis=1) * ea_x + dskip_ref[...] * xs

    zz = z_ref[...].astype(F32)
    y = y * (zz * _sigmoid(zz))
    y_ref[...] = _rms(y, nw_ref[...]).astype(BF16)


def _mlstm_phases(q_ref, k_ref, v_ref, o_ref, g_ref, ib_ref, fb_ref, nw_ref, y_ref,
                  st_scr, m_scr):
    L = q_ref.shape[0]
    dh = LSTM_HEAD_DIM
    scale = dh ** -0.5

    row = lax.broadcasted_iota(jnp.int32, (L, L), 0)
    col = lax.broadcasted_iota(jnp.int32, (L, L), 1)
    causal = row >= col
    tril_bf = causal.astype(F32).astype(BF16)
    lane = lax.broadcasted_iota(jnp.int32, (1, LANES), 1)
    live = lane < LSTM_HEADS

    gates = g_ref[...]
    gi = pltpu.roll(gates, LANES - SSM_HEADS, axis=1)
    gf = pltpu.roll(gates, LANES - SSM_HEADS - LSTM_HEADS, axis=1)
    ii = jnp.where(live, gi + ib_ref[...], 0.0)
    logf = jnp.where(live, -_softplus(-(gf + fb_ref[...])), 0.0)
    cumf = _sel_dot(tril_bf, logf)
    g = ii - cumf
    rid = lax.broadcasted_iota(jnp.int32, (L, LANES), 0)
    cmx = g
    step = 1
    while step < L:
        cmx = jnp.maximum(cmx, jnp.where(rid >= step, pltpu.roll(cmx, step, axis=0), NEG_INF))
        step *= 2
    m_prev = m_scr[...]
    mx = jnp.maximum(m_prev, cmx)
    w_inter = jnp.exp(m_prev - mx)
    enm = jnp.exp(-(cumf + mx))
    g2_t = (g * LOG2E).T
    mx2 = mx * LOG2E - np.log2(scale)
    m_last = mx[L - 1:L, :]
    wk = jnp.exp(g - m_last) * scale
    sc = jnp.exp(m_prev - m_last)
    m_scr[...] = cumf[L - 1:L, :] + m_last
    yield

    ones_bf = jnp.ones((L, dh), BF16)
    heads = range(LSTM_HEADS)
    hsl = [slice(h * dh, (h + 1) * dh) for h in heads]
    vaug = [jnp.concatenate([v_ref[:, hsl[h]], ones_bf], axis=1) for h in heads]
    r2 = [_dot(q_ref[:, hsl[h]], st_scr[h].astype(BF16)) for h in heads]
    upd = [_dot_tn((k_ref[:, hsl[h]].astype(F32) * wk[:, h:h + 1]).astype(BF16), vaug[h]) for h in heads]
    for h in heads:
        st_scr[h] = st_scr[h] * sc[:, h:h + 1] + upd[h]
    yield
    r1 = []
    for h0 in range(0, LSTM_HEADS, LSTM_HEAD_GROUP):
        group = range(h0, h0 + LSTM_HEAD_GROUP)
        s_qk = {h: _dot_nt(q_ref[:, hsl[h]], k_ref[:, hsl[h]]) for h in group}
        yield
        p = {h: (s_qk[h] * jnp.exp2(jnp.where(causal, g2_t[h:h + 1, :] - mx2[:, h:h + 1], NEG_INF))).astype(BF16)
             for h in group}
        yield
        r1 += [_dot(p[h], vaug[h]) for h in group]
        yield
    hh = []
    for h in heads:
        wcol = w_inter[:, h:h + 1]
        num = r1[h][:, :dh] + r2[h][:, :dh] * wcol
        den = r1[h][:, dh:] + r2[h][:, dh:] * wcol
        hh.append(num / jnp.maximum(jnp.abs(den), enm[:, h:h + 1]))
    inv = [lax.rsqrt(jnp.mean(hh[h] * hh[h], axis=-1, keepdims=True) + RMS_EPS) for h in heads]
    for h in heads:
        oo = o_ref[:, hsl[h]].astype(F32)
        y_ref[:, hsl[h]] = (_sigmoid(oo) * (hh[h] * inv[h] * nw_ref[:, hsl[h]])).astype(BF16)


N_SSD_IN, N_LSTM_IN = 10, 8


def _mixer_kernel(*refs):
    ssd_in = refs[:N_SSD_IN]
    lstm_in = refs[N_SSD_IN:N_SSD_IN + N_LSTM_IN]
    y_ssd, y_lstm, u_scr, st_ssd, st_lstm, m_scr = refs[N_SSD_IN + N_LSTM_IN:]

    @pl.when(pl.program_id(1) == 0)
    def _():
        u_scr[...] = jnp.zeros_like(u_scr)
        st_ssd[...] = jnp.zeros_like(st_ssd)
        st_lstm[...] = jnp.zeros_like(st_lstm)
        m_scr[...] = jnp.zeros_like(m_scr)

    streams = [_ssd_phases(*ssd_in, y_ssd, u_scr, st_ssd), _mlstm_phases(*lstm_in, y_lstm, st_lstm, m_scr)]
    while streams:
        for stream in list(streams):
            if next(stream, StopIteration) is StopIteration:
                streams.remove(stream)


def _mixers(xbc, z, gates, convw, convb, dtb, aneg, dskip_x, nw_ssd, sel, q, k, v, o, ib, fb, nw_lstm, b, s):
    L = min(MIX_CHUNK, s)
    nc = s // L
    tok = lambda bi, ci: (bi * nc + ci, 0)
    const = lambda bi, ci: (0, 0)
    big = pl.BlockSpec((L, LSTM_WIDTH), tok)
    gate_block = lambda j: pl.BlockSpec((L, LANES), lambda bi, ci: (bi * nc + ci, j))
    vec = lambda w: pl.BlockSpec((1, w), const)
    in_specs = [pl.BlockSpec((L, CONV_DIM), tok), pl.BlockSpec((L, SSM_WIDTH), tok), gate_block(0),
                pl.BlockSpec((CONV_WIDTH, CONV_DIM), const), vec(CONV_DIM), vec(LANES), vec(LANES),
                vec(SSM_WIDTH), vec(SSM_WIDTH), pl.BlockSpec((LANES, SSM_WIDTH), const),
                big, big, big, big, gate_block(0), vec(LANES), vec(LANES), vec(LSTM_WIDTH)]
    assert len(in_specs) == N_SSD_IN + N_LSTM_IN
    return pl.pallas_call(
        _mixer_kernel,
        grid=(b, nc),
        in_specs=in_specs,
        out_specs=[pl.BlockSpec((L, SSM_WIDTH), tok), big],
        out_shape=[jax.ShapeDtypeStruct((b * s, SSM_WIDTH), BF16), jax.ShapeDtypeStruct((b * s, LSTM_WIDTH), BF16)],
        scratch_shapes=[pltpu.VMEM((SUBLANES, CONV_DIM), F32),
                        pltpu.VMEM((SSM_GROUPS, SSM_STATE, SSM_WIDTH // SSM_GROUPS), F32),
                        pltpu.VMEM((LSTM_HEADS, LSTM_HEAD_DIM, 2 * LSTM_HEAD_DIM), F32),
                        pltpu.VMEM((1, LANES), F32)],
        compiler_params=_cparams(("arbitrary", "arbitrary")),
        name="mixers",
    )(xbc, z, gates, convw, convb, dtb, aneg, dskip_x, nw_ssd, sel, q, k, v, o, gates, ib, fb, nw_lstm)


META_IDX, META_GATE, META_RANK = 0, TOP_K, 2 * TOP_K


def _outproj_kernel(ys_ref, yl_ref, x_ref, wo_ref, nw_ref, wr_ref, br_ref, h_ref, hn_ref, meta_ref,
                    cnt_ref, cnt_scr, logit_scr):
    tm = x_ref.shape[0]
    step = pl.program_id(0)

    @pl.when(step == 0)
    def _():
        cnt_scr[...] = jnp.zeros_like(cnt_scr)
        logit_scr[...] = jnp.zeros_like(logit_scr)

    vals = logit_scr[...]
    routed = (step > 0).astype(F32)

    h = (x_ref[...] + _dot(ys_ref[...], wo_ref[:SSM_WIDTH, :]) + _dot(yl_ref[...], wo_ref[SSM_WIDTH:, :]))
    h_ref[...] = h
    hn = _rms(h, nw_ref[...])
    _store_row_tiles(hn_ref, hn)
    logit_scr[...] = _dot(hn.astype(BF16), wr_ref[...]) + br_ref[...]

    lane = lax.broadcasted_iota(jnp.int32, (tm, LANES), 1)
    member = jnp.zeros((tm, LANES), F32)
    tops, idxs, sels = [], [], []
    for _ in range(TOP_K):
        m = jnp.max(vals, axis=-1, keepdims=True)
        idx = jnp.min(jnp.where(vals == m, lane, LANES), axis=-1, keepdims=True)
        sel = lane == idx
        vals = jnp.where(sel, NEG_INF, vals)
        member = member + sel.astype(F32)
        tops.append(m)
        idxs.append(idx)
        sels.append(sel)
    es = [jnp.exp(t - tops[0]) for t in tops]
    inv = 1.0 / (es[0] + es[1] + es[2] + es[3])
    member = member * routed

    r = lax.broadcasted_iota(jnp.int32, (tm, tm), 0)
    c = lax.broadcasted_iota(jnp.int32, (tm, tm), 1)
    strict = (r > c).astype(F32).astype(BF16)
    carry = cnt_scr[0:1, :]
    rank_all = _dot(strict, member.astype(BF16)) + carry
    total = carry + jnp.sum(member, axis=0, keepdims=True)
    cnt_scr[...] = jnp.broadcast_to(total, cnt_scr.shape)
    cnt_ref[...] = jnp.broadcast_to(total, cnt_ref.shape)

    meta = jnp.zeros((tm, LANES), F32)
    for kk in range(TOP_K):
        rank = jnp.sum(jnp.where(sels[kk], rank_all, 0.0), axis=-1, keepdims=True)
        meta = jnp.where(lane == META_IDX + kk, idxs[kk].astype(F32), meta)
        meta = jnp.where(lane == META_GATE + kk, es[kk] * inv, meta)
        meta = jnp.where(lane == META_RANK + kk, rank, meta)
    meta_ref[...] = meta


def _outproj(ys, yl, x2, wo, nw, wr, br):
    t = x2.shape[0]
    tm = min(ROW_TILE, t)
    n = t // tm
    projected = lambda i: (jnp.minimum(i, n - 1), 0)
    routed = lambda i: (jnp.maximum(i - 1, 0), 0)
    tokspec = lambda w: pl.BlockSpec((tm, w), projected)
    const = lambda i: (0, 0)
    return pl.pallas_call(
        _outproj_kernel,
        grid=(n + 1,),
        in_specs=[tokspec(SSM_WIDTH), tokspec(LSTM_WIDTH), tokspec(D_MODEL),
                  pl.BlockSpec((SSM_WIDTH + LSTM_WIDTH, D_MODEL), const, pipeline_mode=pl.Buffered(1)),
                  pl.BlockSpec((1, D_MODEL), const),
                  pl.BlockSpec((D_MODEL, LANES), const),
                  pl.BlockSpec((1, LANES), const)],
        out_specs=[tokspec(D_MODEL), pl.BlockSpec((tm * ROW_SUBTILES, LANES), projected),
                   pl.BlockSpec((tm, LANES), routed), pl.BlockSpec((SUBLANES, LANES), const)],
        out_shape=[jax.ShapeDtypeStruct((t, D_MODEL), F32), jax.ShapeDtypeStruct((t * ROW_SUBTILES, LANES), F32),
                   jax.ShapeDtypeStruct((t, LANES), F32), jax.ShapeDtypeStruct((SUBLANES, LANES), F32)],
        scratch_shapes=[pltpu.VMEM((SUBLANES, LANES), F32), pltpu.VMEM((tm, LANES), F32)],
        compiler_params=_cparams(("arbitrary",)),
        name="outproj_router",
    )(ys, yl, x2, wo, nw, wr, br)


DISPATCH_TILE = 1024
DISPATCH_GROUP = 4
PAD_CHUNKS = (128, 64, 32, 16, 8, 4, 2, 1)


def _dispatch_kernel(padstart_ref, padlen_ref, misc_ref, pos_ref, hn_ref, xs_hbm, sem, zsem, zeros_scr):
    i = pl.program_id(0)
    td = pos_ref.shape[-1] // TOP_K

    @pl.when(i == 0)
    def _():
        zeros_scr[...] = jnp.zeros_like(zeros_scr)
        for e in range(N_EXPERTS):
            row = padstart_ref[e]
            nrow = padlen_ref[e]
            for chunk in PAD_CHUNKS:
                @pl.when((nrow & chunk) != 0)
                def _(row=row, chunk=chunk):
                    pltpu.make_async_copy(
                        zeros_scr.at[pl.ds(0, chunk * ROW_SUBTILES), :],
                        xs_hbm.at[pl.ds(pl.multiple_of(row * ROW_SUBTILES, ROW_SUBTILES), chunk * ROW_SUBTILES), :],
                        zsem).start()
                row = row + (nrow & chunk)

        def tail_copy(blk):
            return pltpu.make_async_copy(
                zeros_scr,
                xs_hbm.at[pl.ds(pl.multiple_of(blk * (MOE_BLOCK * ROW_SUBTILES), ROW_SUBTILES),
                                MOE_BLOCK * ROW_SUBTILES), :], zsem)

        n_used, n_blocks = misc_ref[1], misc_ref[2]

        def start_tail(blk, carry):
            tail_copy(blk).start()
            return carry
        lax.fori_loop(n_used, n_blocks, start_tail, 0)

        def drain_row(j, carry):
            _wait_row_copies(zeros_scr, xs_hbm, 1, zsem)
            return carry
        lax.fori_loop(0, misc_ref[0], drain_row, 0)

        def drain_tail(blk, carry):
            tail_copy(blk).wait()
            return carry
        lax.fori_loop(n_used, n_blocks, drain_tail, 0)

    def body(g, carry):
        r0 = g * DISPATCH_GROUP
        slots = [pos_ref[0, 0, r0 * TOP_K + j] for j in range(DISPATCH_GROUP * TOP_K)]
        for j, slot in enumerate(slots):
            _row_copy(hn_ref, r0 + j // TOP_K, xs_hbm, slot, sem).start(priority=j % DMA_PRIORITIES)
        return carry
    lax.fori_loop(0, td // DISPATCH_GROUP, body, 0)

    for _ in range(TOP_K):
        _wait_row_copies(hn_ref, xs_hbm, td, sem)


def _dispatch(pad_start, pad_len, misc, pos_flat, hn_rt, n_rows):
    t = pos_flat.shape[0] // TOP_K
    td = min(DISPATCH_TILE, t)
    n = t // td
    grid_spec = pltpu.PrefetchScalarGridSpec(
        num_scalar_prefetch=3,
        grid=(n,),
        in_specs=[pl.BlockSpec((1, 1, TOP_K * td), lambda i, *_: (i, 0, 0), memory_space=pltpu.SMEM),
                  pl.BlockSpec((td * ROW_SUBTILES, LANES), lambda i, *_: (i, 0))],
        out_specs=pl.BlockSpec(memory_space=pl.ANY),
        scratch_shapes=[pltpu.SemaphoreType.DMA(()), pltpu.SemaphoreType.DMA(()),
                        pltpu.VMEM((MOE_BLOCK * ROW_SUBTILES, LANES), F32)],
    )
    return pl.pallas_call(
        _dispatch_kernel,
        grid_spec=grid_spec,
        out_shape=jax.ShapeDtypeStruct((n_rows * ROW_SUBTILES, LANES), F32),
        compiler_params=_cparams(("arbitrary",)),
        name="dispatch",
    )(pad_start, pad_len, misc, pos_flat.reshape(n, 1, TOP_K * td), hn_rt)


def _experts_kernel(be_ref, nused_ref, next_ref, x_ref, wgu_hbm, bgu_ref, wdn_hbm, bdn_ref, out_ref,
                    wgu_bf, wdn_bf, wgu_st, wdn_st, wsem, slot_ref):
    i = pl.program_id(0)
    n_used = nused_ref[0]

    def weight_copies(e, slot):
        return (pltpu.make_async_copy(wgu_hbm.at[e], wgu_st.at[slot], wsem.at[0, slot]),
                pltpu.make_async_copy(wdn_hbm.at[e], wdn_st.at[slot], wsem.at[1, slot]))

    @pl.when(i == 0)
    def _():
        slot_ref[0] = 0
        for cp in weight_copies(be_ref[0], 0):
            cp.start()

    @pl.when(i < n_used)
    def _():
        @pl.when(jnp.logical_or(i == 0, be_ref[i] != be_ref[jnp.maximum(i - 1, 0)]))
        def _():
            e = be_ref[i]
            slot = slot_ref[0]
            for cp in weight_copies(e, slot):
                cp.wait()
            wgu_bf[...] = wgu_st[slot].astype(BF16)
            wdn_bf[...] = wdn_st[slot].astype(BF16)
            nxt = next_ref[e]

            @pl.when(nxt >= 0)
            def _():
                for cp in weight_copies(nxt, 1 - slot):
                    cp.start()
            slot_ref[0] = 1 - slot

        xb = jnp.concatenate(_load_row_tile_cols(x_ref, MOE_BLOCK), axis=1).astype(BF16)
        hh = _dot(xb, wgu_bf[...]) + bgu_ref[...]
        gh = jnp.minimum(hh[:, :D_FF], SWIGLU_LIMIT)
        uh = jnp.clip(hh[:, D_FF:], -SWIGLU_LIMIT, SWIGLU_LIMIT)
        act = (uh + 1.0) * (gh * _sigmoid(SWIGLU_ALPHA * gh))
        _store_row_tiles(out_ref, _dot(act.astype(BF16), wdn_bf[...]) + bdn_ref[...])

    @pl.when(i >= n_used)
    def _():
        out_ref[...] = jnp.zeros_like(out_ref)


def _experts(block_expert, n_used, next_expert, xs_rt, wgu, bgu3, wdn, bdn3):
    n_blocks = block_expert.shape[0]
    blk = MOE_BLOCK * ROW_SUBTILES
    grid_spec = pltpu.PrefetchScalarGridSpec(
        num_scalar_prefetch=3,
        grid=(n_blocks,),
        in_specs=[
            pl.BlockSpec((blk, LANES), lambda i, be, nu, nx: (i, 0)),
            pl.BlockSpec(memory_space=pl.ANY),
            pl.BlockSpec((None, 1, 2 * D_FF), lambda i, be, nu, nx: (be[i], 0, 0)),
            pl.BlockSpec(memory_space=pl.ANY),
            pl.BlockSpec((None, 1, D_MODEL), lambda i, be, nu, nx: (be[i], 0, 0)),
        ],
        out_specs=pl.BlockSpec((blk, LANES), lambda i, be, nu, nx: (i, 0)),
        scratch_shapes=[pltpu.VMEM((D_MODEL, 2 * D_FF), BF16),
                        pltpu.VMEM((D_FF, D_MODEL), BF16),
                        pltpu.VMEM((2, D_MODEL, 2 * D_FF), F32),
                        pltpu.VMEM((2, D_FF, D_MODEL), F32),
                        pltpu.SemaphoreType.DMA((2, 2)),
                        pltpu.SMEM((1,), jnp.int32)],
    )
    return pl.pallas_call(
        _experts_kernel,
        grid_spec=grid_spec,
        out_shape=jax.ShapeDtypeStruct((n_blocks * blk, LANES), F32),
        compiler_params=_cparams(("arbitrary",)),
        name="experts",
    )(block_expert, n_used, next_expert, xs_rt, wgu, bgu3, wdn, bdn3)


def _combine_kernel(pos_cur, pos_next, outs_hbm, h_ref, meta_ref, nw_ref, y_ref, buf, sem):
    i = pl.program_id(0)
    n = pl.num_programs(0)
    tc = h_ref.shape[0]
    slot = i % 2

    def issue(pos_ref, s):
        def body(r, carry):
            for kk in range(TOP_K):
                _row_copy(outs_hbm, pos_ref[0, 0, r * TOP_K + kk], buf.at[s, kk], r,
                          sem.at[s]).start(priority=kk % DMA_PRIORITIES)
            return carry
        lax.fori_loop(0, tc, body, 0, unroll=4)

    @pl.when(i == 0)
    def _():
        issue(pos_cur, 0)

    @pl.when(i + 1 < n)
    def _():
        issue(pos_next, 1 - slot)

    for kk in range(TOP_K):
        _wait_row_copies(outs_hbm, buf.at[slot, kk], tc, sem.at[slot])
    meta = meta_ref[...]
    gates = [meta[:, META_GATE + kk:META_GATE + kk + 1] for kk in range(TOP_K)]
    rows = [_load_row_tile_cols(buf.at[slot, kk], tc) for kk in range(TOP_K)]
    cols = []
    for s in range(ROW_SUBTILES):
        acc = h_ref[:, s * LANES:(s + 1) * LANES]
        for kk in range(TOP_K):
            acc = acc + gates[kk] * rows[kk][s]
        cols.append(acc)
    y_ref[...] = _rms(jnp.concatenate(cols, axis=1), nw_ref[...])


def _combine(pos_flat, outs_rt, h, meta, nw):
    t = h.shape[0]
    tc = min(COMBINE_TILE, t)
    n = t // tc
    pos3 = pos_flat.reshape(n, 1, TOP_K * tc)
    return pl.pallas_call(
        _combine_kernel,
        grid=(n,),
        in_specs=[
            pl.BlockSpec((1, 1, TOP_K * tc), lambda i: (i, 0, 0), memory_space=pltpu.SMEM),
            pl.BlockSpec((1, 1, TOP_K * tc), lambda i: (jnp.minimum(i + 1, n - 1), 0, 0),
                         memory_space=pltpu.SMEM),
            pl.BlockSpec(memory_space=pl.ANY),
            pl.BlockSpec((tc, D_MODEL), lambda i: (i, 0)),
            pl.BlockSpec((tc, LANES), lambda i: (i, 0)),
            pl.BlockSpec((1, D_MODEL), lambda i: (0, 0)),
        ],
        out_specs=pl.BlockSpec((tc, D_MODEL), lambda i: (i, 0)),
        out_shape=jax.ShapeDtypeStruct((t, D_MODEL), F32),
        scratch_shapes=[pltpu.VMEM((2, TOP_K, tc * ROW_SUBTILES, LANES), F32), pltpu.SemaphoreType.DMA((2,))],
        compiler_params=_cparams(("arbitrary",)),
        name="combine",
    )(pos3, pos3, outs_rt, h, meta, nw)


def _pad_lanes(v, fill=0.0):
    v = v.astype(F32).reshape(1, -1)
    return jnp.pad(v, ((0, 0), (0, LANES - v.shape[1])), constant_values=fill)


def kernel(x, norm_mix_w, w_in, conv_w, conv_b, dt_bias, a_log, d_skip, ssm_norm_w, lstm_i_bias,
           lstm_f_bias, lstm_norm_w, w_out, norm_ffn_w, w_router, b_router, w_gate_up, b_gate_up,
           w_down, b_down, norm_final_w):
    b, s, d = x.shape
    t = b * s
    x2 = x.reshape(t, d).astype(F32)
    depth = w_in.shape[0]
    assert depth == 1, "the combine kernel fuses the final norm, so exactly one layer is supported"
    for layer in range(depth):
        w_all = _regroup_weights(w_in[layer].astype(F32))
        sel = (jnp.arange(LANES)[:, None] == (jnp.arange(SSM_WIDTH) // SSM_HEAD_DIM)[None, :]).astype(BF16)
        a_neg = _pad_lanes(-jnp.exp(a_log[layer].astype(F32)))
        dskip_x = jnp.repeat(d_skip[layer].astype(F32), SSM_HEAD_DIM).reshape(1, SSM_WIDTH)
        wr = jnp.pad(w_router[layer], ((0, 0), (0, LANES - N_EXPERTS))).astype(BF16)
        br = _pad_lanes(b_router[layer], fill=NEG_INF)

        z, xbc, q, k, v, o, gates = _inproj(x2, norm_mix_w[layer].reshape(1, d).astype(F32), w_all)
        y_ssd, y_lstm = _mixers(
            xbc, z, gates, conv_w[layer].astype(F32), conv_b[layer].reshape(1, -1).astype(F32),
            _pad_lanes(dt_bias[layer]), a_neg, dskip_x, ssm_norm_w[layer].reshape(1, -1).astype(F32), sel,
            q, k, v, o, _pad_lanes(lstm_i_bias[layer]), _pad_lanes(lstm_f_bias[layer]),
            lstm_norm_w[layer].reshape(1, -1).astype(F32), b, s)
        h, hn_rt, meta, cnt = _outproj(y_ssd, y_lstm, x2, w_out[layer].astype(BF16),
                                       norm_ffn_w[layer].reshape(1, d).astype(F32), wr, br)

        idx = meta[:, META_IDX:META_IDX + TOP_K].astype(jnp.int32)
        rank = meta[:, META_RANK:META_RANK + TOP_K].astype(jnp.int32)
        counts = cnt[0, :N_EXPERTS].astype(jnp.int32)
        n_blocks = -(-(t * TOP_K) // MOE_BLOCK) + N_EXPERTS
        padded = (counts + MOE_BLOCK - 1) // MOE_BLOCK * MOE_BLOCK
        padded_ends = jnp.cumsum(padded)
        padded_starts = padded_ends - padded
        onehot = idx[..., None] == jnp.arange(N_EXPERTS, dtype=jnp.int32)
        pos = (jnp.sum(jnp.where(onehot, padded_starts, 0), axis=-1) + rank).reshape(-1)
        block_start = jnp.arange(n_blocks, dtype=jnp.int32) * MOE_BLOCK
        block_expert = jnp.minimum(
            jnp.sum((padded_ends[None, :] <= block_start[:, None]).astype(jnp.int32), axis=1), N_EXPERTS - 1)
        n_used = padded_ends[-1:] // MOE_BLOCK
        pad_len = padded - counts
        misc = jnp.concatenate([jnp.sum(pad_len, keepdims=True), n_used,
                                jnp.full((1,), n_blocks, jnp.int32)])

        xs_rt = _dispatch(padded_starts + counts, pad_len, misc, pos, hn_rt, n_blocks * MOE_BLOCK)
        eids = jnp.arange(N_EXPERTS, dtype=jnp.int32)
        later_nonempty = (eids[None, :] > eids[:, None]) & (counts[None, :] > 0)
        next_expert = jnp.min(jnp.where(later_nonempty, eids[None, :], N_EXPERTS), axis=1)
        next_expert = jnp.where(next_expert < N_EXPERTS, next_expert, -1)
        outs_rt = _experts(block_expert, n_used, next_expert, xs_rt, w_gate_up[layer],
                           b_gate_up[layer].reshape(N_EXPERTS, 1, -1), w_down[layer],
                           b_down[layer].reshape(N_EXPERTS, 1, -1))
        x2 = _combine(pos, outs_rt, h, meta, norm_final_w.reshape(1, d).astype(F32))
    return x2.reshape(b, s, d).astype(x.dtype)
```

```python
import functools

import jax
import jax.numpy as jnp
import numpy as np
from jax import lax
from jax.experimental import pallas as pl
from jax.experimental.pallas import tpu as pltpu

F32 = jnp.float32
BF16 = jnp.bfloat16

D_MODEL = 1024
SSM_WIDTH = 1024
SSM_HEAD_DIM = 64
SSM_HEADS = 16
SSM_GROUPS = 2
SSM_STATE = 128
CONV_WIDTH = 4
CONV_DIM = SSM_WIDTH + 2 * SSM_GROUPS * SSM_STATE
LSTM_WIDTH = 1024
LSTM_HEAD_DIM = 128
LSTM_HEADS = 8
N_EXPERTS = 32
TOP_K = 4
D_FF = 1024
SWIGLU_LIMIT = 7.0
SWIGLU_ALPHA = 1.702
MOE_BLOCK = 256
RMS_EPS = 1e-6

LANES = 128
SUBLANES = 8
MIX_CHUNK = 256
ROW_TILE = 512
COMBINE_TILE = 512
VMEM_LIMIT = 56 * 1024 * 1024

GATE_COLS = LANES
GATE_LIVE = SSM_HEADS + 2 * LSTM_HEADS
NEG_INF = float("-inf")


def _cparams(sem):
    return pltpu.CompilerParams(dimension_semantics=sem, vmem_limit_bytes=VMEM_LIMIT)


def _rms(x, w):
    return x * lax.rsqrt(jnp.mean(x * x, axis=-1, keepdims=True) + RMS_EPS) * w


LOG2E = 1.4426950408889634


def _sigmoid(x):
    return 1.0 / (1.0 + jnp.exp2(x * -LOG2E))


def _softplus(x):
    return jnp.maximum(x, 0.0) + jnp.log(1.0 + jnp.exp(-jnp.abs(x)))


def _split3(a):
    hi = a.astype(BF16)
    r = a - hi.astype(F32)
    mid = r.astype(BF16)
    lo = (r - mid.astype(F32)).astype(BF16)
    return hi, mid, lo


def _dot(a, b):
    return jnp.dot(a, b, preferred_element_type=F32)


def _dot_nt(a, b):
    return lax.dot_general(a, b, (((1,), (1,)), ((), ())), preferred_element_type=F32)


def _dot_tn(a, b):
    return lax.dot_general(a, b, (((0,), (0,)), ((), ())), preferred_element_type=F32)


def _sel_dot(sel_bf, a):
    hi, mid, lo = _split3(a)
    return _dot(sel_bf, hi) + _dot(sel_bf, mid) + _dot(sel_bf, lo)


def _expand(a, sel_bf, terms=3):
    out = None
    for piece in _split3(a)[:terms]:
        d = _dot(piece, sel_bf)
        out = d if out is None else out + d
    return out


ROW_SUBTILES = D_MODEL // LANES
DMA_PRIORITIES = 2


def _store_row_tiles(ref, x):
    n = x.shape[0]
    for s in range(ROW_SUBTILES):
        ref[pl.ds(s, n, stride=ROW_SUBTILES), :] = x[:, s * LANES:(s + 1) * LANES]


def _load_row_tile_cols(ref, n):
    return [ref[pl.ds(s, n, stride=ROW_SUBTILES), :] for s in range(ROW_SUBTILES)]


def _row_copy(src, src_row, dst, dst_row, sem):
    return pltpu.make_async_copy(
        src.at[pl.ds(pl.multiple_of(src_row * ROW_SUBTILES, ROW_SUBTILES), ROW_SUBTILES), :],
        dst.at[pl.ds(pl.multiple_of(dst_row * ROW_SUBTILES, ROW_SUBTILES), ROW_SUBTILES), :], sem)


def _wait_row_copies(src, dst, n_rows, sem):
    size = n_rows * ROW_SUBTILES
    pltpu.make_async_copy(src.at[pl.ds(0, size), :], dst.at[pl.ds(0, size), :], sem).wait()


_INPROJ_WIDTHS = (SSM_WIDTH, CONV_DIM, LSTM_WIDTH, LSTM_WIDTH, LSTM_WIDTH, LSTM_WIDTH, GATE_COLS)
_C_XBC_END = SSM_WIDTH + CONV_DIM
_C_DT_END = _C_XBC_END + SSM_HEADS
_C_O_END = _C_DT_END + 4 * LSTM_WIDTH
_C_I_END = _C_O_END + LSTM_HEADS
IN_PROJ_DIM = _C_I_END + LSTM_HEADS
REGROUP_ROWS = 128


def _regroup_kernel(w_ref, o_ref):
    rows = w_ref.shape[0]
    big = _C_XBC_END + 4 * LSTM_WIDTH
    o_ref[:, :_C_XBC_END] = w_ref[:, :_C_XBC_END].astype(BF16)
    o_ref[:, _C_XBC_END:big] = w_ref[:, _C_DT_END:_C_O_END].astype(BF16)
    gate_w = jnp.concatenate([w_ref[:, _C_XBC_END:_C_DT_END], w_ref[:, _C_O_END:IN_PROJ_DIM],
                              jnp.zeros((rows, GATE_COLS - GATE_LIVE), F32)], axis=1)
    o_ref[:, big:] = gate_w.astype(BF16)


def _regroup_weights(w_in, layer):
    k = w_in.shape[1]
    ncol = sum(_INPROJ_WIDTHS)
    return pl.pallas_call(
        _regroup_kernel,
        grid=(k // REGROUP_ROWS,),
        in_specs=[pl.BlockSpec((None, REGROUP_ROWS, IN_PROJ_DIM), lambda i: (layer, i, 0))],
        out_specs=pl.BlockSpec((REGROUP_ROWS, ncol), lambda i: (i, 0)),
        out_shape=jax.ShapeDtypeStruct((k, ncol), BF16),
        compiler_params=_cparams(("arbitrary",)),
        name="regroup_weights",
    )(w_in)


def _inproj_kernel(x_ref, nw_ref, w_ref, z_ref, xbc_ref, q_ref, k_ref, v_ref, o_ref, g_ref):
    xb = _rms(x_ref[...], nw_ref[...]).astype(BF16)
    off = 0
    for ref, width in zip((z_ref, xbc_ref, q_ref, k_ref, v_ref, o_ref, g_ref), _INPROJ_WIDTHS):
        ref[...] = _dot(xb, w_ref[:, off:off + width]).astype(ref.dtype)
        off += width


def _inproj(x2, nw, w_all):
    t = x2.shape[0]
    tm = min(ROW_TILE, t)
    ncol = w_all.shape[1]
    out_shape = [jax.ShapeDtypeStruct((t, w), BF16) for w in _INPROJ_WIDTHS[:-1]]
    out_shape.append(jax.ShapeDtypeStruct((t, GATE_COLS), F32))
    return pl.pallas_call(
        _inproj_kernel,
        grid=(t // tm,),
        in_specs=[
            pl.BlockSpec((tm, D_MODEL), lambda i: (i, 0)),
            pl.BlockSpec((1, D_MODEL), lambda i: (0, 0)),
            pl.BlockSpec((D_MODEL, ncol), lambda i: (0, 0), pipeline_mode=pl.Buffered(1)),
        ],
        out_specs=[pl.BlockSpec((tm, w), lambda i: (i, 0)) for w in _INPROJ_WIDTHS],
        out_shape=out_shape,
        compiler_params=_cparams(("arbitrary",)),
        name="inproj",
    )(x2, nw, w_all)


SSD_PAIR_GROUP = 2
LSTM_HEAD_GROUP = 2


def _ssd_phases(xbc_ref, z_ref, g_ref, convw_ref, convb_ref, dtb_ref, aneg_ref, dskip_ref, nw_ref,
                sel_ref, y_ref, u_scr, st_scr):
    L = xbc_ref.shape[0]
    gw = SSM_WIDTH // SSM_GROUPS

    row = lax.broadcasted_iota(jnp.int32, (L, L), 0)
    col = lax.broadcasted_iota(jnp.int32, (L, L), 1)
    causal = row >= col

    xin_bf = xbc_ref[...]
    xin = xin_bf.astype(F32)
    halo = u_scr[...]
    row8 = lax.broadcasted_iota(jnp.int32, (SUBLANES, 1), 0)
    acc = convb_ref[...] + convw_ref[CONV_WIDTH - 1:CONV_WIDTH, :] * xin
    for back in range(1, CONV_WIDTH):
        shifted = _dot((row - col == back).astype(F32).astype(BF16), xin_bf)
        head = shifted[:SUBLANES] + jnp.where(row8 < back, pltpu.roll(halo, back, axis=0), 0.0)
        shifted = jnp.concatenate([head, shifted[SUBLANES:]], axis=0)
        acc = acc + convw_ref[CONV_WIDTH - 1 - back:CONV_WIDTH - back, :] * shifted
    u_scr[...] = xin[L - SUBLANES:, :]
    xbc = acc * _sigmoid(acc)
    xs = xbc[:, :SSM_WIDTH]
    bm = xbc[:, SSM_WIDTH:SSM_WIDTH + SSM_GROUPS * SSM_STATE].astype(BF16)
    cm = xbc[:, SSM_WIDTH + SSM_GROUPS * SSM_STATE:].astype(BF16)
    yield

    tril_bf = causal.astype(F32).astype(BF16)

    lane = lax.broadcasted_iota(jnp.int32, (1, LANES), 1)
    dt = jnp.where(lane < SSM_HEADS, _softplus(g_ref[...] + dtb_ref[...]), 0.0)
    acs = _sel_dot(tril_bf, dt * aneg_ref[...]) * LOG2E
    acs_t = acs.T
    sel = sel_ref[...]
    acs_last = acs[L - 1:L, :]
    ea_x = _expand(jnp.exp2(acs), sel, terms=2)
    xd = xs * _expand(dt, sel, terms=1)
    xd_bf = xd.astype(BF16)
    xw = (xd * _expand(jnp.exp2(acs_last - acs), sel, terms=1)).astype(BF16)
    yield

    cbs = [_dot_nt(cm[:, g * SSM_STATE:(g + 1) * SSM_STATE], bm[:, g * SSM_STATE:(g + 1) * SSM_STATE])
           for g in range(SSM_GROUPS)]
    yoff = [_dot(cm[:, g * SSM_STATE:(g + 1) * SSM_STATE], st_scr[g].astype(BF16)) for g in range(SSM_GROUPS)]
    upd = [_dot_tn(bm[:, g * SSM_STATE:(g + 1) * SSM_STATE], xw[:, g * gw:(g + 1) * gw])
           for g in range(SSM_GROUPS)]
    for g in range(SSM_GROUPS):
        st_scr[g] = st_scr[g] * ea_x[L - 1:L, g * gw:(g + 1) * gw] + upd[g]
    yield

    pair_lane = lax.broadcasted_iota(jnp.int32, (1, LANES), 1)
    heads_per_group = SSM_HEADS // SSM_GROUPS
    rhs = []
    for j in range(SSM_HEADS // 2):
        xpair = xd_bf[:, j * LANES:(j + 1) * LANES]
        zero = jnp.zeros_like(xpair)
        rhs.append(jnp.concatenate([jnp.where(pair_lane < SSM_HEAD_DIM, xpair, zero),
                                    jnp.where(pair_lane >= SSM_HEAD_DIM, xpair, zero)], axis=0))
    ydiag = []
    for j0 in range(0, SSM_HEADS // 2, SSD_PAIR_GROUP):
        pairs = range(j0, j0 + SSD_PAIR_GROUP)
        ms = {h: (cbs[h // heads_per_group]
                  * jnp.exp2(jnp.where(causal, acs[:, h:h + 1] - acs_t[h:h + 1, :], NEG_INF))).astype(BF16)
              for j in pairs for h in (2 * j, 2 * j + 1)}
        yield
        ydiag += [_dot(jnp.concatenate([ms[2 * j], ms[2 * j + 1]], axis=1), rhs[j]) for j in pairs]
        yield
    y = jnp.concatenate(ydiag, axis=1) + jnp.concatenate(yoff, axis=1) * ea_x + dskip_ref[...] * xs

    zz = z_ref[...].astype(F32)
    y = y * (zz * _sigmoid(zz))
    y_ref[...] = _rms(y, nw_ref[...]).astype(BF16)


def _mlstm_phases(q_ref, k_ref, v_ref, o_ref, g_ref, ib_ref, fb_ref, nw_ref, y_ref,
                  st_scr, m_scr):
    L = q_ref.shape[0]
    dh = LSTM_HEAD_DIM
    scale = dh ** -0.5

    row = lax.broadcasted_iota(jnp.int32, (L, L), 0)
    col = lax.broadcasted_iota(jnp.int32, (L, L), 1)
    causal = row >= col
    tril_bf = causal.astype(F32).astype(BF16)
    lane = lax.broadcasted_iota(jnp.int32, (1, LANES), 1)
    live = lane < LSTM_HEADS

    gates = g_ref[...]
    gi = pltpu.roll(gates, LANES - SSM_HEADS, axis=1)
    gf = pltpu.roll(gates, LANES - SSM_HEADS - LSTM_HEADS, axis=1)
    ii = jnp.where(live, gi + ib_ref[...], 0.0)
    logf = jnp.where(live, -_softplus(-(gf + fb_ref[...])), 0.0)
    cumf = _sel_dot(tril_bf, logf)
    g = ii - cumf
    rid = lax.broadcasted_iota(jnp.int32, (L, LANES), 0)
    cmx = g
    step = 1
    while step < L:
        cmx = jnp.maximum(cmx, jnp.where(rid >= step, pltpu.roll(cmx, step, axis=0), NEG_INF))
        step *= 2
    m_prev = m_scr[...]
    mx = jnp.maximum(m_prev, cmx)
    w_inter = jnp.exp(m_prev - mx)
    enm = jnp.exp(-(cumf + mx))
    g2_t = (g * LOG2E).T
    mx2 = mx * LOG2E - np.log2(scale)
    m_last = mx[L - 1:L, :]
    wk = jnp.exp(g - m_last) * scale
    sc = jnp.exp(m_prev - m_last)
    m_scr[...] = cumf[L - 1:L, :] + m_last
    yield

    ones_bf = jnp.ones((L, dh), BF16)
    heads = range(LSTM_HEADS)
    hsl = [slice(h * dh, (h + 1) * dh) for h in heads]
    vaug = [jnp.concatenate([v_ref[:, hsl[h]], ones_bf], axis=1) for h in heads]
    r2 = [_dot(q_ref[:, hsl[h]], st_scr[h].astype(BF16)) for h in heads]
    upd = [_dot_tn((k_ref[:, hsl[h]].astype(F32) * wk[:, h:h + 1]).astype(BF16), vaug[h]) for h in heads]
    for h in heads:
        st_scr[h] = st_scr[h] * sc[:, h:h + 1] + upd[h]
    yield
    r1 = []
    for h0 in range(0, LSTM_HEADS, LSTM_HEAD_GROUP):
        group = range(h0, h0 + LSTM_HEAD_GROUP)
        s_qk = {h: _dot_nt(q_ref[:, hsl[h]], k_ref[:, hsl[h]]) for h in group}
        yield
        p = {h: (s_qk[h] * jnp.exp2(jnp.where(causal, g2_t[h:h + 1, :] - mx2[:, h:h + 1], NEG_INF))).astype(BF16)
             for h in group}
        yield
        r1 += [_dot(p[h], vaug[h]) for h in group]
        yield
    hh = []
    for h in heads:
        wcol = w_inter[:, h:h + 1]
        num = r1[h][:, :dh] + r2[h][:, :dh] * wcol
        den = r1[h][:, dh:] + r2[h][:, dh:] * wcol
        hh.append(num / jnp.maximum(jnp.abs(den), enm[:, h:h + 1]))
    inv = [lax.rsqrt(jnp.mean(hh[h] * hh[h], axis=-1, keepdims=True) + RMS_EPS) for h in heads]
    for h in heads:
        oo = o_ref[:, hsl[h]].astype(F32)
        y_ref[:, hsl[h]] = (_sigmoid(oo) * (hh[h] * inv[h] * nw_ref[:, hsl[h]])).astype(BF16)


N_SSD_IN, N_LSTM_IN = 10, 8


def _mixer_kernel(*refs):
    ssd_in = refs[:N_SSD_IN]
    lstm_in = refs[N_SSD_IN:N_SSD_IN + N_LSTM_IN]
    y_ssd, y_lstm, u_scr, st_ssd, st_lstm, m_scr = refs[N_SSD_IN + N_LSTM_IN:]

    @pl.when(pl.program_id(1) == 0)
    def _():
        u_scr[...] = jnp.zeros_like(u_scr)
        st_ssd[...] = jnp.zeros_like(st_ssd)
        st_lstm[...] = jnp.zeros_like(st_lstm)
        m_scr[...] = jnp.zeros_like(m_scr)

    streams = [_ssd_phases(*ssd_in, y_ssd, u_scr, st_ssd), _mlstm_phases(*lstm_in, y_lstm, st_lstm, m_scr)]
    while streams:
        for stream in list(streams):
            if next(stream, StopIteration) is StopIteration:
                streams.remove(stream)


def _mixers(xbc, z, gates, convw, convb, dtb, aneg, dskip_x, nw_ssd, sel, q, k, v, o, ib, fb, nw_lstm, b, s):
    L = min(MIX_CHUNK, s)
    nc = s // L
    tok = lambda bi, ci: (bi * nc + ci, 0)
    const = lambda bi, ci: (0, 0)
    big = pl.BlockSpec((L, LSTM_WIDTH), tok)
    gate_block = lambda j: pl.BlockSpec((L, LANES), lambda bi, ci: (bi * nc + ci, j))
    vec = lambda w: pl.BlockSpec((1, w), const)
    in_specs = [pl.BlockSpec((L, CONV_DIM), tok), pl.BlockSpec((L, SSM_WIDTH), tok), gate_block(0),
                pl.BlockSpec((CONV_WIDTH, CONV_DIM), const), vec(CONV_DIM), vec(LANES), vec(LANES),
                vec(SSM_WIDTH), vec(SSM_WIDTH), pl.BlockSpec((LANES, SSM_WIDTH), const),
                big, big, big, big, gate_block(0), vec(LANES), vec(LANES), vec(LSTM_WIDTH)]
    assert len(in_specs) == N_SSD_IN + N_LSTM_IN
    return pl.pallas_call(
        _mixer_kernel,
        grid=(b, nc),
        in_specs=in_specs,
        out_specs=[pl.BlockSpec((L, SSM_WIDTH), tok), big],
        out_shape=[jax.ShapeDtypeStruct((b * s, SSM_WIDTH), BF16), jax.ShapeDtypeStruct((b * s, LSTM_WIDTH), BF16)],
        scratch_shapes=[pltpu.VMEM((SUBLANES, CONV_DIM), F32),
                        pltpu.VMEM((SSM_GROUPS, SSM_STATE, SSM_WIDTH // SSM_GROUPS), F32),
                        pltpu.VMEM((LSTM_HEADS, LSTM_HEAD_DIM, 2 * LSTM_HEAD_DIM), F32),
                        pltpu.VMEM((1, LANES), F32)],
        compiler_params=_cparams(("arbitrary", "arbitrary")),
        name="mixers",
    )(xbc, z, gates, convw, convb, dtb, aneg, dskip_x, nw_ssd, sel, q, k, v, o, gates, ib, fb, nw_lstm)


META_IDX, META_GATE, META_RANK = 0, TOP_K, 2 * TOP_K
META_ROWS = 2 * SUBLANES


def _outproj_kernel(ys_ref, yl_ref, x_ref, wo_ref, nw_ref, wr_ref, br_ref, h_ref, hn_ref, meta_ref,
                    meta_t_ref, cnt_ref, cnt_scr, logit_scr):
    tm = x_ref.shape[0]
    step = pl.program_id(0)

    @pl.when(step == 0)
    def _():
        cnt_scr[...] = jnp.zeros_like(cnt_scr)
        logit_scr[...] = jnp.zeros_like(logit_scr)

    vals = logit_scr[...]
    routed = (step > 0).astype(F32)

    h = (x_ref[...] + _dot(ys_ref[...], wo_ref[:SSM_WIDTH, :]) + _dot(yl_ref[...], wo_ref[SSM_WIDTH:, :]))
    h_ref[...] = h
    hn = _rms(h, nw_ref[...])
    _store_row_tiles(hn_ref, hn)
    logit_scr[...] = _dot(hn.astype(BF16), wr_ref[...]) + br_ref[...]

    lane = lax.broadcasted_iota(jnp.int32, (tm, LANES), 1)
    member = jnp.zeros((tm, LANES), F32)
    tops, idxs, sels = [], [], []
    for _ in range(TOP_K):
        m = jnp.max(vals, axis=-1, keepdims=True)
        idx = jnp.min(jnp.where(vals == m, lane, LANES), axis=-1, keepdims=True)
        sel = lane == idx
        vals = jnp.where(sel, NEG_INF, vals)
        member = member + sel.astype(F32)
        tops.append(m)
        idxs.append(idx)
        sels.append(sel)
    es = [jnp.exp(t - tops[0]) for t in tops]
    inv = 1.0 / (es[0] + es[1] + es[2] + es[3])
    member = member * routed

    r = lax.broadcasted_iota(jnp.int32, (tm, tm), 0)
    c = lax.broadcasted_iota(jnp.int32, (tm, tm), 1)
    strict = (r > c).astype(F32).astype(BF16)
    carry = cnt_scr[0:1, :]
    rank_all = _dot(strict, member.astype(BF16)) + carry
    total = carry + jnp.sum(member, axis=0, keepdims=True)
    cnt_scr[...] = jnp.broadcast_to(total, cnt_scr.shape)
    cnt_ref[...] = jnp.broadcast_to(total, cnt_ref.shape)

    meta = jnp.zeros((tm, LANES), F32)
    for kk in range(TOP_K):
        rank = jnp.sum(jnp.where(sels[kk], rank_all, 0.0), axis=-1, keepdims=True)
        meta = jnp.where(lane == META_IDX + kk, idxs[kk].astype(F32), meta)
        meta = jnp.where(lane == META_GATE + kk, es[kk] * inv, meta)
        meta = jnp.where(lane == META_RANK + kk, rank, meta)
    meta_ref[...] = meta
    meta_t_ref[...] = meta.T[:META_ROWS, :]


def _outproj(ys, yl, x2, wo, nw, wr, br):
    t = x2.shape[0]
    tm = min(ROW_TILE, t)
    n = t // tm
    projected = lambda i: (jnp.minimum(i, n - 1), 0)
    routed = lambda i: (jnp.maximum(i - 1, 0), 0)
    tokspec = lambda w: pl.BlockSpec((tm, w), projected)
    const = lambda i: (0, 0)
    return pl.pallas_call(
        _outproj_kernel,
        grid=(n + 1,),
        in_specs=[tokspec(SSM_WIDTH), tokspec(LSTM_WIDTH), tokspec(D_MODEL),
                  pl.BlockSpec((SSM_WIDTH + LSTM_WIDTH, D_MODEL), const, pipeline_mode=pl.Buffered(1)),
                  pl.BlockSpec((1, D_MODEL), const),
                  pl.BlockSpec((D_MODEL, LANES), const),
                  pl.BlockSpec((1, LANES), const)],
        out_specs=[tokspec(D_MODEL), pl.BlockSpec((tm * ROW_SUBTILES, LANES), projected),
                   pl.BlockSpec((tm, LANES), routed),
                   pl.BlockSpec((META_ROWS, tm), lambda i: (0, jnp.maximum(i - 1, 0))),
                   pl.BlockSpec((SUBLANES, LANES), const)],
        out_shape=[jax.ShapeDtypeStruct((t, D_MODEL), F32), jax.ShapeDtypeStruct((t * ROW_SUBTILES, LANES), F32),
                   jax.ShapeDtypeStruct((t, LANES), F32), jax.ShapeDtypeStruct((META_ROWS, t), F32),
                   jax.ShapeDtypeStruct((SUBLANES, LANES), F32)],
        scratch_shapes=[pltpu.VMEM((SUBLANES, LANES), F32), pltpu.VMEM((tm, LANES), F32)],
        compiler_params=_cparams(("arbitrary",)),
        name="outproj_router",
    )(ys, yl, x2, wo, nw, wr, br)


DISPATCH_TILE = 1024
DISPATCH_GROUP = 4
PAD_CHUNKS = (128, 64, 32, 16, 8, 4, 2, 1)


def _dispatch_kernel(padstart_ref, padlen_ref, misc_ref, pos_ref, hn_ref, xs_hbm, sem, zsem, zeros_scr):
    i = pl.program_id(0)
    td = pos_ref.shape[-1] // TOP_K

    @pl.when(i == 0)
    def _():
        zeros_scr[...] = jnp.zeros_like(zeros_scr)
        for e in range(N_EXPERTS):
            row = padstart_ref[e]
            nrow = padlen_ref[e]
            for chunk in PAD_CHUNKS:
                @pl.when((nrow & chunk) != 0)
                def _(row=row, chunk=chunk):
                    pltpu.make_async_copy(
                        zeros_scr.at[pl.ds(0, chunk * ROW_SUBTILES), :],
                        xs_hbm.at[pl.ds(pl.multiple_of(row * ROW_SUBTILES, ROW_SUBTILES), chunk * ROW_SUBTILES), :],
                        zsem).start()
                row = row + (nrow & chunk)

        def tail_copy(blk):
            return pltpu.make_async_copy(
                zeros_scr,
                xs_hbm.at[pl.ds(pl.multiple_of(blk * (MOE_BLOCK * ROW_SUBTILES), ROW_SUBTILES),
                                MOE_BLOCK * ROW_SUBTILES), :], zsem)

        n_used, n_blocks = misc_ref[1], misc_ref[2]

        def start_tail(blk, carry):
            tail_copy(blk).start()
            return carry
        lax.fori_loop(n_used, n_blocks, start_tail, 0)

        def drain_row(j, carry):
            _wait_row_copies(zeros_scr, xs_hbm, 1, zsem)
            return carry
        lax.fori_loop(0, misc_ref[0], drain_row, 0)

        def drain_tail(blk, carry):
            tail_copy(blk).wait()
            return carry
        lax.fori_loop(n_used, n_blocks, drain_tail, 0)

    def body(g, carry):
        r0 = g * DISPATCH_GROUP
        slots = [pos_ref[0, 0, (j % TOP_K) * td + r0 + j // TOP_K] for j in range(DISPATCH_GROUP * TOP_K)]
        for j, slot in enumerate(slots):
            _row_copy(hn_ref, r0 + j // TOP_K, xs_hbm, slot, sem).start(priority=j % DMA_PRIORITIES)
        return carry
    lax.fori_loop(0, td // DISPATCH_GROUP, body, 0)

    for _ in range(TOP_K):
        _wait_row_copies(hn_ref, xs_hbm, td, sem)


def _tile_slots(pos, tile):
    t = pos.shape[1]
    return pos.reshape(TOP_K, t // tile, tile).transpose(1, 0, 2).reshape(t // tile, 1, TOP_K * tile)


def _dispatch(pad_start, pad_len, misc, pos, hn_rt, n_rows):
    t = pos.shape[1]
    td = min(DISPATCH_TILE, t)
    n = t // td
    grid_spec = pltpu.PrefetchScalarGridSpec(
        num_scalar_prefetch=3,
        grid=(n,),
        in_specs=[pl.BlockSpec((1, 1, TOP_K * td), lambda i, *_: (i, 0, 0), memory_space=pltpu.SMEM),
                  pl.BlockSpec((td * ROW_SUBTILES, LANES), lambda i, *_: (i, 0))],
        out_specs=pl.BlockSpec(memory_space=pl.ANY),
        scratch_shapes=[pltpu.SemaphoreType.DMA(()), pltpu.SemaphoreType.DMA(()),
                        pltpu.VMEM((MOE_BLOCK * ROW_SUBTILES, LANES), F32)],
    )
    return pl.pallas_call(
        _dispatch_kernel,
        grid_spec=grid_spec,
        out_shape=jax.ShapeDtypeStruct((n_rows * ROW_SUBTILES, LANES), F32),
        compiler_params=_cparams(("arbitrary",)),
        name="dispatch",
    )(pad_start, pad_len, misc, _tile_slots(pos, td), hn_rt)


def _experts_kernel(be_ref, nused_ref, next_ref, x_ref, wgu_hbm, bgu_ref, wdn_hbm, bdn_ref, out_ref,
                    wgu_bf, wdn_bf, wgu_st, wdn_st, wsem, slot_ref):
    i = pl.program_id(0)
    n_used = nused_ref[0]

    def weight_copies(e, slot):
        return (pltpu.make_async_copy(wgu_hbm.at[e], wgu_st.at[slot], wsem.at[0, slot]),
                pltpu.make_async_copy(wdn_hbm.at[e], wdn_st.at[slot], wsem.at[1, slot]))

    @pl.when(i == 0)
    def _():
        slot_ref[0] = 0
        for cp in weight_copies(be_ref[0], 0):
            cp.start()

    @pl.when(i < n_used)
    def _():
        @pl.when(jnp.logical_or(i == 0, be_ref[i] != be_ref[jnp.maximum(i - 1, 0)]))
        def _():
            e = be_ref[i]
            slot = slot_ref[0]
            for cp in weight_copies(e, slot):
                cp.wait()
            wgu_bf[...] = wgu_st[slot].astype(BF16)
            wdn_bf[...] = wdn_st[slot].astype(BF16)
            nxt = next_ref[e]

            @pl.when(nxt >= 0)
            def _():
                for cp in weight_copies(nxt, 1 - slot):
                    cp.start()
            slot_ref[0] = 1 - slot

        xb = jnp.concatenate(_load_row_tile_cols(x_ref, MOE_BLOCK), axis=1).astype(BF16)
        hh = _dot(xb, wgu_bf[...]) + bgu_ref[...]
        gh = jnp.minimum(hh[:, :D_FF], SWIGLU_LIMIT)
        uh = jnp.clip(hh[:, D_FF:], -SWIGLU_LIMIT, SWIGLU_LIMIT)
        act = (uh + 1.0) * (gh * _sigmoid(SWIGLU_ALPHA * gh))
        _store_row_tiles(out_ref, _dot(act.astype(BF16), wdn_bf[...]) + bdn_ref[...])

    @pl.when(i >= n_used)
    def _():
        out_ref[...] = jnp.zeros_like(out_ref)


def _experts(block_expert, n_used, next_expert, xs_rt, wgu, bgu3, wdn, bdn3):
    n_blocks = block_expert.shape[0]
    blk = MOE_BLOCK * ROW_SUBTILES
    grid_spec = pltpu.PrefetchScalarGridSpec(
        num_scalar_prefetch=3,
        grid=(n_blocks,),
        in_specs=[
            pl.BlockSpec((blk, LANES), lambda i, be, nu, nx: (i, 0)),
            pl.BlockSpec(memory_space=pl.ANY),
            pl.BlockSpec((None, 1, 2 * D_FF), lambda i, be, nu, nx: (be[i], 0, 0)),
            pl.BlockSpec(memory_space=pl.ANY),
            pl.BlockSpec((None, 1, D_MODEL), lambda i, be, nu, nx: (be[i], 0, 0)),
        ],
        out_specs=pl.BlockSpec((blk, LANES), lambda i, be, nu, nx: (i, 0)),
        scratch_shapes=[pltpu.VMEM((D_MODEL, 2 * D_FF), BF16),
                        pltpu.VMEM((D_FF, D_MODEL), BF16),
                        pltpu.VMEM((2, D_MODEL, 2 * D_FF), F32),
                        pltpu.VMEM((2, D_FF, D_MODEL), F32),
                        pltpu.SemaphoreType.DMA((2, 2)),
                        pltpu.SMEM((1,), jnp.int32)],
    )
    return pl.pallas_call(
        _experts_kernel,
        grid_spec=grid_spec,
        out_shape=jax.ShapeDtypeStruct((n_blocks * blk, LANES), F32),
        compiler_params=_cparams(("arbitrary",)),
        name="experts",
    )(block_expert, n_used, next_expert, xs_rt, wgu, bgu3, wdn, bdn3)


def _combine_kernel(pos_cur, pos_next, outs_hbm, h_ref, meta_ref, nw_ref, y_ref, buf, sem):
    i = pl.program_id(0)
    n = pl.num_programs(0)
    tc = h_ref.shape[0]
    slot = i % 2

    def issue(pos_ref, s):
        def body(r, carry):
            for kk in range(TOP_K):
                _row_copy(outs_hbm, pos_ref[0, 0, kk * tc + r], buf.at[s, kk], r,
                          sem.at[s]).start(priority=kk % DMA_PRIORITIES)
            return carry
        lax.fori_loop(0, tc, body, 0, unroll=4)

    @pl.when(i == 0)
    def _():
        issue(pos_cur, 0)

    @pl.when(i + 1 < n)
    def _():
        issue(pos_next, 1 - slot)

    for kk in range(TOP_K):
        _wait_row_copies(outs_hbm, buf.at[slot, kk], tc, sem.at[slot])
    meta = meta_ref[...]
    gates = [meta[:, META_GATE + kk:META_GATE + kk + 1] for kk in range(TOP_K)]
    rows = [_load_row_tile_cols(buf.at[slot, kk], tc) for kk in range(TOP_K)]
    cols = []
    for s in range(ROW_SUBTILES):
        acc = h_ref[:, s * LANES:(s + 1) * LANES]
        for kk in range(TOP_K):
            acc = acc + gates[kk] * rows[kk][s]
        cols.append(acc)
    y_ref[...] = _rms(jnp.concatenate(cols, axis=1), nw_ref[...])


def _combine(pos, outs_rt, h, meta, nw):
    t = h.shape[0]
    tc = min(COMBINE_TILE, t)
    n = t // tc
    pos3 = _tile_slots(pos, tc)
    return pl.pallas_call(
        _combine_kernel,
        grid=(n,),
        in_specs=[
            pl.BlockSpec((1, 1, TOP_K * tc), lambda i: (i, 0, 0), memory_space=pltpu.SMEM),
            pl.BlockSpec((1, 1, TOP_K * tc), lambda i: (jnp.minimum(i + 1, n - 1), 0, 0),
                         memory_space=pltpu.SMEM),
            pl.BlockSpec(memory_space=pl.ANY),
            pl.BlockSpec((tc, D_MODEL), lambda i: (i, 0)),
            pl.BlockSpec((tc, LANES), lambda i: (i, 0)),
            pl.BlockSpec((1, D_MODEL), lambda i: (0, 0)),
        ],
        out_specs=pl.BlockSpec((tc, D_MODEL), lambda i: (i, 0)),
        out_shape=jax.ShapeDtypeStruct((t, D_MODEL), F32),
        scratch_shapes=[pltpu.VMEM((2, TOP_K, tc * ROW_SUBTILES, LANES), F32), pltpu.SemaphoreType.DMA((2,))],
        compiler_params=_cparams(("arbitrary",)),
        name="combine",
    )(pos3, pos3, outs_rt, h, meta, nw)


def _pad_lanes(v, fill=0.0):
    v = v.astype(F32).reshape(1, -1)
    return jnp.pad(v, ((0, 0), (0, LANES - v.shape[1])), constant_values=fill)


def kernel(x, norm_mix_w, w_in, conv_w, conv_b, dt_bias, a_log, d_skip, ssm_norm_w, lstm_i_bias,
           lstm_f_bias, lstm_norm_w, w_out, norm_ffn_w, w_router, b_router, w_gate_up, b_gate_up,
           w_down, b_down, norm_final_w):
    b, s, d = x.shape
    t = b * s
    x2 = x.reshape(t, d).astype(F32)
    depth = w_in.shape[0]
    assert depth == 1, "the combine kernel fuses the final norm, so exactly one layer is supported"
    for layer in range(depth):
        w_all = _regroup_weights(w_in.astype(F32), layer)
        sel = (jnp.arange(LANES)[:, None] == (jnp.arange(SSM_WIDTH) // SSM_HEAD_DIM)[None, :]).astype(BF16)
        a_neg = _pad_lanes(-jnp.exp(a_log[layer].astype(F32)))
        dskip_x = jnp.repeat(d_skip[layer].astype(F32), SSM_HEAD_DIM).reshape(1, SSM_WIDTH)
        wr = jnp.pad(w_router[layer], ((0, 0), (0, LANES - N_EXPERTS))).astype(BF16)
        br = _pad_lanes(b_router[layer], fill=NEG_INF)

        z, xbc, q, k, v, o, gates = _inproj(x2, norm_mix_w[layer].reshape(1, d).astype(F32), w_all)
        y_ssd, y_lstm = _mixers(
            xbc, z, gates, conv_w[layer].astype(F32), conv_b[layer].reshape(1, -1).astype(F32),
            _pad_lanes(dt_bias[layer]), a_neg, dskip_x, ssm_norm_w[layer].reshape(1, -1).astype(F32), sel,
            q, k, v, o, _pad_lanes(lstm_i_bias[layer]), _pad_lanes(lstm_f_bias[layer]),
            lstm_norm_w[layer].reshape(1, -1).astype(F32), b, s)
        h, hn_rt, meta, meta_t, cnt = _outproj(y_ssd, y_lstm, x2, w_out[layer].astype(BF16),
                                       norm_ffn_w[layer].reshape(1, d).astype(F32), wr, br)

        idx = meta_t[META_IDX:META_IDX + TOP_K].astype(jnp.int32)
        rank = meta_t[META_RANK:META_RANK + TOP_K].astype(jnp.int32)
        counts = cnt[0, :N_EXPERTS].astype(jnp.int32)
        n_blocks = -(-(t * TOP_K) // MOE_BLOCK) + N_EXPERTS
        padded = (counts + MOE_BLOCK - 1) // MOE_BLOCK * MOE_BLOCK
        padded_ends = jnp.cumsum(padded)
        padded_starts = padded_ends - padded
        onehot = idx[..., None] == jnp.arange(N_EXPERTS, dtype=jnp.int32)
        pos = jnp.sum(jnp.where(onehot, padded_starts, 0), axis=-1) + rank
        block_start = jnp.arange(n_blocks, dtype=jnp.int32) * MOE_BLOCK
        block_expert = jnp.minimum(
            jnp.sum((padded_ends[None, :] <= block_start[:, None]).astype(jnp.int32), axis=1), N_EXPERTS - 1)
        n_used = padded_ends[-1:] // MOE_BLOCK
        pad_len = padded - counts
        misc = jnp.concatenate([jnp.sum(pad_len, keepdims=True), n_used,
                                jnp.full((1,), n_blocks, jnp.int32)])

        xs_rt = _dispatch(padded_starts + counts, pad_len, misc, pos, hn_rt, n_blocks * MOE_BLOCK)
        eids = jnp.arange(N_EXPERTS, dtype=jnp.int32)
        later_nonempty = (eids[None, :] > eids[:, None]) & (counts[None, :] > 0)
        next_expert = jnp.min(jnp.where(later_nonempty, eids[None, :], N_EXPERTS), axis=1)
        next_expert = jnp.where(next_expert < N_EXPERTS, next_expert, -1)
        outs_rt = _experts(block_expert, n_used, next_expert, xs_rt, w_gate_up[layer],
                           b_gate_up[layer].reshape(N_EXPERTS, 1, -1), w_down[layer],
                           b_down[layer].reshape(N_EXPERTS, 1, -1))
        x2 = _combine(pos, outs_rt, h, meta, norm_final_w.reshape(1, d).astype(F32))
    return x2.reshape(b, s, d).astype(x.dtype)
```

```python
import functools

import jax
import jax.numpy as jnp
import numpy as np
from jax import lax
from jax.experimental import pallas as pl
from jax.experimental.pallas import tpu as pltpu

F32 = jnp.float32
BF16 = jnp.bfloat16

D_MODEL = 1024
SSM_WIDTH = 1024
SSM_HEAD_DIM = 64
SSM_HEADS = 16
SSM_GROUPS = 2
SSM_STATE = 128
CONV_WIDTH = 4
CONV_DIM = SSM_WIDTH + 2 * SSM_GROUPS * SSM_STATE
LSTM_WIDTH = 1024
LSTM_HEAD_DIM = 128
LSTM_HEADS = 8
N_EXPERTS = 32
TOP_K = 4
D_FF = 1024
SWIGLU_LIMIT = 7.0
SWIGLU_ALPHA = 1.702
MOE_BLOCK = 256
RMS_EPS = 1e-6

LANES = 128
SUBLANES = 8
MIX_CHUNK = 256
ROW_TILE = 512
COMBINE_TILE = 512
VMEM_LIMIT = 56 * 1024 * 1024

GATE_COLS = LANES
GATE_LIVE = SSM_HEADS + 2 * LSTM_HEADS
NEG_INF = float("-inf")


def _cparams(sem):
    return pltpu.CompilerParams(dimension_semantics=sem, vmem_limit_bytes=VMEM_LIMIT)


def _rms(x, w):
    return x * lax.rsqrt(jnp.mean(x * x, axis=-1, keepdims=True) + RMS_EPS) * w


LOG2E = 1.4426950408889634


def _sigmoid(x):
    return 1.0 / (1.0 + jnp.exp2(x * -LOG2E))


def _softplus(x):
    return jnp.maximum(x, 0.0) + jnp.log(1.0 + jnp.exp(-jnp.abs(x)))


def _split3(a):
    hi = a.astype(BF16)
    r = a - hi.astype(F32)
    mid = r.astype(BF16)
    lo = (r - mid.astype(F32)).astype(BF16)
    return hi, mid, lo


def _dot(a, b):
    return jnp.dot(a, b, preferred_element_type=F32)


def _dot_nt(a, b):
    return lax.dot_general(a, b, (((1,), (1,)), ((), ())), preferred_element_type=F32)


def _dot_tn(a, b):
    return lax.dot_general(a, b, (((0,), (0,)), ((), ())), preferred_element_type=F32)


def _sel_dot(sel_bf, a):
    hi, mid, lo = _split3(a)
    return _dot(sel_bf, hi) + _dot(sel_bf, mid) + _dot(sel_bf, lo)


def _expand(a, sel_bf, terms=3):
    out = None
    for piece in _split3(a)[:terms]:
        d = _dot(piece, sel_bf)
        out = d if out is None else out + d
    return out


ROW_SUBTILES = D_MODEL // LANES
DMA_PRIORITIES = 2


def _store_row_tiles(ref, x):
    n = x.shape[0]
    for s in range(ROW_SUBTILES):
        ref[pl.ds(s, n, stride=ROW_SUBTILES), :] = x[:, s * LANES:(s + 1) * LANES]


def _load_row_tile_cols(ref, n):
    return [ref[pl.ds(s, n, stride=ROW_SUBTILES), :] for s in range(ROW_SUBTILES)]


def _row_copy(src, src_row, dst, dst_row, sem):
    return pltpu.make_async_copy(
        src.at[pl.ds(pl.multiple_of(src_row * ROW_SUBTILES, ROW_SUBTILES), ROW_SUBTILES), :],
        dst.at[pl.ds(pl.multiple_of(dst_row * ROW_SUBTILES, ROW_SUBTILES), ROW_SUBTILES), :], sem)


def _wait_row_copies(src, dst, n_rows, sem):
    size = n_rows * ROW_SUBTILES
    pltpu.make_async_copy(src.at[pl.ds(0, size), :], dst.at[pl.ds(0, size), :], sem).wait()


_INPROJ_WIDTHS = (SSM_WIDTH, CONV_DIM, LSTM_WIDTH, LSTM_WIDTH, LSTM_WIDTH, LSTM_WIDTH, GATE_COLS)
_C_XBC_END = SSM_WIDTH + CONV_DIM
_C_DT_END = _C_XBC_END + SSM_HEADS
_C_O_END = _C_DT_END + 4 * LSTM_WIDTH
_C_I_END = _C_O_END + LSTM_HEADS
IN_PROJ_DIM = _C_I_END + LSTM_HEADS
REGROUP_ROWS = 128


def _regroup_kernel(w_ref, o_ref):
    rows = w_ref.shape[0]
    big = _C_XBC_END + 4 * LSTM_WIDTH
    o_ref[:, :_C_XBC_END] = w_ref[:, :_C_XBC_END].astype(BF16)
    o_ref[:, _C_XBC_END:big] = w_ref[:, _C_DT_END:_C_O_END].astype(BF16)
    gate_w = jnp.concatenate([w_ref[:, _C_XBC_END:_C_DT_END], w_ref[:, _C_O_END:IN_PROJ_DIM],
                              jnp.zeros((rows, GATE_COLS - GATE_LIVE), F32)], axis=1)
    o_ref[:, big:] = gate_w.astype(BF16)


def _regroup_weights(w_in, layer):
    k = w_in.shape[1]
    ncol = sum(_INPROJ_WIDTHS)
    return pl.pallas_call(
        _regroup_kernel,
        grid=(k // REGROUP_ROWS,),
        in_specs=[pl.BlockSpec((None, REGROUP_ROWS, IN_PROJ_DIM), lambda i: (layer, i, 0))],
        out_specs=pl.BlockSpec((REGROUP_ROWS, ncol), lambda i: (i, 0)),
        out_shape=jax.ShapeDtypeStruct((k, ncol), BF16),
        compiler_params=_cparams(("arbitrary",)),
        name="regroup_weights",
    )(w_in)


def _inproj_kernel(x_ref, nw_ref, w_ref, z_ref, xbc_ref, q_ref, k_ref, v_ref, o_ref, g_ref):
    xb = _rms(x_ref[...], nw_ref[...]).astype(BF16)
    off = 0
    for ref, width in zip((z_ref, xbc_ref, q_ref, k_ref, v_ref, o_ref, g_ref), _INPROJ_WIDTHS):
        ref[...] = _dot(xb, w_ref[:, off:off + width]).astype(ref.dtype)
        off += width


def _inproj(x2, nw, w_all):
    t = x2.shape[0]
    tm = min(ROW_TILE, t)
    ncol = w_all.shape[1]
    out_shape = [jax.ShapeDtypeStruct((t, w), BF16) for w in _INPROJ_WIDTHS[:-1]]
    out_shape.append(jax.ShapeDtypeStruct((t, GATE_COLS), F32))
    return pl.pallas_call(
        _inproj_kernel,
        grid=(t // tm,),
        in_specs=[
            pl.BlockSpec((tm, D_MODEL), lambda i: (i, 0)),
            pl.BlockSpec((1, D_MODEL), lambda i: (0, 0)),
            pl.BlockSpec((D_MODEL, ncol), lambda i: (0, 0), pipeline_mode=pl.Buffered(1)),
        ],
        out_specs=[pl.BlockSpec((tm, w), lambda i: (i, 0)) for w in _INPROJ_WIDTHS],
        out_shape=out_shape,
        compiler_params=_cparams(("arbitrary",)),
        name="inproj",
    )(x2, nw, w_all)


SSD_PAIR_GROUP = 2
LSTM_HEAD_GROUP = 2


def _ssd_phases(xbc_ref, z_ref, g_ref, convw_ref, convb_ref, dtb_ref, aneg_ref, dskip_ref, nw_ref,
                sel_ref, y_ref, u_scr, st_scr):
    L = xbc_ref.shape[0]
    gw = SSM_WIDTH // SSM_GROUPS

    row = lax.broadcasted_iota(jnp.int32, (L, L), 0)
    col = lax.broadcasted_iota(jnp.int32, (L, L), 1)
    causal = row >= col

    xin_bf = xbc_ref[...]
    xin = xin_bf.astype(F32)
    halo = u_scr[...]
    row8 = lax.broadcasted_iota(jnp.int32, (SUBLANES, 1), 0)
    acc = convb_ref[...] + convw_ref[CONV_WIDTH - 1:CONV_WIDTH, :] * xin
    for back in range(1, CONV_WIDTH):
        shifted = _dot((row - col == back).astype(F32).astype(BF16), xin_bf)
        head = shifted[:SUBLANES] + jnp.where(row8 < back, pltpu.roll(halo, back, axis=0), 0.0)
        shifted = jnp.concatenate([head, shifted[SUBLANES:]], axis=0)
        acc = acc + convw_ref[CONV_WIDTH - 1 - back:CONV_WIDTH - back, :] * shifted
    u_scr[...] = xin[L - SUBLANES:, :]
    xbc = acc * _sigmoid(acc)
    xs = xbc[:, :SSM_WIDTH]
    bm = xbc[:, SSM_WIDTH:SSM_WIDTH + SSM_GROUPS * SSM_STATE].astype(BF16)
    cm = xbc[:, SSM_WIDTH + SSM_GROUPS * SSM_STATE:].astype(BF16)
    yield

    tril_bf = causal.astype(F32).astype(BF16)

    lane = lax.broadcasted_iota(jnp.int32, (1, LANES), 1)
    dt = jnp.where(lane < SSM_HEADS, _softplus(g_ref[...] + dtb_ref[...]), 0.0)
    acs = _sel_dot(tril_bf, dt * aneg_ref[...]) * LOG2E
    acs_t = acs.T
    sel = sel_ref[...]
    acs_last = acs[L - 1:L, :]
    ea_x = _expand(jnp.exp2(acs), sel, terms=2)
    xd = xs * _expand(dt, sel, terms=1)
    xd_bf = xd.astype(BF16)
    xw = (xd * _expand(jnp.exp2(acs_last - acs), sel, terms=1)).astype(BF16)
    yield

    cbs = [_dot_nt(cm[:, g * SSM_STATE:(g + 1) * SSM_STATE], bm[:, g * SSM_STATE:(g + 1) * SSM_STATE])
           for g in range(SSM_GROUPS)]
    yoff = [_dot(cm[:, g * SSM_STATE:(g + 1) * SSM_STATE], st_scr[g].astype(BF16)) for g in range(SSM_GROUPS)]
    upd = [_dot_tn(bm[:, g * SSM_STATE:(g + 1) * SSM_STATE], xw[:, g * gw:(g + 1) * gw])
           for g in range(SSM_GROUPS)]
    for g in range(SSM_GROUPS):
        st_scr[g] = st_scr[g] * ea_x[L - 1:L, g * gw:(g + 1) * gw] + upd[g]
    yield

    pair_lane = lax.broadcasted_iota(jnp.int32, (1, LANES), 1)
    heads_per_group = SSM_HEADS // SSM_GROUPS
    rhs = []
    for j in range(SSM_HEADS // 2):
        xpair = xd_bf[:, j * LANES:(j + 1) * LANES]
        zero = jnp.zeros_like(xpair)
        rhs.append(jnp.concatenate([jnp.where(pair_lane < SSM_HEAD_DIM, xpair, zero),
                                    jnp.where(pair_lane >= SSM_HEAD_DIM, xpair, zero)], axis=0))
    ydiag = []
    for j0 in range(0, SSM_HEADS // 2, SSD_PAIR_GROUP):
        pairs = range(j0, j0 + SSD_PAIR_GROUP)
        ms = {h: (cbs[h // heads_per_group]
                  * jnp.exp2(jnp.where(causal, acs[:, h:h + 1] - acs_t[h:h + 1, :], NEG_INF))).astype(BF16)
              for j in pairs for h in (2 * j, 2 * j + 1)}
        yield
        ydiag += [_dot(jnp.concatenate([ms[2 * j], ms[2 * j + 1]], axis=1), rhs[j]) for j in pairs]
        yield
    y = jnp.concatenate(ydiag, axis=1) + jnp.concatenate(yoff, axis=1) * ea_x + dskip_ref[...] * xs

    zz = z_ref[...].astype(F32)
    y = y * (zz * _sigmoid(zz))
    y_ref[...] = _rms(y, nw_ref[...]).astype(BF16)


def _mlstm_phases(q_ref, k_ref, v_ref, o_ref, g_ref, ib_ref, fb_ref, nw_ref, y_ref,
                  st_scr, m_scr):
    L = q_ref.shape[0]
    dh = LSTM_HEAD_DIM
    scale = dh ** -0.5

    row = lax.broadcasted_iota(jnp.int32, (L, L), 0)
    col = lax.broadcasted_iota(jnp.int32, (L, L), 1)
    causal = row >= col
    tril_bf = causal.astype(F32).astype(BF16)
    lane = lax.broadcasted_iota(jnp.int32, (1, LANES), 1)
    live = lane < LSTM_HEADS

    gates = g_ref[...]
    gi = pltpu.roll(gates, LANES - SSM_HEADS, axis=1)
    gf = pltpu.roll(gates, LANES - SSM_HEADS - LSTM_HEADS, axis=1)
    ii = jnp.where(live, gi + ib_ref[...], 0.0)
    logf = jnp.where(live, -_softplus(-(gf + fb_ref[...])), 0.0)
    cumf = _sel_dot(tril_bf, logf)
    g = ii - cumf
    rid = lax.broadcasted_iota(jnp.int32, (L, LANES), 0)
    cmx = g
    step = 1
    while step < L:
        cmx = jnp.maximum(cmx, jnp.where(rid >= step, pltpu.roll(cmx, step, axis=0), NEG_INF))
        step *= 2
    m_prev = m_scr[...]
    mx = jnp.maximum(m_prev, cmx)
    w_inter = jnp.exp(m_prev - mx)
    enm = jnp.exp(-(cumf + mx))
    g2_t = (g * LOG2E).T
    mx2 = mx * LOG2E - np.log2(scale)
    m_last = mx[L - 1:L, :]
    wk = jnp.exp(g - m_last) * scale
    sc = jnp.exp(m_prev - m_last)
    m_scr[...] = cumf[L - 1:L, :] + m_last
    yield

    ones_bf = jnp.ones((L, dh), BF16)
    heads = range(LSTM_HEADS)
    hsl = [slice(h * dh, (h + 1) * dh) for h in heads]
    vaug = [jnp.concatenate([v_ref[:, hsl[h]], ones_bf], axis=1) for h in heads]
    r2 = [_dot(q_ref[:, hsl[h]], st_scr[h].astype(BF16)) for h in heads]
    upd = [_dot_tn((k_ref[:, hsl[h]].astype(F32) * wk[:, h:h + 1]).astype(BF16), vaug[h]) for h in heads]
    for h in heads:
        st_scr[h] = st_scr[h] * sc[:, h:h + 1] + upd[h]
    yield
    r1 = []
    for h0 in range(0, LSTM_HEADS, LSTM_HEAD_GROUP):
        group = range(h0, h0 + LSTM_HEAD_GROUP)
        s_qk = {h: _dot_nt(q_ref[:, hsl[h]], k_ref[:, hsl[h]]) for h in group}
        yield
        p = {h: (s_qk[h] * jnp.exp2(jnp.where(causal, g2_t[h:h + 1, :] - mx2[:, h:h + 1], NEG_INF))).astype(BF16)
             for h in group}
        yield
        r1 += [_dot(p[h], vaug[h]) for h in group]
        yield
    hh = []
    for h in heads:
        wcol = w_inter[:, h:h + 1]
        num = r1[h][:, :dh] + r2[h][:, :dh] * wcol
        den = r1[h][:, dh:] + r2[h][:, dh:] * wcol
        hh.append(num / jnp.maximum(jnp.abs(den), enm[:, h:h + 1]))
    inv = [lax.rsqrt(jnp.mean(hh[h] * hh[h], axis=-1, keepdims=True) + RMS_EPS) for h in heads]
    for h in heads:
        oo = o_ref[:, hsl[h]].astype(F32)
        y_ref[:, hsl[h]] = (_sigmoid(oo) * (hh[h] * inv[h] * nw_ref[:, hsl[h]])).astype(BF16)


N_SSD_IN, N_LSTM_IN = 10, 8


def _mixer_kernel(*refs):
    ssd_in = refs[:N_SSD_IN]
    lstm_in = refs[N_SSD_IN:N_SSD_IN + N_LSTM_IN]
    y_ssd, y_lstm, u_scr, st_ssd, st_lstm, m_scr = refs[N_SSD_IN + N_LSTM_IN:]

    @pl.when(pl.program_id(1) == 0)
    def _():
        u_scr[...] = jnp.zeros_like(u_scr)
        st_ssd[...] = jnp.zeros_like(st_ssd)
        st_lstm[...] = jnp.zeros_like(st_lstm)
        m_scr[...] = jnp.zeros_like(m_scr)

    streams = [_ssd_phases(*ssd_in, y_ssd, u_scr, st_ssd), _mlstm_phases(*lstm_in, y_lstm, st_lstm, m_scr)]
    while streams:
        for stream in list(streams):
            if next(stream, StopIteration) is StopIteration:
                streams.remove(stream)


def _mixers(xbc, z, gates, convw, convb, dtb, aneg, dskip_x, nw_ssd, sel, q, k, v, o, ib, fb, nw_lstm, b, s):
    L = min(MIX_CHUNK, s)
    nc = s // L
    tok = lambda bi, ci: (bi * nc + ci, 0)
    const = lambda bi, ci: (0, 0)
    big = pl.BlockSpec((L, LSTM_WIDTH), tok)
    gate_block = lambda j: pl.BlockSpec((L, LANES), lambda bi, ci: (bi * nc + ci, j))
    vec = lambda w: pl.BlockSpec((1, w), const)
    in_specs = [pl.BlockSpec((L, CONV_DIM), tok), pl.BlockSpec((L, SSM_WIDTH), tok), gate_block(0),
                pl.BlockSpec((CONV_WIDTH, CONV_DIM), const), vec(CONV_DIM), vec(LANES), vec(LANES),
                vec(SSM_WIDTH), vec(SSM_WIDTH), pl.BlockSpec((LANES, SSM_WIDTH), const),
                big, big, big, big, gate_block(0), vec(LANES), vec(LANES), vec(LSTM_WIDTH)]
    assert len(in_specs) == N_SSD_IN + N_LSTM_IN
    return pl.pallas_call(
        _mixer_kernel,
        grid=(b, nc),
        in_specs=in_specs,
        out_specs=[pl.BlockSpec((L, SSM_WIDTH), tok), big],
        out_shape=[jax.ShapeDtypeStruct((b * s, SSM_WIDTH), BF16), jax.ShapeDtypeStruct((b * s, LSTM_WIDTH), BF16)],
        scratch_shapes=[pltpu.VMEM((SUBLANES, CONV_DIM), F32),
                        pltpu.VMEM((SSM_GROUPS, SSM_STATE, SSM_WIDTH // SSM_GROUPS), F32),
                        pltpu.VMEM((LSTM_HEADS, LSTM_HEAD_DIM, 2 * LSTM_HEAD_DIM), F32),
                        pltpu.VMEM((1, LANES), F32)],
        compiler_params=_cparams(("arbitrary", "arbitrary")),
        name="mixers",
    )(xbc, z, gates, convw, convb, dtb, aneg, dskip_x, nw_ssd, sel, q, k, v, o, gates, ib, fb, nw_lstm)


META_IDX, META_GATE, META_RANK = 0, TOP_K, 2 * TOP_K
META_ROWS = 2 * SUBLANES


def _outproj_kernel(ys_ref, yl_ref, x_ref, wo_ref, nw_ref, wr_ref, br_ref, h_ref, hn_ref, meta_ref,
                    meta_t_ref, cnt_ref, cnt_scr, logit_scr):
    tm = x_ref.shape[0]
    step = pl.program_id(0)

    @pl.when(step == 0)
    def _():
        cnt_scr[...] = jnp.zeros_like(cnt_scr)
        logit_scr[...] = jnp.zeros_like(logit_scr)

    vals = logit_scr[...]
    routed = (step > 0).astype(F32)

    h = (x_ref[...] + _dot(ys_ref[...], wo_ref[:SSM_WIDTH, :]) + _dot(yl_ref[...], wo_ref[SSM_WIDTH:, :]))
    h_ref[...] = h
    hn = _rms(h, nw_ref[...])
    _store_row_tiles(hn_ref, hn)
    logit_scr[...] = _dot(hn.astype(BF16), wr_ref[...]) + br_ref[...]

    lane = lax.broadcasted_iota(jnp.int32, (tm, LANES), 1)
    member = jnp.zeros((tm, LANES), F32)
    tops, idxs, sels = [], [], []
    for _ in range(TOP_K):
        m = jnp.max(vals, axis=-1, keepdims=True)
        idx = jnp.min(jnp.where(vals == m, lane, LANES), axis=-1, keepdims=True)
        sel = lane == idx
        vals = jnp.where(sel, NEG_INF, vals)
        member = member + sel.astype(F32)
        tops.append(m)
        idxs.append(idx)
        sels.append(sel)
    es = [jnp.exp(t - tops[0]) for t in tops]
    inv = 1.0 / (es[0] + es[1] + es[2] + es[3])
    member = member * routed

    r = lax.broadcasted_iota(jnp.int32, (tm, tm), 0)
    c = lax.broadcasted_iota(jnp.int32, (tm, tm), 1)
    strict = (r > c).astype(F32).astype(BF16)
    carry = cnt_scr[0:1, :]
    rank_all = _dot(strict, member.astype(BF16)) + carry
    total = carry + jnp.sum(member, axis=0, keepdims=True)
    cnt_scr[...] = jnp.broadcast_to(total, cnt_scr.shape)
    cnt_ref[...] = jnp.broadcast_to(total, cnt_ref.shape)

    meta = jnp.zeros((tm, LANES), F32)
    for kk in range(TOP_K):
        rank = jnp.sum(jnp.where(sels[kk], rank_all, 0.0), axis=-1, keepdims=True)
        meta = jnp.where(lane == META_IDX + kk, idxs[kk].astype(F32), meta)
        meta = jnp.where(lane == META_GATE + kk, es[kk] * inv, meta)
        meta = jnp.where(lane == META_RANK + kk, rank, meta)
    meta_ref[...] = meta
    meta_t_ref[...] = meta.T[:META_ROWS, :]


def _outproj(ys, yl, x2, wo, nw, wr, br):
    t = x2.shape[0]
    tm = min(ROW_TILE, t)
    n = t // tm
    projected = lambda i: (jnp.minimum(i, n - 1), 0)
    routed = lambda i: (jnp.maximum(i - 1, 0), 0)
    tokspec = lambda w: pl.BlockSpec((tm, w), projected)
    const = lambda i: (0, 0)
    return pl.pallas_call(
        _outproj_kernel,
        grid=(n + 1,),
        in_specs=[tokspec(SSM_WIDTH), tokspec(LSTM_WIDTH), tokspec(D_MODEL),
                  pl.BlockSpec((SSM_WIDTH + LSTM_WIDTH, D_MODEL), const, pipeline_mode=pl.Buffered(1)),
                  pl.BlockSpec((1, D_MODEL), const),
                  pl.BlockSpec((D_MODEL, LANES), const),
                  pl.BlockSpec((1, LANES), const)],
        out_specs=[tokspec(D_MODEL), pl.BlockSpec((tm * ROW_SUBTILES, LANES), projected),
                   pl.BlockSpec((tm, LANES), routed),
                   pl.BlockSpec((META_ROWS, tm), lambda i: (0, jnp.maximum(i - 1, 0))),
                   pl.BlockSpec((SUBLANES, LANES), const)],
        out_shape=[jax.ShapeDtypeStruct((t, D_MODEL), F32), jax.ShapeDtypeStruct((t * ROW_SUBTILES, LANES), F32),
                   jax.ShapeDtypeStruct((t, LANES), F32), jax.ShapeDtypeStruct((META_ROWS, t), F32),
                   jax.ShapeDtypeStruct((SUBLANES, LANES), F32)],
        scratch_shapes=[pltpu.VMEM((SUBLANES, LANES), F32), pltpu.VMEM((tm, LANES), F32)],
        compiler_params=_cparams(("arbitrary",)),
        name="outproj_router",
    )(ys, yl, x2, wo, nw, wr, br)


DISPATCH_TILE = 1024
DISPATCH_GROUP = 4
PAD_CHUNKS = (128, 64, 32, 16, 8, 4, 2, 1)


def _dispatch_kernel(padstart_ref, padlen_ref, misc_ref, pos_ref, hn_ref, xs_hbm, sem, zsem, zeros_scr):
    i = pl.program_id(0)
    td = pos_ref.shape[-1] // TOP_K

    @pl.when(i == 0)
    def _():
        zeros_scr[...] = jnp.zeros_like(zeros_scr)
        for e in range(N_EXPERTS):
            row = padstart_ref[e]
            nrow = padlen_ref[e]
            for chunk in PAD_CHUNKS:
                @pl.when((nrow & chunk) != 0)
                def _(row=row, chunk=chunk):
                    pltpu.make_async_copy(
                        zeros_scr.at[pl.ds(0, chunk * ROW_SUBTILES), :],
                        xs_hbm.at[pl.ds(pl.multiple_of(row * ROW_SUBTILES, ROW_SUBTILES), chunk * ROW_SUBTILES), :],
                        zsem).start()
                row = row + (nrow & chunk)

        def tail_copy(blk):
            return pltpu.make_async_copy(
                zeros_scr,
                xs_hbm.at[pl.ds(pl.multiple_of(blk * (MOE_BLOCK * ROW_SUBTILES), ROW_SUBTILES),
                                MOE_BLOCK * ROW_SUBTILES), :], zsem)

        n_used, n_blocks = misc_ref[1], misc_ref[2]

        def start_tail(blk, carry):
            tail_copy(blk).start()
            return carry
        lax.fori_loop(n_used, n_blocks, start_tail, 0)

        def drain_row(j, carry):
            _wait_row_copies(zeros_scr, xs_hbm, 1, zsem)
            return carry
        lax.fori_loop(0, misc_ref[0], drain_row, 0)

        def drain_tail(blk, carry):
            tail_copy(blk).wait()
            return carry
        lax.fori_loop(n_used, n_blocks, drain_tail, 0)

    def body(g, carry):
        r0 = g * DISPATCH_GROUP
        slots = [pos_ref[0, 0, (j % TOP_K) * td + r0 + j // TOP_K] for j in range(DISPATCH_GROUP * TOP_K)]
        for j, slot in enumerate(slots):
            _row_copy(hn_ref, r0 + j // TOP_K, xs_hbm, slot, sem).start(priority=j % DMA_PRIORITIES)
        return carry
    lax.fori_loop(0, td // DISPATCH_GROUP, body, 0)

    for _ in range(TOP_K):
        _wait_row_copies(hn_ref, xs_hbm, td, sem)


def _tile_slots(pos, tile):
    t = pos.shape[1]
    return pos.reshape(TOP_K, t // tile, tile).transpose(1, 0, 2).reshape(t // tile, 1, TOP_K * tile)


def _dispatch(pad_start, pad_len, misc, pos, hn_rt, n_rows):
    t = pos.shape[1]
    td = min(DISPATCH_TILE, t)
    n = t // td
    grid_spec = pltpu.PrefetchScalarGridSpec(
        num_scalar_prefetch=3,
        grid=(n,),
        in_specs=[pl.BlockSpec((1, 1, TOP_K * td), lambda i, *_: (i, 0, 0), memory_space=pltpu.SMEM),
                  pl.BlockSpec((td * ROW_SUBTILES, LANES), lambda i, *_: (i, 0))],
        out_specs=pl.BlockSpec(memory_space=pl.ANY),
        scratch_shapes=[pltpu.SemaphoreType.DMA(()), pltpu.SemaphoreType.DMA(()),
                        pltpu.VMEM((MOE_BLOCK * ROW_SUBTILES, LANES), F32)],
    )
    return pl.pallas_call(
        _dispatch_kernel,
        grid_spec=grid_spec,
        out_shape=jax.ShapeDtypeStruct((n_rows * ROW_SUBTILES, LANES), F32),
        compiler_params=_cparams(("arbitrary",)),
        name="dispatch",
    )(pad_start, pad_len, misc, _tile_slots(pos, td), hn_rt)


def _experts_kernel(be_ref, nused_ref, next_ref, x_ref, wgu_hbm, bgu_ref, wdn_hbm, bdn_ref, out_ref,
                    wgu_bf, wdn_bf, wgu_st, wdn_st, wsem, slot_ref):
    i = pl.program_id(0)
    n_used = nused_ref[0]

    def weight_copies(e, slot):
        return (pltpu.make_async_copy(wgu_hbm.at[e], wgu_st.at[slot], wsem.at[0, slot]),
                pltpu.make_async_copy(wdn_hbm.at[e], wdn_st.at[slot], wsem.at[1, slot]))

    @pl.when(i == 0)
    def _():
        slot_ref[0] = 0
        for cp in weight_copies(be_ref[0], 0):
            cp.start()

    @pl.when(i < n_used)
    def _():
        @pl.when(jnp.logical_or(i == 0, be_ref[i] != be_ref[jnp.maximum(i - 1, 0)]))
        def _():
            e = be_ref[i]
            slot = slot_ref[0]
            for cp in weight_copies(e, slot):
                cp.wait()
            wgu_bf[...] = wgu_st[slot].astype(BF16)
            wdn_bf[...] = wdn_st[slot].astype(BF16)
            nxt = next_ref[e]

            @pl.when(nxt >= 0)
            def _():
                for cp in weight_copies(nxt, 1 - slot):
                    cp.start()
            slot_ref[0] = 1 - slot

        xb = jnp.concatenate(_load_row_tile_cols(x_ref, MOE_BLOCK), axis=1).astype(BF16)
        hh = _dot(xb, wgu_bf[...]) + bgu_ref[...]
        gh = jnp.minimum(hh[:, :D_FF], SWIGLU_LIMIT)
        uh = jnp.clip(hh[:, D_FF:], -SWIGLU_LIMIT, SWIGLU_LIMIT)
        act = (uh + 1.0) * (gh * _sigmoid(SWIGLU_ALPHA * gh))
        _store_row_tiles(out_ref, _dot(act.astype(BF16), wdn_bf[...]) + bdn_ref[...])

    @pl.when(i >= n_used)
    def _():
        out_ref[...] = jnp.zeros_like(out_ref)


def _experts(block_expert, n_used, next_expert, xs_rt, wgu, bgu3, wdn, bdn3):
    n_blocks = block_expert.shape[0]
    blk = MOE_BLOCK * ROW_SUBTILES
    grid_spec = pltpu.PrefetchScalarGridSpec(
        num_scalar_prefetch=3,
        grid=(n_blocks,),
        in_specs=[
            pl.BlockSpec((blk, LANES), lambda i, be, nu, nx: (i, 0)),
            pl.BlockSpec(memory_space=pl.ANY),
            pl.BlockSpec((None, 1, 2 * D_FF), lambda i, be, nu, nx: (be[i], 0, 0)),
            pl.BlockSpec(memory_space=pl.ANY),
            pl.BlockSpec((None, 1, D_MODEL), lambda i, be, nu, nx: (be[i], 0, 0)),
        ],
        out_specs=pl.BlockSpec((blk, LANES), lambda i, be, nu, nx: (i, 0)),
        scratch_shapes=[pltpu.VMEM((D_MODEL, 2 * D_FF), BF16),
                        pltpu.VMEM((D_FF, D_MODEL), BF16),
                        pltpu.VMEM((2, D_MODEL, 2 * D_FF), F32),
                        pltpu.VMEM((2, D_FF, D_MODEL), F32),
                        pltpu.SemaphoreType.DMA((2, 2)),
                        pltpu.SMEM((1,), jnp.int32)],
    )
    return pl.pallas_call(
        _experts_kernel,
        grid_spec=grid_spec,
        out_shape=jax.ShapeDtypeStruct((n_blocks * blk, LANES), F32),
        compiler_params=_cparams(("arbitrary",)),
        name="experts",
    )(block_expert, n_used, next_expert, xs_rt, wgu, bgu3, wdn, bdn3)


COMBINE_GROUP = 16


def _combine_kernel(pos_cur, pos_next, outs_hbm, h_ref, meta_ref, nw_ref, y_ref, buf, sem):
    i = pl.program_id(0)
    n = pl.num_programs(0)
    tc = h_ref.shape[0]
    slot = i % 2

    def issue(pos_ref, s):
        def body(g, carry):
            r0 = pl.multiple_of(g * COMBINE_GROUP, COMBINE_GROUP)
            for j in range(COMBINE_GROUP):
                for kk in range(TOP_K):
                    _row_copy(outs_hbm, pos_ref[0, 0, kk * tc + r0 + j], buf.at[s, kk], r0 + j,
                              sem.at[s]).start(priority=kk % DMA_PRIORITIES)
            return carry
        lax.fori_loop(0, tc // COMBINE_GROUP, body, 0)

    @pl.when(i == 0)
    def _():
        issue(pos_cur, 0)

    @pl.when(i + 1 < n)
    def _():
        issue(pos_next, 1 - slot)

    for kk in range(TOP_K):
        _wait_row_copies(outs_hbm, buf.at[slot, kk], tc, sem.at[slot])
    meta = meta_ref[...]
    gates = [meta[:, META_GATE + kk:META_GATE + kk + 1] for kk in range(TOP_K)]
    rows = [_load_row_tile_cols(buf.at[slot, kk], tc) for kk in range(TOP_K)]
    cols = []
    for s in range(ROW_SUBTILES):
        acc = h_ref[:, s * LANES:(s + 1) * LANES]
        for kk in range(TOP_K):
            acc = acc + gates[kk] * rows[kk][s]
        cols.append(acc)
    y_ref[...] = _rms(jnp.concatenate(cols, axis=1), nw_ref[...])


def _combine(pos, outs_rt, h, meta, nw):
    t = h.shape[0]
    tc = min(COMBINE_TILE, t)
    n = t // tc
    pos3 = _tile_slots(pos, tc)
    return pl.pallas_call(
        _combine_kernel,
        grid=(n,),
        in_specs=[
            pl.BlockSpec((1, 1, TOP_K * tc), lambda i: (i, 0, 0), memory_space=pltpu.SMEM),
            pl.BlockSpec((1, 1, TOP_K * tc), lambda i: (jnp.minimum(i + 1, n - 1), 0, 0),
                         memory_space=pltpu.SMEM),
            pl.BlockSpec(memory_space=pl.ANY),
            pl.BlockSpec((tc, D_MODEL), lambda i: (i, 0)),
            pl.BlockSpec((tc, LANES), lambda i: (i, 0)),
            pl.BlockSpec((1, D_MODEL), lambda i: (0, 0)),
        ],
        out_specs=pl.BlockSpec((tc, D_MODEL), lambda i: (i, 0)),
        out_shape=jax.ShapeDtypeStruct((t, D_MODEL), F32),
        scratch_shapes=[pltpu.VMEM((2, TOP_K, tc * ROW_SUBTILES, LANES), F32), pltpu.SemaphoreType.DMA((2,))],
        compiler_params=_cparams(("arbitrary",)),
        name="combine",
    )(pos3, pos3, outs_rt, h, meta, nw)


def _pad_lanes(v, fill=0.0):
    v = v.astype(F32).reshape(1, -1)
    return jnp.pad(v, ((0, 0), (0, LANES - v.shape[1])), constant_values=fill)


def kernel(x, norm_mix_w, w_in, conv_w, conv_b, dt_bias, a_log, d_skip, ssm_norm_w, lstm_i_bias,
           lstm_f_bias, lstm_norm_w, w_out, norm_ffn_w, w_router, b_router, w_gate_up, b_gate_up,
           w_down, b_down, norm_final_w):
    b, s, d = x.shape
    t = b * s
    x2 = x.reshape(t, d).astype(F32)
    depth = w_in.shape[0]
    assert depth == 1, "the combine kernel fuses the final norm, so exactly one layer is supported"
    for layer in range(depth):
        w_all = _regroup_weights(w_in.astype(F32), layer)
        sel = (jnp.arange(LANES)[:, None] == (jnp.arange(SSM_WIDTH) // SSM_HEAD_DIM)[None, :]).astype(BF16)
        a_neg = _pad_lanes(-jnp.exp(a_log[layer].astype(F32)))
        dskip_x = jnp.repeat(d_skip[layer].astype(F32), SSM_HEAD_DIM).reshape(1, SSM_WIDTH)
        wr = jnp.pad(w_router[layer], ((0, 0), (0, LANES - N_EXPERTS))).astype(BF16)
        br = _pad_lanes(b_router[layer], fill=NEG_INF)

        z, xbc, q, k, v, o, gates = _inproj(x2, norm_mix_w[layer].reshape(1, d).astype(F32), w_all)
        y_ssd, y_lstm = _mixers(
            xbc, z, gates, conv_w[layer].astype(F32), conv_b[layer].reshape(1, -1).astype(F32),
            _pad_lanes(dt_bias[layer]), a_neg, dskip_x, ssm_norm_w[layer].reshape(1, -1).astype(F32), sel,
            q, k, v, o, _pad_lanes(lstm_i_bias[layer]), _pad_lanes(lstm_f_bias[layer]),
            lstm_norm_w[layer].reshape(1, -1).astype(F32), b, s)
        h, hn_rt, meta, meta_t, cnt = _outproj(y_ssd, y_lstm, x2, w_out[layer].astype(BF16),
                                       norm_ffn_w[layer].reshape(1, d).astype(F32), wr, br)

        idx = meta_t[META_IDX:META_IDX + TOP_K].astype(jnp.int32)
        rank = meta_t[META_RANK:META_RANK + TOP_K].astype(jnp.int32)
        counts = cnt[0, :N_EXPERTS].astype(jnp.int32)
        n_blocks = -(-(t * TOP_K) // MOE_BLOCK) + N_EXPERTS
        padded = (counts + MOE_BLOCK - 1) // MOE_BLOCK * MOE_BLOCK
        padded_ends = jnp.cumsum(padded)
        padded_starts = padded_ends - padded
        onehot = idx[..., None] == jnp.arange(N_EXPERTS, dtype=jnp.int32)
        pos = jnp.sum(jnp.where(onehot, padded_starts, 0), axis=-1) + rank
        block_start = jnp.arange(n_blocks, dtype=jnp.int32) * MOE_BLOCK
        block_expert = jnp.minimum(
            jnp.sum((padded_ends[None, :] <= block_start[:, None]).astype(jnp.int32), axis=1), N_EXPERTS - 1)
        n_used = padded_ends[-1:] // MOE_BLOCK
        pad_len = padded - counts
        misc = jnp.concatenate([jnp.sum(pad_len, keepdims=True), n_used,
                                jnp.full((1,), n_blocks, jnp.int32)])

        xs_rt = _dispatch(padded_starts + counts, pad_len, misc, pos, hn_rt, n_blocks * MOE_BLOCK)
        eids = jnp.arange(N_EXPERTS, dtype=jnp.int32)
        later_nonempty = (eids[None, :] > eids[:, None]) & (counts[None, :] > 0)
        next_expert = jnp.min(jnp.where(later_nonempty, eids[None, :], N_EXPERTS), axis=1)
        next_expert = jnp.where(next_expert < N_EXPERTS, next_expert, -1)
        outs_rt = _experts(block_expert, n_used, next_expert, xs_rt, w_gate_up[layer],
                           b_gate_up[layer].reshape(N_EXPERTS, 1, -1), w_down[layer],
                           b_down[layer].reshape(N_EXPERTS, 1, -1))
        x2 = _combine(pos, outs_rt, h, meta, norm_final_w.reshape(1, d).astype(F32))
    return x2.reshape(b, s, d).astype(x.dtype)
```

```python
import jax
import jax.numpy as jnp
import numpy as np
from jax import lax
from jax.experimental import pallas as pl
from jax.experimental.pallas import tpu as pltpu

F32 = jnp.float32
BF16 = jnp.bfloat16

D_MODEL = 1024
SSM_WIDTH = 1024
SSM_HEAD_DIM = 64
SSM_HEADS = 16
SSM_GROUPS = 2
SSM_STATE = 128
CONV_WIDTH = 4
CONV_DIM = SSM_WIDTH + 2 * SSM_GROUPS * SSM_STATE
LSTM_WIDTH = 1024
LSTM_HEAD_DIM = 128
LSTM_HEADS = 8
N_EXPERTS = 32
TOP_K = 4
D_FF = 1024
SWIGLU_LIMIT = 7.0
SWIGLU_ALPHA = 1.702
MOE_BLOCK = 256
RMS_EPS = 1e-6

LANES = 128
SUBLANES = 8
MIX_CHUNK = 256
ROW_TILE = 512
COMBINE_TILE = 512
VMEM_LIMIT = 56 * 1024 * 1024

GATE_COLS = LANES
GATE_LIVE = SSM_HEADS + 2 * LSTM_HEADS
NEG_INF = float("-inf")


def _cparams(sem):
    return pltpu.CompilerParams(dimension_semantics=sem, vmem_limit_bytes=VMEM_LIMIT)


def _rms(x, w):
    return x * lax.rsqrt(jnp.mean(x * x, axis=-1, keepdims=True) + RMS_EPS) * w


LOG2E = 1.4426950408889634


def _sigmoid(x):
    return 1.0 / (1.0 + jnp.exp2(x * -LOG2E))


def _softplus(x):
    return jnp.maximum(x, 0.0) + jnp.log(1.0 + jnp.exp(-jnp.abs(x)))


def _split3(a):
    hi = a.astype(BF16)
    r = a - hi.astype(F32)
    mid = r.astype(BF16)
    lo = (r - mid.astype(F32)).astype(BF16)
    return hi, mid, lo


def _dot(a, b):
    return jnp.dot(a, b, preferred_element_type=F32)


def _dot_nt(a, b):
    return lax.dot_general(a, b, (((1,), (1,)), ((), ())), preferred_element_type=F32)


def _dot_tn(a, b):
    return lax.dot_general(a, b, (((0,), (0,)), ((), ())), preferred_element_type=F32)


def _sel_dot(sel_bf, a):
    hi, mid, lo = _split3(a)
    return _dot(sel_bf, hi) + _dot(sel_bf, mid) + _dot(sel_bf, lo)


def _expand(a, sel_bf, terms=3):
    out = None
    for piece in _split3(a)[:terms]:
        d = _dot(piece, sel_bf)
        out = d if out is None else out + d
    return out


ROW_SUBTILES = D_MODEL // LANES
assert ROW_SUBTILES == SUBLANES, "a model row must fill exactly one (8, 128) f32 tile"
DMA_PRIORITIES = 2


def _store_row_tiles(ref, x):
    n = x.shape[0]
    for s in range(ROW_SUBTILES):
        ref[pl.ds(s, n, stride=ROW_SUBTILES), :] = x[:, s * LANES:(s + 1) * LANES]


def _load_row_tile_cols(ref, n):
    return [ref[pl.ds(s, n, stride=ROW_SUBTILES), :] for s in range(ROW_SUBTILES)]


def _row_copy(src, src_row, dst, dst_row, sem):
    return pltpu.make_async_copy(
        src.at[pl.ds(pl.multiple_of(src_row * ROW_SUBTILES, ROW_SUBTILES), ROW_SUBTILES), :],
        dst.at[pl.ds(pl.multiple_of(dst_row * ROW_SUBTILES, ROW_SUBTILES), ROW_SUBTILES), :], sem)


def _wait_row_copies(src, dst, n_rows, sem):
    size = n_rows * ROW_SUBTILES
    pltpu.make_async_copy(src.at[pl.ds(0, size), :], dst.at[pl.ds(0, size), :], sem).wait()


_INPROJ_WIDTHS = (SSM_WIDTH, CONV_DIM, LSTM_WIDTH, LSTM_WIDTH, LSTM_WIDTH, LSTM_WIDTH, GATE_COLS)
_C_XBC_END = SSM_WIDTH + CONV_DIM
_C_DT_END = _C_XBC_END + SSM_HEADS
_C_O_END = _C_DT_END + 4 * LSTM_WIDTH
_C_I_END = _C_O_END + LSTM_HEADS
IN_PROJ_DIM = _C_I_END + LSTM_HEADS
REGROUP_ROWS = 128


def _regroup_kernel(w_ref, o_ref):
    rows = w_ref.shape[0]
    big = _C_XBC_END + 4 * LSTM_WIDTH
    o_ref[:, :_C_XBC_END] = w_ref[:, :_C_XBC_END].astype(BF16)
    o_ref[:, _C_XBC_END:big] = w_ref[:, _C_DT_END:_C_O_END].astype(BF16)
    gate_w = jnp.concatenate([w_ref[:, _C_XBC_END:_C_DT_END], w_ref[:, _C_O_END:IN_PROJ_DIM],
                              jnp.zeros((rows, GATE_COLS - GATE_LIVE), F32)], axis=1)
    o_ref[:, big:] = gate_w.astype(BF16)


def _regroup_weights(w_in, layer):
    k = w_in.shape[1]
    ncol = sum(_INPROJ_WIDTHS)
    return pl.pallas_call(
        _regroup_kernel,
        grid=(k // REGROUP_ROWS,),
        in_specs=[pl.BlockSpec((None, REGROUP_ROWS, IN_PROJ_DIM), lambda i: (layer, i, 0))],
        out_specs=pl.BlockSpec((REGROUP_ROWS, ncol), lambda i: (i, 0)),
        out_shape=jax.ShapeDtypeStruct((k, ncol), BF16),
        compiler_params=_cparams(("arbitrary",)),
        name="regroup_weights",
    )(w_in)


def _inproj_kernel(x_ref, nw_ref, w_ref, z_ref, xbc_ref, q_ref, k_ref, v_ref, o_ref, g_ref):
    xb = _rms(x_ref[...], nw_ref[...]).astype(BF16)
    off = 0
    for ref, width in zip((z_ref, xbc_ref, q_ref, k_ref, v_ref, o_ref, g_ref), _INPROJ_WIDTHS):
        ref[...] = _dot(xb, w_ref[:, off:off + width]).astype(ref.dtype)
        off += width


def _inproj(x2, nw, w_all):
    t = x2.shape[0]
    tm = min(ROW_TILE, t)
    ncol = w_all.shape[1]
    out_shape = [jax.ShapeDtypeStruct((t, w), BF16) for w in _INPROJ_WIDTHS[:-1]]
    out_shape.append(jax.ShapeDtypeStruct((t, GATE_COLS), F32))
    return pl.pallas_call(
        _inproj_kernel,
        grid=(t // tm,),
        in_specs=[
            pl.BlockSpec((tm, D_MODEL), lambda i: (i, 0)),
            pl.BlockSpec((1, D_MODEL), lambda i: (0, 0)),
            pl.BlockSpec((D_MODEL, ncol), lambda i: (0, 0), pipeline_mode=pl.Buffered(1)),
        ],
        out_specs=[pl.BlockSpec((tm, w), lambda i: (i, 0)) for w in _INPROJ_WIDTHS],
        out_shape=out_shape,
        compiler_params=_cparams(("arbitrary",)),
        name="inproj",
    )(x2, nw, w_all)


SSD_PAIR_GROUP = 2
LSTM_HEAD_GROUP = 2


def _ssd_phases(xbc_ref, z_ref, g_ref, convw_ref, convb_ref, dtb_ref, aneg_ref, dskip_ref, nw_ref,
                sel_ref, y_ref, u_scr, st_scr):
    L = xbc_ref.shape[0]
    gw = SSM_WIDTH // SSM_GROUPS

    row = lax.broadcasted_iota(jnp.int32, (L, L), 0)
    col = lax.broadcasted_iota(jnp.int32, (L, L), 1)
    causal = row >= col

    xin_bf = xbc_ref[...]
    xin = xin_bf.astype(F32)
    halo = u_scr[...]
    row8 = lax.broadcasted_iota(jnp.int32, (SUBLANES, 1), 0)
    acc = convb_ref[...] + convw_ref[CONV_WIDTH - 1:CONV_WIDTH, :] * xin
    for back in range(1, CONV_WIDTH):
        shifted = _dot((row - col == back).astype(F32).astype(BF16), xin_bf)
        head = shifted[:SUBLANES] + jnp.where(row8 < back, pltpu.roll(halo, back, axis=0), 0.0)
        shifted = jnp.concatenate([head, shifted[SUBLANES:]], axis=0)
        acc = acc + convw_ref[CONV_WIDTH - 1 - back:CONV_WIDTH - back, :] * shifted
    u_scr[...] = xin[L - SUBLANES:, :]
    xbc = acc * _sigmoid(acc)
    xs = xbc[:, :SSM_WIDTH]
    bm = xbc[:, SSM_WIDTH:SSM_WIDTH + SSM_GROUPS * SSM_STATE].astype(BF16)
    cm = xbc[:, SSM_WIDTH + SSM_GROUPS * SSM_STATE:].astype(BF16)
    yield

    tril_bf = causal.astype(F32).astype(BF16)

    lane = lax.broadcasted_iota(jnp.int32, (1, LANES), 1)
    dt = jnp.where(lane < SSM_HEADS, _softplus(g_ref[...] + dtb_ref[...]), 0.0)
    acs = _sel_dot(tril_bf, dt * aneg_ref[...]) * LOG2E
    acs_t = acs.T
    sel = sel_ref[...]
    acs_last = acs[L - 1:L, :]
    ea_x = _expand(jnp.exp2(acs), sel, terms=2)
    xd = xs * _expand(dt, sel, terms=1)
    xd_bf = xd.astype(BF16)
    xw = (xd * _expand(jnp.exp2(acs_last - acs), sel, terms=1)).astype(BF16)
    yield

    cbs = [_dot_nt(cm[:, g * SSM_STATE:(g + 1) * SSM_STATE], bm[:, g * SSM_STATE:(g + 1) * SSM_STATE])
           for g in range(SSM_GROUPS)]
    yoff = [_dot(cm[:, g * SSM_STATE:(g + 1) * SSM_STATE], st_scr[g].astype(BF16)) for g in range(SSM_GROUPS)]
    upd = [_dot_tn(bm[:, g * SSM_STATE:(g + 1) * SSM_STATE], xw[:, g * gw:(g + 1) * gw])
           for g in range(SSM_GROUPS)]
    for g in range(SSM_GROUPS):
        st_scr[g] = st_scr[g] * ea_x[L - 1:L, g * gw:(g + 1) * gw] + upd[g]
    yield

    pair_lane = lax.broadcasted_iota(jnp.int32, (1, LANES), 1)
    heads_per_group = SSM_HEADS // SSM_GROUPS
    rhs = []
    for j in range(SSM_HEADS // 2):
        xpair = xd_bf[:, j * LANES:(j + 1) * LANES]
        zero = jnp.zeros_like(xpair)
        rhs.append(jnp.concatenate([jnp.where(pair_lane < SSM_HEAD_DIM, xpair, zero),
                                    jnp.where(pair_lane >= SSM_HEAD_DIM, xpair, zero)], axis=0))
    ydiag = []
    for j0 in range(0, SSM_HEADS // 2, SSD_PAIR_GROUP):
        pairs = range(j0, j0 + SSD_PAIR_GROUP)
        ms = {h: (cbs[h // heads_per_group]
                  * jnp.exp2(jnp.where(causal, acs[:, h:h + 1] - acs_t[h:h + 1, :], NEG_INF))).astype(BF16)
              for j in pairs for h in (2 * j, 2 * j + 1)}
        yield
        ydiag += [_dot(jnp.concatenate([ms[2 * j], ms[2 * j + 1]], axis=1), rhs[j]) for j in pairs]
        yield
    y = jnp.concatenate(ydiag, axis=1) + jnp.concatenate(yoff, axis=1) * ea_x + dskip_ref[...] * xs

    zz = z_ref[...].astype(F32)
    y = y * (zz * _sigmoid(zz))
    y_ref[...] = _rms(y, nw_ref[...]).astype(BF16)


def _mlstm_phases(q_ref, k_ref, v_ref, o_ref, g_ref, ib_ref, fb_ref, nw_ref, y_ref,
                  st_scr, m_scr):
    L = q_ref.shape[0]
    dh = LSTM_HEAD_DIM
    scale = dh ** -0.5

    row = lax.broadcasted_iota(jnp.int32, (L, L), 0)
    col = lax.broadcasted_iota(jnp.int32, (L, L), 1)
    causal = row >= col
    tril_bf = causal.astype(F32).astype(BF16)
    lane = lax.broadcasted_iota(jnp.int32, (1, LANES), 1)
    live = lane < LSTM_HEADS

    gates = g_ref[...]
    gi = pltpu.roll(gates, LANES - SSM_HEADS, axis=1)
    gf = pltpu.roll(gates, LANES - SSM_HEADS - LSTM_HEADS, axis=1)
    ii = jnp.where(live, gi + ib_ref[...], 0.0)
    logf = jnp.where(live, -_softplus(-(gf + fb_ref[...])), 0.0)
    cumf = _sel_dot(tril_bf, logf)
    g = ii - cumf
    rid = lax.broadcasted_iota(jnp.int32, (L, LANES), 0)
    cmx = g
    step = 1
    while step < L:
        cmx = jnp.maximum(cmx, jnp.where(rid >= step, pltpu.roll(cmx, step, axis=0), NEG_INF))
        step *= 2
    m_prev = m_scr[...]
    mx = jnp.maximum(m_prev, cmx)
    w_inter = jnp.exp(m_prev - mx)
    enm = jnp.exp(-(cumf + mx))
    g2_t = (g * LOG2E).T
    mx2 = mx * LOG2E - np.log2(scale)
    m_last = mx[L - 1:L, :]
    wk = jnp.exp(g - m_last) * scale
    sc = jnp.exp(m_prev - m_last)
    m_scr[...] = cumf[L - 1:L, :] + m_last
    yield

    ones_bf = jnp.ones((L, dh), BF16)
    heads = range(LSTM_HEADS)
    hsl = [slice(h * dh, (h + 1) * dh) for h in heads]
    vaug = [jnp.concatenate([v_ref[:, hsl[h]], ones_bf], axis=1) for h in heads]
    r2 = [_dot(q_ref[:, hsl[h]], st_scr[h].astype(BF16)) for h in heads]
    upd = [_dot_tn((k_ref[:, hsl[h]].astype(F32) * wk[:, h:h + 1]).astype(BF16), vaug[h]) for h in heads]
    for h in heads:
        st_scr[h] = st_scr[h] * sc[:, h:h + 1] + upd[h]
    yield
    r1 = []
    for h0 in range(0, LSTM_HEADS, LSTM_HEAD_GROUP):
        group = range(h0, h0 + LSTM_HEAD_GROUP)
        s_qk = {h: _dot_nt(q_ref[:, hsl[h]], k_ref[:, hsl[h]]) for h in group}
        yield
        p = {h: (s_qk[h] * jnp.exp2(jnp.where(causal, g2_t[h:h + 1, :] - mx2[:, h:h + 1], NEG_INF))).astype(BF16)
             for h in group}
        yield
        r1 += [_dot(p[h], vaug[h]) for h in group]
        yield
    hh = []
    for h in heads:
        wcol = w_inter[:, h:h + 1]
        num = r1[h][:, :dh] + r2[h][:, :dh] * wcol
        den = r1[h][:, dh:] + r2[h][:, dh:] * wcol
        hh.append(num / jnp.maximum(jnp.abs(den), enm[:, h:h + 1]))
    inv = [lax.rsqrt(jnp.mean(hh[h] * hh[h], axis=-1, keepdims=True) + RMS_EPS) for h in heads]
    for h in heads:
        oo = o_ref[:, hsl[h]].astype(F32)
        y_ref[:, hsl[h]] = (_sigmoid(oo) * (hh[h] * inv[h] * nw_ref[:, hsl[h]])).astype(BF16)


N_SSD_IN, N_LSTM_IN = 10, 8


def _mixer_kernel(*refs):
    ssd_in = refs[:N_SSD_IN]
    lstm_in = refs[N_SSD_IN:N_SSD_IN + N_LSTM_IN]
    y_ssd, y_lstm, u_scr, st_ssd, st_lstm, m_scr = refs[N_SSD_IN + N_LSTM_IN:]

    @pl.when(pl.program_id(1) == 0)
    def _():
        u_scr[...] = jnp.zeros_like(u_scr)
        st_ssd[...] = jnp.zeros_like(st_ssd)
        st_lstm[...] = jnp.zeros_like(st_lstm)
        m_scr[...] = jnp.zeros_like(m_scr)

    streams = [_ssd_phases(*ssd_in, y_ssd, u_scr, st_ssd), _mlstm_phases(*lstm_in, y_lstm, st_lstm, m_scr)]
    while streams:
        for stream in list(streams):
            if next(stream, StopIteration) is StopIteration:
                streams.remove(stream)


def _mixers(xbc, z, gates, convw, convb, dtb, aneg, dskip_x, nw_ssd, sel, q, k, v, o, ib, fb, nw_lstm, b, s):
    L = min(MIX_CHUNK, s)
    nc = s // L
    tok = lambda bi, ci: (bi * nc + ci, 0)
    const = lambda bi, ci: (0, 0)
    big = pl.BlockSpec((L, LSTM_WIDTH), tok)
    gate_block = lambda j: pl.BlockSpec((L, LANES), lambda bi, ci: (bi * nc + ci, j))
    vec = lambda w: pl.BlockSpec((1, w), const)
    in_specs = [pl.BlockSpec((L, CONV_DIM), tok), pl.BlockSpec((L, SSM_WIDTH), tok), gate_block(0),
                pl.BlockSpec((CONV_WIDTH, CONV_DIM), const), vec(CONV_DIM), vec(LANES), vec(LANES),
                vec(SSM_WIDTH), vec(SSM_WIDTH), pl.BlockSpec((LANES, SSM_WIDTH), const),
                big, big, big, big, gate_block(0), vec(LANES), vec(LANES), vec(LSTM_WIDTH)]
    assert len(in_specs) == N_SSD_IN + N_LSTM_IN
    return pl.pallas_call(
        _mixer_kernel,
        grid=(b, nc),
        in_specs=in_specs,
        out_specs=[pl.BlockSpec((L, SSM_WIDTH), tok), big],
        out_shape=[jax.ShapeDtypeStruct((b * s, SSM_WIDTH), BF16), jax.ShapeDtypeStruct((b * s, LSTM_WIDTH), BF16)],
        scratch_shapes=[pltpu.VMEM((SUBLANES, CONV_DIM), F32),
                        pltpu.VMEM((SSM_GROUPS, SSM_STATE, SSM_WIDTH // SSM_GROUPS), F32),
                        pltpu.VMEM((LSTM_HEADS, LSTM_HEAD_DIM, 2 * LSTM_HEAD_DIM), F32),
                        pltpu.VMEM((1, LANES), F32)],
        compiler_params=_cparams(("arbitrary", "arbitrary")),
        name="mixers",
    )(xbc, z, gates, convw, convb, dtb, aneg, dskip_x, nw_ssd, sel, q, k, v, o, gates, ib, fb, nw_lstm)


META_IDX, META_GATE, META_RANK = 0, TOP_K, 2 * TOP_K
META_ROWS = 2 * SUBLANES


def _outproj_kernel(ys_ref, yl_ref, x_ref, wo_ref, nw_ref, wr_ref, br_ref, h_ref, hn_ref, meta_ref,
                    meta_t_ref, cnt_ref, cnt_scr, logit_scr):
    tm = x_ref.shape[0]
    step = pl.program_id(0)

    @pl.when(step == 0)
    def _():
        cnt_scr[...] = jnp.zeros_like(cnt_scr)
        logit_scr[...] = jnp.zeros_like(logit_scr)

    vals = logit_scr[...]
    routed = (step > 0).astype(F32)

    h = (x_ref[...] + _dot(ys_ref[...], wo_ref[:SSM_WIDTH, :]) + _dot(yl_ref[...], wo_ref[SSM_WIDTH:, :]))
    h_ref[...] = h
    hn = _rms(h, nw_ref[...])
    _store_row_tiles(hn_ref, hn)
    logit_scr[...] = _dot(hn.astype(BF16), wr_ref[...]) + br_ref[...]

    lane = lax.broadcasted_iota(jnp.int32, (tm, LANES), 1)
    member = jnp.zeros((tm, LANES), F32)
    tops, idxs, sels = [], [], []
    for _ in range(TOP_K):
        m = jnp.max(vals, axis=-1, keepdims=True)
        idx = jnp.min(jnp.where(vals == m, lane, LANES), axis=-1, keepdims=True)
        sel = lane == idx
        vals = jnp.where(sel, NEG_INF, vals)
        member = member + sel.astype(F32)
        tops.append(m)
        idxs.append(idx)
        sels.append(sel)
    es = [jnp.exp(t - tops[0]) for t in tops]
    inv = 1.0 / (es[0] + es[1] + es[2] + es[3])
    member = member * routed

    r = lax.broadcasted_iota(jnp.int32, (tm, tm), 0)
    c = lax.broadcasted_iota(jnp.int32, (tm, tm), 1)
    strict = (r > c).astype(F32).astype(BF16)
    carry = cnt_scr[0:1, :]
    rank_all = _dot(strict, member.astype(BF16)) + carry
    total = carry + jnp.sum(member, axis=0, keepdims=True)
    cnt_scr[...] = jnp.broadcast_to(total, cnt_scr.shape)
    cnt_ref[...] = jnp.broadcast_to(total, cnt_ref.shape)

    meta = jnp.zeros((tm, LANES), F32)
    for kk in range(TOP_K):
        rank = jnp.sum(jnp.where(sels[kk], rank_all, 0.0), axis=-1, keepdims=True)
        meta = jnp.where(lane == META_IDX + kk, idxs[kk].astype(F32), meta)
        meta = jnp.where(lane == META_GATE + kk, es[kk] * inv, meta)
        meta = jnp.where(lane == META_RANK + kk, rank, meta)
    meta_ref[...] = meta
    eye = (lax.broadcasted_iota(jnp.int32, (META_ROWS, LANES), 0)
           == lax.broadcasted_iota(jnp.int32, (META_ROWS, LANES), 1)).astype(F32).astype(BF16)
    pieces = _split3(meta)
    meta_t_ref[...] = _dot_nt(eye, pieces[0]) + _dot_nt(eye, pieces[1]) + _dot_nt(eye, pieces[2])


def _outproj(ys, yl, x2, wo, nw, wr, br):
    t = x2.shape[0]
    tm = min(ROW_TILE, t)
    n = t // tm
    projected = lambda i: (jnp.minimum(i, n - 1), 0)
    routed = lambda i: (jnp.maximum(i - 1, 0), 0)
    tokspec = lambda w: pl.BlockSpec((tm, w), projected)
    const = lambda i: (0, 0)
    return pl.pallas_call(
        _outproj_kernel,
        grid=(n + 1,),
        in_specs=[tokspec(SSM_WIDTH), tokspec(LSTM_WIDTH), tokspec(D_MODEL),
                  pl.BlockSpec((SSM_WIDTH + LSTM_WIDTH, D_MODEL), const, pipeline_mode=pl.Buffered(1)),
                  pl.BlockSpec((1, D_MODEL), const),
                  pl.BlockSpec((D_MODEL, LANES), const),
                  pl.BlockSpec((1, LANES), const)],
        out_specs=[tokspec(D_MODEL), pl.BlockSpec((tm * ROW_SUBTILES, LANES), projected),
                   pl.BlockSpec((tm, LANES), routed),
                   pl.BlockSpec((META_ROWS, tm), lambda i: (0, jnp.maximum(i - 1, 0))),
                   pl.BlockSpec((SUBLANES, LANES), const)],
        out_shape=[jax.ShapeDtypeStruct((t, D_MODEL), F32), jax.ShapeDtypeStruct((t * ROW_SUBTILES, LANES), F32),
                   jax.ShapeDtypeStruct((t, LANES), F32), jax.ShapeDtypeStruct((META_ROWS, t), F32),
                   jax.ShapeDtypeStruct((SUBLANES, LANES), F32)],
        scratch_shapes=[pltpu.VMEM((SUBLANES, LANES), F32), pltpu.VMEM((tm, LANES), F32)],
        compiler_params=_cparams(("arbitrary",)),
        name="outproj_router",
    )(ys, yl, x2, wo, nw, wr, br)


DISPATCH_TILE = 1024
DISPATCH_GROUP = 4
PAD_CHUNKS = (128, 64, 32, 16, 8, 4, 2, 1)


def _dispatch_kernel(padstart_ref, padlen_ref, misc_ref, pos_ref, hn_ref, xs_hbm, sem, zsem, zeros_scr):
    i = pl.program_id(0)
    td = pos_ref.shape[-1] // TOP_K

    @pl.when(i == 0)
    def _():
        zeros_scr[...] = jnp.zeros_like(zeros_scr)
        for e in range(N_EXPERTS):
            row = padstart_ref[e]
            nrow = padlen_ref[e]
            for chunk in PAD_CHUNKS:
                @pl.when((nrow & chunk) != 0)
                def _(row=row, chunk=chunk):
                    pltpu.make_async_copy(
                        zeros_scr.at[pl.ds(0, chunk * ROW_SUBTILES), :],
                        xs_hbm.at[pl.ds(pl.multiple_of(row * ROW_SUBTILES, ROW_SUBTILES), chunk * ROW_SUBTILES), :],
                        zsem).start()
                row = row + (nrow & chunk)

        def tail_copy(blk):
            return pltpu.make_async_copy(
                zeros_scr,
                xs_hbm.at[pl.ds(pl.multiple_of(blk * (MOE_BLOCK * ROW_SUBTILES), ROW_SUBTILES),
                                MOE_BLOCK * ROW_SUBTILES), :], zsem)

        n_used, n_blocks = misc_ref[1], misc_ref[2]

        def start_tail(blk, carry):
            tail_copy(blk).start()
            return carry
        lax.fori_loop(n_used, n_blocks, start_tail, 0)

        def drain_row(j, carry):
            _wait_row_copies(zeros_scr, xs_hbm, 1, zsem)
            return carry
        lax.fori_loop(0, misc_ref[0], drain_row, 0)

        def drain_tail(blk, carry):
            tail_copy(blk).wait()
            return carry
        lax.fori_loop(n_used, n_blocks, drain_tail, 0)

    def body(g, carry):
        r0 = g * DISPATCH_GROUP
        slots = [pos_ref[0, 0, (j % TOP_K) * td + r0 + j // TOP_K] for j in range(DISPATCH_GROUP * TOP_K)]
        for j, slot in enumerate(slots):
            _row_copy(hn_ref, r0 + j // TOP_K, xs_hbm, slot, sem).start(priority=j % DMA_PRIORITIES)
        return carry
    lax.fori_loop(0, td // DISPATCH_GROUP, body, 0)

    for _ in range(TOP_K):
        _wait_row_copies(hn_ref, xs_hbm, td, sem)


def _tile_slots(pos, tile):
    t = pos.shape[1]
    return pos.reshape(TOP_K, t // tile, tile).transpose(1, 0, 2).reshape(t // tile, 1, TOP_K * tile)


def _dispatch(pad_start, pad_len, misc, pos, hn_rt, n_rows):
    t = pos.shape[1]
    td = min(DISPATCH_TILE, t)
    n = t // td
    grid_spec = pltpu.PrefetchScalarGridSpec(
        num_scalar_prefetch=3,
        grid=(n,),
        in_specs=[pl.BlockSpec((1, 1, TOP_K * td), lambda i, *_: (i, 0, 0), memory_space=pltpu.SMEM),
                  pl.BlockSpec((td * ROW_SUBTILES, LANES), lambda i, *_: (i, 0))],
        out_specs=pl.BlockSpec(memory_space=pl.ANY),
        scratch_shapes=[pltpu.SemaphoreType.DMA(()), pltpu.SemaphoreType.DMA(()),
                        pltpu.VMEM((MOE_BLOCK * ROW_SUBTILES, LANES), F32)],
    )
    return pl.pallas_call(
        _dispatch_kernel,
        grid_spec=grid_spec,
        out_shape=jax.ShapeDtypeStruct((n_rows * ROW_SUBTILES, LANES), F32),
        compiler_params=_cparams(("arbitrary",)),
        name="dispatch",
    )(pad_start, pad_len, misc, _tile_slots(pos, td), hn_rt)


def _experts_kernel(be_ref, nused_ref, next_ref, x_ref, wgu_hbm, bgu_ref, wdn_hbm, bdn_ref, out_ref,
                    wgu_bf, wdn_bf, wgu_st, wdn_st, wsem, slot_ref):
    i = pl.program_id(0)
    n_used = nused_ref[0]

    def weight_copies(e, slot):
        return (pltpu.make_async_copy(wgu_hbm.at[e], wgu_st.at[slot], wsem.at[0, slot]),
                pltpu.make_async_copy(wdn_hbm.at[e], wdn_st.at[slot], wsem.at[1, slot]))

    @pl.when(i == 0)
    def _():
        slot_ref[0] = 0
        for cp in weight_copies(be_ref[0], 0):
            cp.start()

    @pl.when(i < n_used)
    def _():
        @pl.when(jnp.logical_or(i == 0, be_ref[i] != be_ref[jnp.maximum(i - 1, 0)]))
        def _():
            e = be_ref[i]
            slot = slot_ref[0]
            for cp in weight_copies(e, slot):
                cp.wait()
            wgu_bf[...] = wgu_st[slot].astype(BF16)
            wdn_bf[...] = wdn_st[slot].astype(BF16)
            nxt = next_ref[e]

            @pl.when(nxt >= 0)
            def _():
                for cp in weight_copies(nxt, 1 - slot):
                    cp.start()
            slot_ref[0] = 1 - slot

        xb = jnp.concatenate(_load_row_tile_cols(x_ref, MOE_BLOCK), axis=1).astype(BF16)
        hh = _dot(xb, wgu_bf[...]) + bgu_ref[...]
        gh = jnp.minimum(hh[:, :D_FF], SWIGLU_LIMIT)
        uh = jnp.clip(hh[:, D_FF:], -SWIGLU_LIMIT, SWIGLU_LIMIT)
        act = (uh + 1.0) * (gh * _sigmoid(SWIGLU_ALPHA * gh))
        _store_row_tiles(out_ref, _dot(act.astype(BF16), wdn_bf[...]) + bdn_ref[...])

    @pl.when(i >= n_used)
    def _():
        out_ref[...] = jnp.zeros_like(out_ref)


def _experts(block_expert, n_used, next_expert, xs_rt, wgu, bgu3, wdn, bdn3):
    n_blocks = block_expert.shape[0]
    blk = MOE_BLOCK * ROW_SUBTILES
    grid_spec = pltpu.PrefetchScalarGridSpec(
        num_scalar_prefetch=3,
        grid=(n_blocks,),
        in_specs=[
            pl.BlockSpec((blk, LANES), lambda i, be, nu, nx: (i, 0)),
            pl.BlockSpec(memory_space=pl.ANY),
            pl.BlockSpec((None, 1, 2 * D_FF), lambda i, be, nu, nx: (be[i], 0, 0)),
            pl.BlockSpec(memory_space=pl.ANY),
            pl.BlockSpec((None, 1, D_MODEL), lambda i, be, nu, nx: (be[i], 0, 0)),
        ],
        out_specs=pl.BlockSpec((blk, LANES), lambda i, be, nu, nx: (i, 0)),
        scratch_shapes=[pltpu.VMEM((D_MODEL, 2 * D_FF), BF16),
                        pltpu.VMEM((D_FF, D_MODEL), BF16),
                        pltpu.VMEM((2, D_MODEL, 2 * D_FF), F32),
                        pltpu.VMEM((2, D_FF, D_MODEL), F32),
                        pltpu.SemaphoreType.DMA((2, 2)),
                        pltpu.SMEM((1,), jnp.int32)],
    )
    return pl.pallas_call(
        _experts_kernel,
        grid_spec=grid_spec,
        out_shape=jax.ShapeDtypeStruct((n_blocks * blk, LANES), F32),
        compiler_params=_cparams(("arbitrary",)),
        name="experts",
    )(block_expert, n_used, next_expert, xs_rt, wgu, bgu3, wdn, bdn3)


def _combine_kernel(pos_cur, pos_next, outs_hbm, h_ref, meta_ref, nw_ref, y_ref, buf, sem):
    i = pl.program_id(0)
    n = pl.num_programs(0)
    tc = h_ref.shape[0]
    slot = i % 2

    def issue(pos_ref, s):
        def body(r, carry):
            for kk in range(TOP_K):
                _row_copy(outs_hbm, pos_ref[0, 0, kk * tc + r], buf.at[s, kk], r,
                          sem.at[s]).start(priority=kk % DMA_PRIORITIES)
            return carry
        lax.fori_loop(0, tc, body, 0, unroll=4)

    @pl.when(i == 0)
    def _():
        issue(pos_cur, 0)

    @pl.when(i + 1 < n)
    def _():
        issue(pos_next, 1 - slot)

    for kk in range(TOP_K):
        _wait_row_copies(outs_hbm, buf.at[slot, kk], tc, sem.at[slot])
    meta = meta_ref[...]
    gates = [meta[:, META_GATE + kk:META_GATE + kk + 1] for kk in range(TOP_K)]
    rows = [_load_row_tile_cols(buf.at[slot, kk], tc) for kk in range(TOP_K)]
    cols = []
    for s in range(ROW_SUBTILES):
        acc = h_ref[:, s * LANES:(s + 1) * LANES]
        for kk in range(TOP_K):
            acc = acc + gates[kk] * rows[kk][s]
        cols.append(acc)
    y_ref[...] = _rms(jnp.concatenate(cols, axis=1), nw_ref[...])


def _combine(pos, outs_rt, h, meta, nw):
    t = h.shape[0]
    tc = min(COMBINE_TILE, t)
    n = t // tc
    pos3 = _tile_slots(pos, tc)
    return pl.pallas_call(
        _combine_kernel,
        grid=(n,),
        in_specs=[
            pl.BlockSpec((1, 1, TOP_K * tc), lambda i: (i, 0, 0), memory_space=pltpu.SMEM),
            pl.BlockSpec((1, 1, TOP_K * tc), lambda i: (jnp.minimum(i + 1, n - 1), 0, 0),
                         memory_space=pltpu.SMEM),
            pl.BlockSpec(memory_space=pl.ANY),
            pl.BlockSpec((tc, D_MODEL), lambda i: (i, 0)),
            pl.BlockSpec((tc, LANES), lambda i: (i, 0)),
            pl.BlockSpec((1, D_MODEL), lambda i: (0, 0)),
        ],
        out_specs=pl.BlockSpec((tc, D_MODEL), lambda i: (i, 0)),
        out_shape=jax.ShapeDtypeStruct((t, D_MODEL), F32),
        scratch_shapes=[pltpu.VMEM((2, TOP_K, tc * ROW_SUBTILES, LANES), F32), pltpu.SemaphoreType.DMA((2,))],
        compiler_params=_cparams(("arbitrary",)),
        name="combine",
    )(pos3, pos3, outs_rt, h, meta, nw)


def _pad_lanes(v, fill=0.0):
    v = v.astype(F32).reshape(1, -1)
    return jnp.pad(v, ((0, 0), (0, LANES - v.shape[1])), constant_values=fill)


def kernel(x, norm_mix_w, w_in, conv_w, conv_b, dt_bias, a_log, d_skip, ssm_norm_w, lstm_i_bias,
           lstm_f_bias, lstm_norm_w, w_out, norm_ffn_w, w_router, b_router, w_gate_up, b_gate_up,
           w_down, b_down, norm_final_w):
    b, s, d = x.shape
    t = b * s
    x2 = x.reshape(t, d).astype(F32)
    depth = w_in.shape[0]
    assert depth == 1, "the combine kernel fuses the final norm, so exactly one layer is supported"
    for layer in range(depth):
        w_all = _regroup_weights(w_in.astype(F32), layer)
        sel = (jnp.arange(LANES)[:, None] == (jnp.arange(SSM_WIDTH) // SSM_HEAD_DIM)[None, :]).astype(BF16)
        a_neg = _pad_lanes(-jnp.exp(a_log[layer].astype(F32)))
        dskip_x = jnp.repeat(d_skip[layer].astype(F32), SSM_HEAD_DIM).reshape(1, SSM_WIDTH)
        wr = jnp.pad(w_router[layer], ((0, 0), (0, LANES - N_EXPERTS))).astype(BF16)
        br = _pad_lanes(b_router[layer], fill=NEG_INF)

        z, xbc, q, k, v, o, gates = _inproj(x2, norm_mix_w[layer].reshape(1, d).astype(F32), w_all)
        y_ssd, y_lstm = _mixers(
            xbc, z, gates, conv_w[layer].astype(F32), conv_b[layer].reshape(1, -1).astype(F32),
            _pad_lanes(dt_bias[layer]), a_neg, dskip_x, ssm_norm_w[layer].reshape(1, -1).astype(F32), sel,
            q, k, v, o, _pad_lanes(lstm_i_bias[layer]), _pad_lanes(lstm_f_bias[layer]),
            lstm_norm_w[layer].reshape(1, -1).astype(F32), b, s)
        h, hn_rt, meta, meta_t, cnt = _outproj(y_ssd, y_lstm, x2, w_out[layer].astype(BF16),
                                               norm_ffn_w[layer].reshape(1, d).astype(F32), wr, br)

        idx = meta_t[META_IDX:META_IDX + TOP_K].astype(jnp.int32)
        rank = meta_t[META_RANK:META_RANK + TOP_K].astype(jnp.int32)
        counts = cnt[0, :N_EXPERTS].astype(jnp.int32)
        n_blocks = -(-(t * TOP_K) // MOE_BLOCK) + N_EXPERTS
        padded = (counts + MOE_BLOCK - 1) // MOE_BLOCK * MOE_BLOCK
        padded_ends = jnp.cumsum(padded)
        padded_starts = padded_ends - padded
        onehot = idx[..., None] == jnp.arange(N_EXPERTS, dtype=jnp.int32)
        pos = jnp.sum(jnp.where(onehot, padded_starts, 0), axis=-1) + rank
        block_start = jnp.arange(n_blocks, dtype=jnp.int32) * MOE_BLOCK
        block_expert = jnp.minimum(
            jnp.sum((padded_ends[None, :] <= block_start[:, None]).astype(jnp.int32), axis=1), N_EXPERTS - 1)
        n_used = padded_ends[-1:] // MOE_BLOCK
        pad_len = padded - counts
        misc = jnp.concatenate([jnp.sum(pad_len, keepdims=True), n_used,
                                jnp.full((1,), n_blocks, jnp.int32)])

        xs_rt = _dispatch(padded_starts + counts, pad_len, misc, pos, hn_rt, n_blocks * MOE_BLOCK)
        eids = jnp.arange(N_EXPERTS, dtype=jnp.int32)
        later_nonempty = (eids[None, :] > eids[:, None]) & (counts[None, :] > 0)
        next_expert = jnp.min(jnp.where(later_nonempty, eids[None, :], N_EXPERTS), axis=1)
        next_expert = jnp.where(next_expert < N_EXPERTS, next_expert, -1)
        outs_rt = _experts(block_expert, n_used, next_expert, xs_rt, w_gate_up[layer],
                           b_gate_up[layer].reshape(N_EXPERTS, 1, -1), w_down[layer],
                           b_down[layer].reshape(N_EXPERTS, 1, -1))
        x2 = _combine(pos, outs_rt, h, meta, norm_final_w.reshape(1, d).astype(F32))
    return x2.reshape(b, s, d).astype(x.dtype)
```

```python
import jax
import jax.numpy as jnp
import numpy as np
from jax import lax
from jax.experimental import pallas as pl
from jax.experimental.pallas import tpu as pltpu

F32 = jnp.float32
BF16 = jnp.bfloat16

D_MODEL = 1024
SSM_WIDTH = 1024
SSM_HEAD_DIM = 64
SSM_HEADS = 16
SSM_GROUPS = 2
SSM_STATE = 128
CONV_WIDTH = 4
CONV_DIM = SSM_WIDTH + 2 * SSM_GROUPS * SSM_STATE
LSTM_WIDTH = 1024
LSTM_HEAD_DIM = 128
LSTM_HEADS = 8
N_EXPERTS = 32
TOP_K = 4
D_FF = 1024
SWIGLU_LIMIT = 7.0
SWIGLU_ALPHA = 1.702
MOE_BLOCK = 256
RMS_EPS = 1e-6

LANES = 128
SUBLANES = 8
MIX_CHUNK = 256
ROW_TILE = 512
COMBINE_TILE = 512
VMEM_LIMIT = 56 * 1024 * 1024

GATE_COLS = LANES
GATE_LIVE = SSM_HEADS + 2 * LSTM_HEADS
NEG_INF = float("-inf")


def _cparams(sem):
    return pltpu.CompilerParams(dimension_semantics=sem, vmem_limit_bytes=VMEM_LIMIT)


def _rms(x, w):
    return x * lax.rsqrt(jnp.mean(x * x, axis=-1, keepdims=True) + RMS_EPS) * w


LOG2E = 1.4426950408889634


def _sigmoid(x):
    return 1.0 / (1.0 + jnp.exp2(x * -LOG2E))


def _softplus(x):
    return jnp.maximum(x, 0.0) + jnp.log(1.0 + jnp.exp(-jnp.abs(x)))


def _split3(a):
    hi = a.astype(BF16)
    r = a - hi.astype(F32)
    mid = r.astype(BF16)
    lo = (r - mid.astype(F32)).astype(BF16)
    return hi, mid, lo


def _dot(a, b):
    return jnp.dot(a, b, preferred_element_type=F32)


def _dot_nt(a, b):
    return lax.dot_general(a, b, (((1,), (1,)), ((), ())), preferred_element_type=F32)


def _dot_tn(a, b):
    return lax.dot_general(a, b, (((0,), (0,)), ((), ())), preferred_element_type=F32)


def _sel_dot(sel_bf, a):
    hi, mid, lo = _split3(a)
    return _dot(sel_bf, hi) + _dot(sel_bf, mid) + _dot(sel_bf, lo)


def _expand(a, sel_bf, terms=3):
    out = None
    for piece in _split3(a)[:terms]:
        d = _dot(piece, sel_bf)
        out = d if out is None else out + d
    return out


ROW_SUBTILES = D_MODEL // LANES
assert ROW_SUBTILES == SUBLANES, "a model row must fill exactly one (8, 128) f32 tile"
DMA_PRIORITIES = 2


def _store_row_tiles(ref, x):
    n = x.shape[0]
    for s in range(ROW_SUBTILES):
        ref[pl.ds(s, n, stride=ROW_SUBTILES), :] = x[:, s * LANES:(s + 1) * LANES]


def _load_row_tile_cols(ref, n):
    return [ref[pl.ds(s, n, stride=ROW_SUBTILES), :] for s in range(ROW_SUBTILES)]


def _row_copy(src, src_row, dst, dst_row, sem):
    return pltpu.make_async_copy(
        src.at[pl.ds(pl.multiple_of(src_row * ROW_SUBTILES, ROW_SUBTILES), ROW_SUBTILES), :],
        dst.at[pl.ds(pl.multiple_of(dst_row * ROW_SUBTILES, ROW_SUBTILES), ROW_SUBTILES), :], sem)


def _wait_row_copies(src, dst, n_rows, sem):
    size = n_rows * ROW_SUBTILES
    pltpu.make_async_copy(src.at[pl.ds(0, size), :], dst.at[pl.ds(0, size), :], sem).wait()


_INPROJ_WIDTHS = (SSM_WIDTH, CONV_DIM, LSTM_WIDTH, LSTM_WIDTH, LSTM_WIDTH, LSTM_WIDTH, GATE_COLS)
_C_XBC_END = SSM_WIDTH + CONV_DIM
_C_DT_END = _C_XBC_END + SSM_HEADS
_C_O_END = _C_DT_END + 4 * LSTM_WIDTH
_C_I_END = _C_O_END + LSTM_HEADS
IN_PROJ_DIM = _C_I_END + LSTM_HEADS
REGROUP_ROWS = 128


def _regroup_kernel(w_ref, o_ref):
    rows = w_ref.shape[0]
    big = _C_XBC_END + 4 * LSTM_WIDTH
    o_ref[:, :_C_XBC_END] = w_ref[:, :_C_XBC_END].astype(BF16)
    o_ref[:, _C_XBC_END:big] = w_ref[:, _C_DT_END:_C_O_END].astype(BF16)
    gate_w = jnp.concatenate([w_ref[:, _C_XBC_END:_C_DT_END], w_ref[:, _C_O_END:IN_PROJ_DIM],
                              jnp.zeros((rows, GATE_COLS - GATE_LIVE), F32)], axis=1)
    o_ref[:, big:] = gate_w.astype(BF16)


def _regroup_weights(w_in, layer):
    k = w_in.shape[1]
    ncol = sum(_INPROJ_WIDTHS)
    return pl.pallas_call(
        _regroup_kernel,
        grid=(k // REGROUP_ROWS,),
        in_specs=[pl.BlockSpec((None, REGROUP_ROWS, IN_PROJ_DIM), lambda i: (layer, i, 0))],
        out_specs=pl.BlockSpec((REGROUP_ROWS, ncol), lambda i: (i, 0)),
        out_shape=jax.ShapeDtypeStruct((k, ncol), BF16),
        compiler_params=_cparams(("arbitrary",)),
        name="regroup_weights",
    )(w_in)


def _inproj_kernel(x_ref, nw_ref, w_ref, z_ref, xbc_ref, q_ref, k_ref, v_ref, o_ref, g_ref):
    xb = _rms(x_ref[...], nw_ref[...]).astype(BF16)
    off = 0
    for ref, width in zip((z_ref, xbc_ref, q_ref, k_ref, v_ref, o_ref, g_ref), _INPROJ_WIDTHS):
        ref[...] = _dot(xb, w_ref[:, off:off + width]).astype(ref.dtype)
        off += width


def _inproj(x2, nw, w_all):
    t = x2.shape[0]
    tm = min(ROW_TILE, t)
    ncol = w_all.shape[1]
    out_shape = [jax.ShapeDtypeStruct((t, w), BF16) for w in _INPROJ_WIDTHS[:-1]]
    out_shape.append(jax.ShapeDtypeStruct((t, GATE_COLS), F32))
    return pl.pallas_call(
        _inproj_kernel,
        grid=(t // tm,),
        in_specs=[
            pl.BlockSpec((tm, D_MODEL), lambda i: (i, 0)),
            pl.BlockSpec((1, D_MODEL), lambda i: (0, 0)),
            pl.BlockSpec((D_MODEL, ncol), lambda i: (0, 0), pipeline_mode=pl.Buffered(1)),
        ],
        out_specs=[pl.BlockSpec((tm, w), lambda i: (i, 0)) for w in _INPROJ_WIDTHS],
        out_shape=out_shape,
        compiler_params=_cparams(("arbitrary",)),
        name="inproj",
    )(x2, nw, w_all)


SSD_PAIR_GROUP = 2
LSTM_HEAD_GROUP = 2


def _ssd_phases(xbc_ref, z_ref, g_ref, convw_ref, convb_ref, dtb_ref, aneg_ref, dskip_ref, nw_ref,
                sel_ref, y_ref, u_scr, st_scr):
    L = xbc_ref.shape[0]
    gw = SSM_WIDTH // SSM_GROUPS

    row = lax.broadcasted_iota(jnp.int32, (L, L), 0)
    col = lax.broadcasted_iota(jnp.int32, (L, L), 1)
    causal = row >= col

    xin_bf = xbc_ref[...]
    xin = xin_bf.astype(F32)
    halo = u_scr[...]
    row8 = lax.broadcasted_iota(jnp.int32, (SUBLANES, 1), 0)
    acc = convb_ref[...] + convw_ref[CONV_WIDTH - 1:CONV_WIDTH, :] * xin
    for back in range(1, CONV_WIDTH):
        shifted = _dot((row - col == back).astype(F32).astype(BF16), xin_bf)
        head = shifted[:SUBLANES] + jnp.where(row8 < back, pltpu.roll(halo, back, axis=0), 0.0)
        shifted = jnp.concatenate([head, shifted[SUBLANES:]], axis=0)
        acc = acc + convw_ref[CONV_WIDTH - 1 - back:CONV_WIDTH - back, :] * shifted
    u_scr[...] = xin[L - SUBLANES:, :]
    xbc = acc * _sigmoid(acc)
    xs = xbc[:, :SSM_WIDTH]
    bm = xbc[:, SSM_WIDTH:SSM_WIDTH + SSM_GROUPS * SSM_STATE].astype(BF16)
    cm = xbc[:, SSM_WIDTH + SSM_GROUPS * SSM_STATE:].astype(BF16)
    yield

    tril_bf = causal.astype(F32).astype(BF16)

    lane = lax.broadcasted_iota(jnp.int32, (1, LANES), 1)
    dt = jnp.where(lane < SSM_HEADS, _softplus(g_ref[...] + dtb_ref[...]), 0.0)
    acs = _sel_dot(tril_bf, dt * aneg_ref[...]) * LOG2E
    acs_t = acs.T
    sel = sel_ref[...]
    acs_last = acs[L - 1:L, :]
    ea_x = _expand(jnp.exp2(acs), sel, terms=2)
    xd = xs * _expand(dt, sel, terms=1)
    xd_bf = xd.astype(BF16)
    xw = (xd * _expand(jnp.exp2(acs_last - acs), sel, terms=1)).astype(BF16)
    yield

    cbs = [_dot_nt(cm[:, g * SSM_STATE:(g + 1) * SSM_STATE], bm[:, g * SSM_STATE:(g + 1) * SSM_STATE])
           for g in range(SSM_GROUPS)]
    yoff = [_dot(cm[:, g * SSM_STATE:(g + 1) * SSM_STATE], st_scr[g].astype(BF16)) for g in range(SSM_GROUPS)]
    upd = [_dot_tn(bm[:, g * SSM_STATE:(g + 1) * SSM_STATE], xw[:, g * gw:(g + 1) * gw])
           for g in range(SSM_GROUPS)]
    for g in range(SSM_GROUPS):
        st_scr[g] = st_scr[g] * ea_x[L - 1:L, g * gw:(g + 1) * gw] + upd[g]
    yield

    pair_lane = lax.broadcasted_iota(jnp.int32, (1, LANES), 1)
    heads_per_group = SSM_HEADS // SSM_GROUPS
    rhs = []
    for j in range(SSM_HEADS // 2):
        xpair = xd_bf[:, j * LANES:(j + 1) * LANES]
        zero = jnp.zeros_like(xpair)
        rhs.append(jnp.concatenate([jnp.where(pair_lane < SSM_HEAD_DIM, xpair, zero),
                                    jnp.where(pair_lane >= SSM_HEAD_DIM, xpair, zero)], axis=0))
    ydiag = []
    for j0 in range(0, SSM_HEADS // 2, SSD_PAIR_GROUP):
        pairs = range(j0, j0 + SSD_PAIR_GROUP)
        ms = {h: (cbs[h // heads_per_group]
                  * jnp.exp2(jnp.where(causal, acs[:, h:h + 1] - acs_t[h:h + 1, :], NEG_INF))).astype(BF16)
              for j in pairs for h in (2 * j, 2 * j + 1)}
        yield
        ydiag += [_dot(jnp.concatenate([ms[2 * j], ms[2 * j + 1]], axis=1), rhs[j]) for j in pairs]
        yield
    y = jnp.concatenate(ydiag, axis=1) + jnp.concatenate(yoff, axis=1) * ea_x + dskip_ref[...] * xs

    zz = z_ref[...].astype(F32)
    y = y * (zz * _sigmoid(zz))
    y_ref[...] = _rms(y, nw_ref[...]).astype(BF16)


def _mlstm_phases(q_ref, k_ref, v_ref, o_ref, g_ref, ib_ref, fb_ref, nw_ref, y_ref,
                  st_scr, m_scr):
    L = q_ref.shape[0]
    dh = LSTM_HEAD_DIM
    scale = dh ** -0.5

    row = lax.broadcasted_iota(jnp.int32, (L, L), 0)
    col = lax.broadcasted_iota(jnp.int32, (L, L), 1)
    causal = row >= col
    tril_bf = causal.astype(F32).astype(BF16)
    lane = lax.broadcasted_iota(jnp.int32, (1, LANES), 1)
    live = lane < LSTM_HEADS

    gates = g_ref[...]
    gi = pltpu.roll(gates, LANES - SSM_HEADS, axis=1)
    gf = pltpu.roll(gates, LANES - SSM_HEADS - LSTM_HEADS, axis=1)
    ii = jnp.where(live, gi + ib_ref[...], 0.0)
    logf = jnp.where(live, -_softplus(-(gf + fb_ref[...])), 0.0)
    cumf = _sel_dot(tril_bf, logf)
    g = ii - cumf
    rid = lax.broadcasted_iota(jnp.int32, (L, LANES), 0)
    cmx = g
    step = 1
    while step < L:
        cmx = jnp.maximum(cmx, jnp.where(rid >= step, pltpu.roll(cmx, step, axis=0), NEG_INF))
        step *= 2
    m_prev = m_scr[...]
    mx = jnp.maximum(m_prev, cmx)
    w_inter = jnp.exp(m_prev - mx)
    enm = jnp.exp(-(cumf + mx))
    g2_t = (g * LOG2E).T
    mx2 = mx * LOG2E - np.log2(scale)
    m_last = mx[L - 1:L, :]
    wk = jnp.exp(g - m_last) * scale
    sc = jnp.exp(m_prev - m_last)
    m_scr[...] = cumf[L - 1:L, :] + m_last
    yield

    ones_bf = jnp.ones((L, dh), BF16)
    heads = range(LSTM_HEADS)
    hsl = [slice(h * dh, (h + 1) * dh) for h in heads]
    vaug = [jnp.concatenate([v_ref[:, hsl[h]], ones_bf], axis=1) for h in heads]
    r2 = [_dot(q_ref[:, hsl[h]], st_scr[h].astype(BF16)) for h in heads]
    upd = [_dot_tn((k_ref[:, hsl[h]].astype(F32) * wk[:, h:h + 1]).astype(BF16), vaug[h]) for h in heads]
    for h in heads:
        st_scr[h] = st_scr[h] * sc[:, h:h + 1] + upd[h]
    yield
    r1 = []
    for h0 in range(0, LSTM_HEADS, LSTM_HEAD_GROUP):
        group = range(h0, h0 + LSTM_HEAD_GROUP)
        s_qk = {h: _dot_nt(q_ref[:, hsl[h]], k_ref[:, hsl[h]]) for h in group}
        yield
        p = {h: (s_qk[h] * jnp.exp2(jnp.where(causal, g2_t[h:h + 1, :] - mx2[:, h:h + 1], NEG_INF))).astype(BF16)
             for h in group}
        yield
        r1 += [_dot(p[h], vaug[h]) for h in group]
        yield
    hh = []
    for h in heads:
        wcol = w_inter[:, h:h + 1]
        num = r1[h][:, :dh] + r2[h][:, :dh] * wcol
        den = r1[h][:, dh:] + r2[h][:, dh:] * wcol
        hh.append(num / jnp.maximum(jnp.abs(den), enm[:, h:h + 1]))
    inv = [lax.rsqrt(jnp.mean(hh[h] * hh[h], axis=-1, keepdims=True) + RMS_EPS) for h in heads]
    for h in heads:
        oo = o_ref[:, hsl[h]].astype(F32)
        y_ref[:, hsl[h]] = (_sigmoid(oo) * (hh[h] * inv[h] * nw_ref[:, hsl[h]])).astype(BF16)


N_SSD_IN, N_LSTM_IN = 10, 8


def _mixer_kernel(*refs):
    ssd_in = refs[:N_SSD_IN]
    lstm_in = refs[N_SSD_IN:N_SSD_IN + N_LSTM_IN]
    y_ssd, y_lstm, u_scr, st_ssd, st_lstm, m_scr = refs[N_SSD_IN + N_LSTM_IN:]

    @pl.when(pl.program_id(1) == 0)
    def _():
        u_scr[...] = jnp.zeros_like(u_scr)
        st_ssd[...] = jnp.zeros_like(st_ssd)
        st_lstm[...] = jnp.zeros_like(st_lstm)
        m_scr[...] = jnp.zeros_like(m_scr)

    streams = [_ssd_phases(*ssd_in, y_ssd, u_scr, st_ssd), _mlstm_phases(*lstm_in, y_lstm, st_lstm, m_scr)]
    while streams:
        for stream in list(streams):
            if next(stream, StopIteration) is StopIteration:
                streams.remove(stream)


def _mixers(xbc, z, gates, convw, convb, dtb, aneg, dskip_x, nw_ssd, sel, q, k, v, o, ib, fb, nw_lstm, b, s):
    L = min(MIX_CHUNK, s)
    nc = s // L
    tok = lambda bi, ci: (bi * nc + ci, 0)
    const = lambda bi, ci: (0, 0)
    big = pl.BlockSpec((L, LSTM_WIDTH), tok)
    gate_block = lambda j: pl.BlockSpec((L, LANES), lambda bi, ci: (bi * nc + ci, j))
    vec = lambda w: pl.BlockSpec((1, w), const)
    in_specs = [pl.BlockSpec((L, CONV_DIM), tok), pl.BlockSpec((L, SSM_WIDTH), tok), gate_block(0),
                pl.BlockSpec((CONV_WIDTH, CONV_DIM), const), vec(CONV_DIM), vec(LANES), vec(LANES),
                vec(SSM_WIDTH), vec(SSM_WIDTH), pl.BlockSpec((LANES, SSM_WIDTH), const),
                big, big, big, big, gate_block(0), vec(LANES), vec(LANES), vec(LSTM_WIDTH)]
    assert len(in_specs) == N_SSD_IN + N_LSTM_IN
    return pl.pallas_call(
        _mixer_kernel,
        grid=(b, nc),
        in_specs=in_specs,
        out_specs=[pl.BlockSpec((L, SSM_WIDTH), tok), big],
        out_shape=[jax.ShapeDtypeStruct((b * s, SSM_WIDTH), BF16), jax.ShapeDtypeStruct((b * s, LSTM_WIDTH), BF16)],
        scratch_shapes=[pltpu.VMEM((SUBLANES, CONV_DIM), F32),
                        pltpu.VMEM((SSM_GROUPS, SSM_STATE, SSM_WIDTH // SSM_GROUPS), F32),
                        pltpu.VMEM((LSTM_HEADS, LSTM_HEAD_DIM, 2 * LSTM_HEAD_DIM), F32),
                        pltpu.VMEM((1, LANES), F32)],
        compiler_params=_cparams(("arbitrary", "arbitrary")),
        name="mixers",
    )(xbc, z, gates, convw, convb, dtb, aneg, dskip_x, nw_ssd, sel, q, k, v, o, gates, ib, fb, nw_lstm)


META_IDX, META_GATE, META_RANK = 0, TOP_K, 2 * TOP_K
META_ROWS = 2 * SUBLANES


def _outproj_kernel(ys_ref, yl_ref, x_ref, wo_ref, nw_ref, wr_ref, br_ref, h_ref, hn_ref, meta_ref,
                    meta_t_ref, cnt_ref, cnt_scr, logit_scr):
    tm = x_ref.shape[0]
    step = pl.program_id(0)

    @pl.when(step == 0)
    def _():
        cnt_scr[...] = jnp.zeros_like(cnt_scr)
        logit_scr[...] = jnp.zeros_like(logit_scr)

    vals = logit_scr[...]
    routed = (step > 0).astype(F32)

    h = (x_ref[...] + _dot(ys_ref[...], wo_ref[:SSM_WIDTH, :]) + _dot(yl_ref[...], wo_ref[SSM_WIDTH:, :]))
    h_ref[...] = h
    hn = _rms(h, nw_ref[...])
    _store_row_tiles(hn_ref, hn)
    logit_scr[...] = _dot(hn.astype(BF16), wr_ref[...]) + br_ref[...]

    lane = lax.broadcasted_iota(jnp.int32, (tm, LANES), 1)
    member = jnp.zeros((tm, LANES), F32)
    tops, idxs, sels = [], [], []
    for _ in range(TOP_K):
        m = jnp.max(vals, axis=-1, keepdims=True)
        idx = jnp.min(jnp.where(vals == m, lane, LANES), axis=-1, keepdims=True)
        sel = lane == idx
        vals = jnp.where(sel, NEG_INF, vals)
        member = member + sel.astype(F32)
        tops.append(m)
        idxs.append(idx)
        sels.append(sel)
    es = [jnp.exp(t - tops[0]) for t in tops]
    inv = 1.0 / (es[0] + es[1] + es[2] + es[3])
    member = member * routed

    r = lax.broadcasted_iota(jnp.int32, (tm, tm), 0)
    c = lax.broadcasted_iota(jnp.int32, (tm, tm), 1)
    strict = (r > c).astype(F32).astype(BF16)
    carry = cnt_scr[0:1, :]
    rank_all = _dot(strict, member.astype(BF16)) + carry
    total = carry + jnp.sum(member, axis=0, keepdims=True)
    cnt_scr[...] = jnp.broadcast_to(total, cnt_scr.shape)
    cnt_ref[...] = jnp.broadcast_to(total, cnt_ref.shape)

    meta = jnp.zeros((tm, LANES), F32)
    for kk in range(TOP_K):
        rank = jnp.sum(jnp.where(sels[kk], rank_all, 0.0), axis=-1, keepdims=True)
        meta = jnp.where(lane == META_IDX + kk, idxs[kk].astype(F32), meta)
        meta = jnp.where(lane == META_GATE + kk, es[kk] * inv, meta)
        meta = jnp.where(lane == META_RANK + kk, rank, meta)
    meta_ref[...] = meta
    eye = (lax.broadcasted_iota(jnp.int32, (META_ROWS, LANES), 0)
           == lax.broadcasted_iota(jnp.int32, (META_ROWS, LANES), 1)).astype(F32).astype(BF16)
    pieces = _split3(meta)
    meta_t_ref[...] = _dot_nt(eye, pieces[0]) + _dot_nt(eye, pieces[1]) + _dot_nt(eye, pieces[2])


def _outproj(ys, yl, x2, wo, nw, wr, br):
    t = x2.shape[0]
    tm = min(ROW_TILE, t)
    n = t // tm
    projected = lambda i: (jnp.minimum(i, n - 1), 0)
    routed = lambda i: (jnp.maximum(i - 1, 0), 0)
    tokspec = lambda w: pl.BlockSpec((tm, w), projected)
    const = lambda i: (0, 0)
    return pl.pallas_call(
        _outproj_kernel,
        grid=(n + 1,),
        in_specs=[tokspec(SSM_WIDTH), tokspec(LSTM_WIDTH), tokspec(D_MODEL),
                  pl.BlockSpec((SSM_WIDTH + LSTM_WIDTH, D_MODEL), const, pipeline_mode=pl.Buffered(1)),
                  pl.BlockSpec((1, D_MODEL), const),
                  pl.BlockSpec((D_MODEL, LANES), const),
                  pl.BlockSpec((1, LANES), const)],
        out_specs=[tokspec(D_MODEL), pl.BlockSpec((tm * ROW_SUBTILES, LANES), projected),
                   pl.BlockSpec((tm, LANES), routed),
                   pl.BlockSpec((META_ROWS, tm), lambda i: (0, jnp.maximum(i - 1, 0))),
                   pl.BlockSpec((SUBLANES, LANES), const)],
        out_shape=[jax.ShapeDtypeStruct((t, D_MODEL), F32), jax.ShapeDtypeStruct((t * ROW_SUBTILES, LANES), F32),
                   jax.ShapeDtypeStruct((t, LANES), F32), jax.ShapeDtypeStruct((META_ROWS, t), F32),
                   jax.ShapeDtypeStruct((SUBLANES, LANES), F32)],
        scratch_shapes=[pltpu.VMEM((SUBLANES, LANES), F32), pltpu.VMEM((tm, LANES), F32)],
        compiler_params=_cparams(("arbitrary",)),
        name="outproj_router",
    )(ys, yl, x2, wo, nw, wr, br)


DISPATCH_TILE = 1024
DISPATCH_GROUP = 4
PAD_CHUNKS = (128, 64, 32, 16, 8, 4, 2, 1)


def _dispatch_kernel(padstart_ref, padlen_ref, misc_ref, pos_ref, hn_ref, xs_hbm, sem, zsem, zeros_scr):
    i = pl.program_id(0)
    td = pos_ref.shape[-1] // TOP_K

    @pl.when(i == 0)
    def _():
        zeros_scr[...] = jnp.zeros_like(zeros_scr)
        for e in range(N_EXPERTS):
            row = padstart_ref[e]
            nrow = padlen_ref[e]
            for chunk in PAD_CHUNKS:
                @pl.when((nrow & chunk) != 0)
                def _(row=row, chunk=chunk):
                    pltpu.make_async_copy(
                        zeros_scr.at[pl.ds(0, chunk * ROW_SUBTILES), :],
                        xs_hbm.at[pl.ds(pl.multiple_of(row * ROW_SUBTILES, ROW_SUBTILES), chunk * ROW_SUBTILES), :],
                        zsem).start()
                row = row + (nrow & chunk)

        def tail_copy(blk):
            return pltpu.make_async_copy(
                zeros_scr,
                xs_hbm.at[pl.ds(pl.multiple_of(blk * (MOE_BLOCK * ROW_SUBTILES), ROW_SUBTILES),
                                MOE_BLOCK * ROW_SUBTILES), :], zsem)

        n_used, n_blocks = misc_ref[1], misc_ref[2]

        def start_tail(blk, carry):
            tail_copy(blk).start()
            return carry
        lax.fori_loop(n_used, n_blocks, start_tail, 0)

        def drain_row(j, carry):
            _wait_row_copies(zeros_scr, xs_hbm, 1, zsem)
            return carry
        lax.fori_loop(0, misc_ref[0], drain_row, 0)

        def drain_tail(blk, carry):
            tail_copy(blk).wait()
            return carry
        lax.fori_loop(n_used, n_blocks, drain_tail, 0)

    def body(g, carry):
        r0 = g * DISPATCH_GROUP
        slots = [pos_ref[0, 0, (j % TOP_K) * td + r0 + j // TOP_K] for j in range(DISPATCH_GROUP * TOP_K)]
        for j, slot in enumerate(slots):
            _row_copy(hn_ref, r0 + j // TOP_K, xs_hbm, slot, sem).start(priority=j % DMA_PRIORITIES)
        return carry
    lax.fori_loop(0, td // DISPATCH_GROUP, body, 0)

    for _ in range(TOP_K):
        _wait_row_copies(hn_ref, xs_hbm, td, sem)


def _tile_slots(pos, tile):
    t = pos.shape[1]
    return pos.reshape(TOP_K, t // tile, tile).transpose(1, 0, 2).reshape(t // tile, 1, TOP_K * tile)


def _dispatch(pad_start, pad_len, misc, pos, hn_rt, n_rows):
    t = pos.shape[1]
    td = min(DISPATCH_TILE, t)
    n = t // td
    grid_spec = pltpu.PrefetchScalarGridSpec(
        num_scalar_prefetch=3,
        grid=(n,),
        in_specs=[pl.BlockSpec((1, 1, TOP_K * td), lambda i, *_: (i, 0, 0), memory_space=pltpu.SMEM),
                  pl.BlockSpec((td * ROW_SUBTILES, LANES), lambda i, *_: (i, 0))],
        out_specs=pl.BlockSpec(memory_space=pl.ANY),
        scratch_shapes=[pltpu.SemaphoreType.DMA(()), pltpu.SemaphoreType.DMA(()),
                        pltpu.VMEM((MOE_BLOCK * ROW_SUBTILES, LANES), F32)],
    )
    return pl.pallas_call(
        _dispatch_kernel,
        grid_spec=grid_spec,
        out_shape=jax.ShapeDtypeStruct((n_rows * ROW_SUBTILES, LANES), F32),
        compiler_params=_cparams(("arbitrary",)),
        name="dispatch",
    )(pad_start, pad_len, misc, _tile_slots(pos, td), hn_rt)


def _experts_kernel(be_ref, nused_ref, next_ref, x_ref, wgu_hbm, bgu_ref, wdn_hbm, bdn_ref, out_ref,
                    wgu_bf, wdn_bf, wgu_st, wdn_st, wsem, slot_ref):
    i = pl.program_id(0)
    n_used = nused_ref[0]

    def weight_copies(e, slot):
        return (pltpu.make_async_copy(wgu_hbm.at[e], wgu_st.at[slot], wsem.at[0, slot]),
                pltpu.make_async_copy(wdn_hbm.at[e], wdn_st.at[slot], wsem.at[1, slot]))

    @pl.when(i == 0)
    def _():
        slot_ref[0] = 0
        for cp in weight_copies(be_ref[0], 0):
            cp.start()

    @pl.when(i < n_used)
    def _():
        @pl.when(jnp.logical_or(i == 0, be_ref[i] != be_ref[jnp.maximum(i - 1, 0)]))
        def _():
            e = be_ref[i]
            slot = slot_ref[0]
            for cp in weight_copies(e, slot):
                cp.wait()
            wgu_bf[...] = wgu_st[slot].astype(BF16)
            wdn_bf[...] = wdn_st[slot].astype(BF16)
            nxt = next_ref[e]

            @pl.when(nxt >= 0)
            def _():
                for cp in weight_copies(nxt, 1 - slot):
                    cp.start()
            slot_ref[0] = 1 - slot

        xb = jnp.concatenate(_load_row_tile_cols(x_ref, MOE_BLOCK), axis=1).astype(BF16)
        hh = _dot(xb, wgu_bf[...]) + bgu_ref[...]
        gh = jnp.minimum(hh[:, :D_FF], SWIGLU_LIMIT)
        uh = jnp.clip(hh[:, D_FF:], -SWIGLU_LIMIT, SWIGLU_LIMIT)
        act = (uh + 1.0) * (gh * _sigmoid(SWIGLU_ALPHA * gh))
        _store_row_tiles(out_ref, _dot(act.astype(BF16), wdn_bf[...]) + bdn_ref[...])

    @pl.when(i >= n_used)
    def _():
        out_ref[...] = jnp.zeros_like(out_ref)


def _experts(block_expert, n_used, next_expert, xs_rt, wgu, bgu3, wdn, bdn3):
    n_blocks = block_expert.shape[0]
    blk = MOE_BLOCK * ROW_SUBTILES
    grid_spec = pltpu.PrefetchScalarGridSpec(
        num_scalar_prefetch=3,
        grid=(n_blocks,),
        in_specs=[
            pl.BlockSpec((blk, LANES), lambda i, be, nu, nx: (i, 0)),
            pl.BlockSpec(memory_space=pl.ANY),
            pl.BlockSpec((None, 1, 2 * D_FF), lambda i, be, nu, nx: (be[i], 0, 0)),
            pl.BlockSpec(memory_space=pl.ANY),
            pl.BlockSpec((None, 1, D_MODEL), lambda i, be, nu, nx: (be[i], 0, 0)),
        ],
        out_specs=pl.BlockSpec((blk, LANES), lambda i, be, nu, nx: (i, 0)),
        scratch_shapes=[pltpu.VMEM((D_MODEL, 2 * D_FF), BF16),
                        pltpu.VMEM((D_FF, D_MODEL), BF16),
                        pltpu.VMEM((2, D_MODEL, 2 * D_FF), F32),
                        pltpu.VMEM((2, D_FF, D_MODEL), F32),
                        pltpu.SemaphoreType.DMA((2, 2)),
                        pltpu.SMEM((1,), jnp.int32)],
    )
    return pl.pallas_call(
        _experts_kernel,
        grid_spec=grid_spec,
        out_shape=jax.ShapeDtypeStruct((n_blocks * blk, LANES), F32),
        compiler_params=_cparams(("arbitrary",)),
        name="experts",
    )(block_expert, n_used, next_expert, xs_rt, wgu, bgu3, wdn, bdn3)


GATHER_PITCH = ROW_SUBTILES + 1


def _combine_kernel(pos_cur, pos_next, outs_hbm, h_ref, meta_ref, nw_ref, y_ref, buf, sem):
    i = pl.program_id(0)
    n = pl.num_programs(0)
    tc = h_ref.shape[0]
    slot = i % 2

    def issue(pos_ref, s):
        def body(r, carry):
            for kk in range(TOP_K):
                src_row = pos_ref[0, 0, kk * tc + r]
                pltpu.make_async_copy(
                    outs_hbm.at[pl.ds(pl.multiple_of(src_row * ROW_SUBTILES, ROW_SUBTILES), ROW_SUBTILES), :],
                    buf.at[s, kk, pl.ds(r * GATHER_PITCH, ROW_SUBTILES), :],
                    sem.at[s]).start(priority=kk % DMA_PRIORITIES)
            return carry
        lax.fori_loop(0, tc, body, 0, unroll=4)

    @pl.when(i == 0)
    def _():
        issue(pos_cur, 0)

    @pl.when(i + 1 < n)
    def _():
        issue(pos_next, 1 - slot)

    for kk in range(TOP_K):
        _wait_row_copies(outs_hbm, buf.at[slot, kk], tc, sem.at[slot])
    meta = meta_ref[...]
    gates = [meta[:, META_GATE + kk:META_GATE + kk + 1] for kk in range(TOP_K)]
    rows = [[buf[slot, kk, pl.ds(s, tc, stride=GATHER_PITCH), :] for s in range(ROW_SUBTILES)]
            for kk in range(TOP_K)]
    cols = []
    for s in range(ROW_SUBTILES):
        acc = h_ref[:, s * LANES:(s + 1) * LANES]
        for kk in range(TOP_K):
            acc = acc + gates[kk] * rows[kk][s]
        cols.append(acc)
    y_ref[...] = _rms(jnp.concatenate(cols, axis=1), nw_ref[...])


def _combine(pos, outs_rt, h, meta, nw):
    t = h.shape[0]
    tc = min(COMBINE_TILE, t)
    n = t // tc
    pos3 = _tile_slots(pos, tc)
    return pl.pallas_call(
        _combine_kernel,
        grid=(n,),
        in_specs=[
            pl.BlockSpec((1, 1, TOP_K * tc), lambda i: (i, 0, 0), memory_space=pltpu.SMEM),
            pl.BlockSpec((1, 1, TOP_K * tc), lambda i: (jnp.minimum(i + 1, n - 1), 0, 0),
                         memory_space=pltpu.SMEM),
            pl.BlockSpec(memory_space=pl.ANY),
            pl.BlockSpec((tc, D_MODEL), lambda i: (i, 0)),
            pl.BlockSpec((tc, LANES), lambda i: (i, 0)),
            pl.BlockSpec((1, D_MODEL), lambda i: (0, 0)),
        ],
        out_specs=pl.BlockSpec((tc, D_MODEL), lambda i: (i, 0)),
        out_shape=jax.ShapeDtypeStruct((t, D_MODEL), F32),
        scratch_shapes=[pltpu.VMEM((2, TOP_K, tc * GATHER_PITCH, LANES), F32), pltpu.SemaphoreType.DMA((2,))],
        compiler_params=_cparams(("arbitrary",)),
        name="combine",
    )(pos3, pos3, outs_rt, h, meta, nw)


def _pad_lanes(v, fill=0.0):
    v = v.astype(F32).reshape(1, -1)
    return jnp.pad(v, ((0, 0), (0, LANES - v.shape[1])), constant_values=fill)


def kernel(x, norm_mix_w, w_in, conv_w, conv_b, dt_bias, a_log, d_skip, ssm_norm_w, lstm_i_bias,
           lstm_f_bias, lstm_norm_w, w_out, norm_ffn_w, w_router, b_router, w_gate_up, b_gate_up,
           w_down, b_down, norm_final_w):
    b, s, d = x.shape
    t = b * s
    x2 = x.reshape(t, d).astype(F32)
    depth = w_in.shape[0]
    assert depth == 1, "the combine kernel fuses the final norm, so exactly one layer is supported"
    for layer in range(depth):
        w_all = _regroup_weights(w_in.astype(F32), layer)
        sel = (jnp.arange(LANES)[:, None] == (jnp.arange(SSM_WIDTH) // SSM_HEAD_DIM)[None, :]).astype(BF16)
        a_neg = _pad_lanes(-jnp.exp(a_log[layer].astype(F32)))
        dskip_x = jnp.repeat(d_skip[layer].astype(F32), SSM_HEAD_DIM).reshape(1, SSM_WIDTH)
        wr = jnp.pad(w_router[layer], ((0, 0), (0, LANES - N_EXPERTS))).astype(BF16)
        br = _pad_lanes(b_router[layer], fill=NEG_INF)

        z, xbc, q, k, v, o, gates = _inproj(x2, norm_mix_w[layer].reshape(1, d).astype(F32), w_all)
        y_ssd, y_lstm = _mixers(
            xbc, z, gates, conv_w[layer].astype(F32), conv_b[layer].reshape(1, -1).astype(F32),
            _pad_lanes(dt_bias[layer]), a_neg, dskip_x, ssm_norm_w[layer].reshape(1, -1).astype(F32), sel,
            q, k, v, o, _pad_lanes(lstm_i_bias[layer]), _pad_lanes(lstm_f_bias[layer]),
            lstm_norm_w[layer].reshape(1, -1).astype(F32), b, s)
        h, hn_rt, meta, meta_t, cnt = _outproj(y_ssd, y_lstm, x2, w_out[layer].astype(BF16),
                                               norm_ffn_w[layer].reshape(1, d).astype(F32), wr, br)

        idx = meta_t[META_IDX:META_IDX + TOP_K].astype(jnp.int32)
        rank = meta_t[META_RANK:META_RANK + TOP_K].astype(jnp.int32)
        counts = cnt[0, :N_EXPERTS].astype(jnp.int32)
        n_blocks = -(-(t * TOP_K) // MOE_BLOCK) + N_EXPERTS
        padded = (counts + MOE_BLOCK - 1) // MOE_BLOCK * MOE_BLOCK
        padded_ends = jnp.cumsum(padded)
        padded_starts = padded_ends - padded
        onehot = idx[..., None] == jnp.arange(N_EXPERTS, dtype=jnp.int32)
        pos = jnp.sum(jnp.where(onehot, padded_starts, 0), axis=-1) + rank
        block_start = jnp.arange(n_blocks, dtype=jnp.int32) * MOE_BLOCK
        block_expert = jnp.minimum(
            jnp.sum((padded_ends[None, :] <= block_start[:, None]).astype(jnp.int32), axis=1), N_EXPERTS - 1)
        n_used = padded_ends[-1:] // MOE_BLOCK
        pad_len = padded - counts
        misc = jnp.concatenate([jnp.sum(pad_len, keepdims=True), n_used,
                                jnp.full((1,), n_blocks, jnp.int32)])

        xs_rt = _dispatch(padded_starts + counts, pad_len, misc, pos, hn_rt, n_blocks * MOE_BLOCK)
        eids = jnp.arange(N_EXPERTS, dtype=jnp.int32)
        later_nonempty = (eids[None, :] > eids[:, None]) & (counts[None, :] > 0)
        next_expert = jnp.min(jnp.where(later_nonempty, eids[None, :], N_EXPERTS), axis=1)
        next_expert = jnp.where(next_expert < N_EXPERTS, next_expert, -1)
        outs_rt = _experts(block_expert, n_used, next_expert, xs_rt, w_gate_up[layer],
                           b_gate_up[layer].reshape(N_EXPERTS, 1, -1), w_down[layer],
                           b_down[layer].reshape(N_EXPERTS, 1, -1))
        x2 = _combine(pos, outs_rt, h, meta, norm_final_w.reshape(1, d).astype(F32))
    return x2.reshape(b, s, d).astype(x.dtype)
```

```python
import jax
import jax.numpy as jnp
import numpy as np
from jax import lax
from jax.experimental import pallas as pl
from jax.experimental.pallas import tpu as pltpu

F32 = jnp.float32
BF16 = jnp.bfloat16

D_MODEL = 1024
SSM_WIDTH = 1024
SSM_HEAD_DIM = 64
SSM_HEADS = 16
SSM_GROUPS = 2
SSM_STATE = 128
CONV_WIDTH = 4
CONV_DIM = SSM_WIDTH + 2 * SSM_GROUPS * SSM_STATE
LSTM_WIDTH = 1024
LSTM_HEAD_DIM = 128
LSTM_HEADS = 8
N_EXPERTS = 32
TOP_K = 4
D_FF = 1024
SWIGLU_LIMIT = 7.0
SWIGLU_ALPHA = 1.702
MOE_BLOCK = 256
RMS_EPS = 1e-6

LANES = 128
SUBLANES = 8
MIX_CHUNK = 256
ROW_TILE = 512
COMBINE_TILE = 512
VMEM_LIMIT = 56 * 1024 * 1024

GATE_COLS = LANES
GATE_LIVE = SSM_HEADS + 2 * LSTM_HEADS
NEG_INF = float("-inf")


def _cparams(sem):
    return pltpu.CompilerParams(dimension_semantics=sem, vmem_limit_bytes=VMEM_LIMIT)


def _rms(x, w):
    return x * lax.rsqrt(jnp.mean(x * x, axis=-1, keepdims=True) + RMS_EPS) * w


LOG2E = 1.4426950408889634


def _sigmoid(x):
    return 1.0 / (1.0 + jnp.exp2(x * -LOG2E))


def _softplus(x):
    return jnp.maximum(x, 0.0) + jnp.log(1.0 + jnp.exp(-jnp.abs(x)))


def _split3(a):
    hi = a.astype(BF16)
    r = a - hi.astype(F32)
    mid = r.astype(BF16)
    lo = (r - mid.astype(F32)).astype(BF16)
    return hi, mid, lo


def _dot(a, b):
    return jnp.dot(a, b, preferred_element_type=F32)


def _dot_nt(a, b):
    return lax.dot_general(a, b, (((1,), (1,)), ((), ())), preferred_element_type=F32)


def _dot_tn(a, b):
    return lax.dot_general(a, b, (((0,), (0,)), ((), ())), preferred_element_type=F32)


def _sel_dot(sel_bf, a):
    hi, mid, lo = _split3(a)
    return _dot(sel_bf, hi) + _dot(sel_bf, mid) + _dot(sel_bf, lo)


def _expand(a, sel_bf, terms=3):
    out = None
    for piece in _split3(a)[:terms]:
        d = _dot(piece, sel_bf)
        out = d if out is None else out + d
    return out


ROW_SUBTILES = D_MODEL // LANES
ROW_PITCH = ROW_SUBTILES + 1
DMA_PRIORITIES = 2


def _store_row_tiles(ref, x):
    n = x.shape[0]
    for s in range(ROW_SUBTILES):
        ref[pl.ds(s, n, stride=ROW_PITCH), :] = x[:, s * LANES:(s + 1) * LANES]
    ref[pl.ds(ROW_SUBTILES, n, stride=ROW_PITCH), :] = jnp.zeros((n, LANES), ref.dtype)


def _load_row_tile_cols(ref, n):
    return [ref[pl.ds(s, n, stride=ROW_PITCH), :] for s in range(ROW_SUBTILES)]


def _row_copy(src, src_row, dst, dst_row, sem, sublane_rows=ROW_SUBTILES):
    return pltpu.make_async_copy(src.at[pl.ds(src_row * ROW_PITCH, sublane_rows), :],
                                 dst.at[pl.ds(dst_row * ROW_PITCH, sublane_rows), :], sem)


def _wait_row_copies(src, dst, n_rows, sem, sublane_rows=ROW_SUBTILES):
    size = n_rows * sublane_rows
    pltpu.make_async_copy(src.at[pl.ds(0, size), :], dst.at[pl.ds(0, size), :], sem).wait()


_INPROJ_WIDTHS = (SSM_WIDTH, CONV_DIM, LSTM_WIDTH, LSTM_WIDTH, LSTM_WIDTH, LSTM_WIDTH, GATE_COLS)
_C_XBC_END = SSM_WIDTH + CONV_DIM
_C_DT_END = _C_XBC_END + SSM_HEADS
_C_O_END = _C_DT_END + 4 * LSTM_WIDTH
_C_I_END = _C_O_END + LSTM_HEADS
IN_PROJ_DIM = _C_I_END + LSTM_HEADS
REGROUP_ROWS = 128


def _regroup_kernel(w_ref, o_ref):
    rows = w_ref.shape[0]
    big = _C_XBC_END + 4 * LSTM_WIDTH
    o_ref[:, :_C_XBC_END] = w_ref[:, :_C_XBC_END].astype(BF16)
    o_ref[:, _C_XBC_END:big] = w_ref[:, _C_DT_END:_C_O_END].astype(BF16)
    gate_w = jnp.concatenate([w_ref[:, _C_XBC_END:_C_DT_END], w_ref[:, _C_O_END:IN_PROJ_DIM],
                              jnp.zeros((rows, GATE_COLS - GATE_LIVE), F32)], axis=1)
    o_ref[:, big:] = gate_w.astype(BF16)


def _regroup_weights(w_in, layer):
    k = w_in.shape[1]
    ncol = sum(_INPROJ_WIDTHS)
    return pl.pallas_call(
        _regroup_kernel,
        grid=(k // REGROUP_ROWS,),
        in_specs=[pl.BlockSpec((None, REGROUP_ROWS, IN_PROJ_DIM), lambda i: (layer, i, 0))],
        out_specs=pl.BlockSpec((REGROUP_ROWS, ncol), lambda i: (i, 0)),
        out_shape=jax.ShapeDtypeStruct((k, ncol), BF16),
        compiler_params=_cparams(("arbitrary",)),
        name="regroup_weights",
    )(w_in)


def _inproj_kernel(x_ref, nw_ref, w_ref, z_ref, xbc_ref, q_ref, k_ref, v_ref, o_ref, g_ref):
    xb = _rms(x_ref[...], nw_ref[...]).astype(BF16)
    off = 0
    for ref, width in zip((z_ref, xbc_ref, q_ref, k_ref, v_ref, o_ref, g_ref), _INPROJ_WIDTHS):
        ref[...] = _dot(xb, w_ref[:, off:off + width]).astype(ref.dtype)
        off += width


def _inproj(x2, nw, w_all):
    t = x2.shape[0]
    tm = min(ROW_TILE, t)
    ncol = w_all.shape[1]
    out_shape = [jax.ShapeDtypeStruct((t, w), BF16) for w in _INPROJ_WIDTHS[:-1]]
    out_shape.append(jax.ShapeDtypeStruct((t, GATE_COLS), F32))
    return pl.pallas_call(
        _inproj_kernel,
        grid=(t // tm,),
        in_specs=[
            pl.BlockSpec((tm, D_MODEL), lambda i: (i, 0)),
            pl.BlockSpec((1, D_MODEL), lambda i: (0, 0)),
            pl.BlockSpec((D_MODEL, ncol), lambda i: (0, 0), pipeline_mode=pl.Buffered(1)),
        ],
        out_specs=[pl.BlockSpec((tm, w), lambda i: (i, 0)) for w in _INPROJ_WIDTHS],
        out_shape=out_shape,
        compiler_params=_cparams(("arbitrary",)),
        name="inproj",
    )(x2, nw, w_all)


SSD_PAIR_GROUP = 2
LSTM_HEAD_GROUP = 2


def _ssd_phases(xbc_ref, z_ref, g_ref, convw_ref, convb_ref, dtb_ref, aneg_ref, dskip_ref, nw_ref,
                sel_ref, y_ref, u_scr, st_scr):
    L = xbc_ref.shape[0]
    gw = SSM_WIDTH // SSM_GROUPS

    row = lax.broadcasted_iota(jnp.int32, (L, L), 0)
    col = lax.broadcasted_iota(jnp.int32, (L, L), 1)
    causal = row >= col

    xin_bf = xbc_ref[...]
    xin = xin_bf.astype(F32)
    halo = u_scr[...]
    row8 = lax.broadcasted_iota(jnp.int32, (SUBLANES, 1), 0)
    acc = convb_ref[...] + convw_ref[CONV_WIDTH - 1:CONV_WIDTH, :] * xin
    for back in range(1, CONV_WIDTH):
        shifted = _dot((row - col == back).astype(F32).astype(BF16), xin_bf)
        head = shifted[:SUBLANES] + jnp.where(row8 < back, pltpu.roll(halo, back, axis=0), 0.0)
        shifted = jnp.concatenate([head, shifted[SUBLANES:]], axis=0)
        acc = acc + convw_ref[CONV_WIDTH - 1 - back:CONV_WIDTH - back, :] * shifted
    u_scr[...] = xin[L - SUBLANES:, :]
    xbc = acc * _sigmoid(acc)
    xs = xbc[:, :SSM_WIDTH]
    bm = xbc[:, SSM_WIDTH:SSM_WIDTH + SSM_GROUPS * SSM_STATE].astype(BF16)
    cm = xbc[:, SSM_WIDTH + SSM_GROUPS * SSM_STATE:].astype(BF16)
    yield

    tril_bf = causal.astype(F32).astype(BF16)

    lane = lax.broadcasted_iota(jnp.int32, (1, LANES), 1)
    dt = jnp.where(lane < SSM_HEADS, _softplus(g_ref[...] + dtb_ref[...]), 0.0)
    acs = _sel_dot(tril_bf, dt * aneg_ref[...]) * LOG2E
    acs_t = acs.T
    sel = sel_ref[...]
    acs_last = acs[L - 1:L, :]
    ea_x = _expand(jnp.exp2(acs), sel, terms=2)
    xd = xs * _expand(dt, sel, terms=1)
    xd_bf = xd.astype(BF16)
    xw = (xd * _expand(jnp.exp2(acs_last - acs), sel, terms=1)).astype(BF16)
    yield

    cbs = [_dot_nt(cm[:, g * SSM_STATE:(g + 1) * SSM_STATE], bm[:, g * SSM_STATE:(g + 1) * SSM_STATE])
           for g in range(SSM_GROUPS)]
    yoff = [_dot(cm[:, g * SSM_STATE:(g + 1) * SSM_STATE], st_scr[g].astype(BF16)) for g in range(SSM_GROUPS)]
    upd = [_dot_tn(bm[:, g * SSM_STATE:(g + 1) * SSM_STATE], xw[:, g * gw:(g + 1) * gw])
           for g in range(SSM_GROUPS)]
    for g in range(SSM_GROUPS):
        st_scr[g] = st_scr[g] * ea_x[L - 1:L, g * gw:(g + 1) * gw] + upd[g]
    yield

    pair_lane = lax.broadcasted_iota(jnp.int32, (1, LANES), 1)
    heads_per_group = SSM_HEADS // SSM_GROUPS
    rhs = []
    for j in range(SSM_HEADS // 2):
        xpair = xd_bf[:, j * LANES:(j + 1) * LANES]
        zero = jnp.zeros_like(xpair)
        rhs.append(jnp.concatenate([jnp.where(pair_lane < SSM_HEAD_DIM, xpair, zero),
                                    jnp.where(pair_lane >= SSM_HEAD_DIM, xpair, zero)], axis=0))
    ydiag = []
    for j0 in range(0, SSM_HEADS // 2, SSD_PAIR_GROUP):
        pairs = range(j0, j0 + SSD_PAIR_GROUP)
        ms = {h: (cbs[h // heads_per_group]
                  * jnp.exp2(jnp.where(causal, acs[:, h:h + 1] - acs_t[h:h + 1, :], NEG_INF))).astype(BF16)
              for j in pairs for h in (2 * j, 2 * j + 1)}
        yield
        ydiag += [_dot(jnp.concatenate([ms[2 * j], ms[2 * j + 1]], axis=1), rhs[j]) for j in pairs]
        yield
    y = jnp.concatenate(ydiag, axis=1) + jnp.concatenate(yoff, axis=1) * ea_x + dskip_ref[...] * xs

    zz = z_ref[...].astype(F32)
    y = y * (zz * _sigmoid(zz))
    y_ref[...] = _rms(y, nw_ref[...]).astype(BF16)


def _mlstm_phases(q_ref, k_ref, v_ref, o_ref, g_ref, ib_ref, fb_ref, nw_ref, y_ref,
                  st_scr, m_scr):
    L = q_ref.shape[0]
    dh = LSTM_HEAD_DIM
    scale = dh ** -0.5

    row = lax.broadcasted_iota(jnp.int32, (L, L), 0)
    col = lax.broadcasted_iota(jnp.int32, (L, L), 1)
    causal = row >= col
    tril_bf = causal.astype(F32).astype(BF16)
    lane = lax.broadcasted_iota(jnp.int32, (1, LANES), 1)
    live = lane < LSTM_HEADS

    gates = g_ref[...]
    gi = pltpu.roll(gates, LANES - SSM_HEADS, axis=1)
    gf = pltpu.roll(gates, LANES - SSM_HEADS - LSTM_HEADS, axis=1)
    ii = jnp.where(live, gi + ib_ref[...], 0.0)
    logf = jnp.where(live, -_softplus(-(gf + fb_ref[...])), 0.0)
    cumf = _sel_dot(tril_bf, logf)
    g = ii - cumf
    rid = lax.broadcasted_iota(jnp.int32, (L, LANES), 0)
    cmx = g
    step = 1
    while step < L:
        cmx = jnp.maximum(cmx, jnp.where(rid >= step, pltpu.roll(cmx, step, axis=0), NEG_INF))
        step *= 2
    m_prev = m_scr[...]
    mx = jnp.maximum(m_prev, cmx)
    w_inter = jnp.exp(m_prev - mx)
    enm = jnp.exp(-(cumf + mx))
    g2_t = (g * LOG2E).T
    mx2 = mx * LOG2E - np.log2(scale)
    m_last = mx[L - 1:L, :]
    wk = jnp.exp(g - m_last) * scale
    sc = jnp.exp(m_prev - m_last)
    m_scr[...] = cumf[L - 1:L, :] + m_last
    yield

    ones_bf = jnp.ones((L, dh), BF16)
    heads = range(LSTM_HEADS)
    hsl = [slice(h * dh, (h + 1) * dh) for h in heads]
    vaug = [jnp.concatenate([v_ref[:, hsl[h]], ones_bf], axis=1) for h in heads]
    r2 = [_dot(q_ref[:, hsl[h]], st_scr[h].astype(BF16)) for h in heads]
    upd = [_dot_tn((k_ref[:, hsl[h]].astype(F32) * wk[:, h:h + 1]).astype(BF16), vaug[h]) for h in heads]
    for h in heads:
        st_scr[h] = st_scr[h] * sc[:, h:h + 1] + upd[h]
    yield
    r1 = []
    for h0 in range(0, LSTM_HEADS, LSTM_HEAD_GROUP):
        group = range(h0, h0 + LSTM_HEAD_GROUP)
        s_qk = {h: _dot_nt(q_ref[:, hsl[h]], k_ref[:, hsl[h]]) for h in group}
        yield
        p = {h: (s_qk[h] * jnp.exp2(jnp.where(causal, g2_t[h:h + 1, :] - mx2[:, h:h + 1], NEG_INF))).astype(BF16)
             for h in group}
        yield
        r1 += [_dot(p[h], vaug[h]) for h in group]
        yield
    hh = []
    for h in heads:
        wcol = w_inter[:, h:h + 1]
        num = r1[h][:, :dh] + r2[h][:, :dh] * wcol
        den = r1[h][:, dh:] + r2[h][:, dh:] * wcol
        hh.append(num / jnp.maximum(jnp.abs(den), enm[:, h:h + 1]))
    inv = [lax.rsqrt(jnp.mean(hh[h] * hh[h], axis=-1, keepdims=True) + RMS_EPS) for h in heads]
    for h in heads:
        oo = o_ref[:, hsl[h]].astype(F32)
        y_ref[:, hsl[h]] = (_sigmoid(oo) * (hh[h] * inv[h] * nw_ref[:, hsl[h]])).astype(BF16)


N_SSD_IN, N_LSTM_IN = 10, 8


def _mixer_kernel(*refs):
    ssd_in = refs[:N_SSD_IN]
    lstm_in = refs[N_SSD_IN:N_SSD_IN + N_LSTM_IN]
    y_ssd, y_lstm, u_scr, st_ssd, st_lstm, m_scr = refs[N_SSD_IN + N_LSTM_IN:]

    @pl.when(pl.program_id(1) == 0)
    def _():
        u_scr[...] = jnp.zeros_like(u_scr)
        st_ssd[...] = jnp.zeros_like(st_ssd)
        st_lstm[...] = jnp.zeros_like(st_lstm)
        m_scr[...] = jnp.zeros_like(m_scr)

    streams = [_ssd_phases(*ssd_in, y_ssd, u_scr, st_ssd), _mlstm_phases(*lstm_in, y_lstm, st_lstm, m_scr)]
    while streams:
        for stream in list(streams):
            if next(stream, StopIteration) is StopIteration:
                streams.remove(stream)


def _mixers(xbc, z, gates, convw, convb, dtb, aneg, dskip_x, nw_ssd, sel, q, k, v, o, ib, fb, nw_lstm, b, s):
    L = min(MIX_CHUNK, s)
    nc = s // L
    tok = lambda bi, ci: (bi * nc + ci, 0)
    const = lambda bi, ci: (0, 0)
    big = pl.BlockSpec((L, LSTM_WIDTH), tok)
    gate_block = lambda j: pl.BlockSpec((L, LANES), lambda bi, ci: (bi * nc + ci, j))
    vec = lambda w: pl.BlockSpec((1, w), const)
    in_specs = [pl.BlockSpec((L, CONV_DIM), tok), pl.BlockSpec((L, SSM_WIDTH), tok), gate_block(0),
                pl.BlockSpec((CONV_WIDTH, CONV_DIM), const), vec(CONV_DIM), vec(LANES), vec(LANES),
                vec(SSM_WIDTH), vec(SSM_WIDTH), pl.BlockSpec((LANES, SSM_WIDTH), const),
                big, big, big, big, gate_block(0), vec(LANES), vec(LANES), vec(LSTM_WIDTH)]
    assert len(in_specs) == N_SSD_IN + N_LSTM_IN
    return pl.pallas_call(
        _mixer_kernel,
        grid=(b, nc),
        in_specs=in_specs,
        out_specs=[pl.BlockSpec((L, SSM_WIDTH), tok), big],
        out_shape=[jax.ShapeDtypeStruct((b * s, SSM_WIDTH), BF16), jax.ShapeDtypeStruct((b * s, LSTM_WIDTH), BF16)],
        scratch_shapes=[pltpu.VMEM((SUBLANES, CONV_DIM), F32),
                        pltpu.VMEM((SSM_GROUPS, SSM_STATE, SSM_WIDTH // SSM_GROUPS), F32),
                        pltpu.VMEM((LSTM_HEADS, LSTM_HEAD_DIM, 2 * LSTM_HEAD_DIM), F32),
                        pltpu.VMEM((1, LANES), F32)],
        compiler_params=_cparams(("arbitrary", "arbitrary")),
        name="mixers",
    )(xbc, z, gates, convw, convb, dtb, aneg, dskip_x, nw_ssd, sel, q, k, v, o, gates, ib, fb, nw_lstm)


META_IDX, META_GATE, META_RANK = 0, TOP_K, 2 * TOP_K
META_ROWS = 2 * SUBLANES


def _outproj_kernel(ys_ref, yl_ref, x_ref, wo_ref, nw_ref, wr_ref, br_ref, h_ref, hn_ref, meta_ref,
                    meta_t_ref, cnt_ref, cnt_scr, logit_scr):
    tm = x_ref.shape[0]
    step = pl.program_id(0)

    @pl.when(step == 0)
    def _():
        cnt_scr[...] = jnp.zeros_like(cnt_scr)
        logit_scr[...] = jnp.zeros_like(logit_scr)

    vals = logit_scr[...]
    routed = (step > 0).astype(F32)

    h = (x_ref[...] + _dot(ys_ref[...], wo_ref[:SSM_WIDTH, :]) + _dot(yl_ref[...], wo_ref[SSM_WIDTH:, :]))
    h_ref[...] = h
    hn = _rms(h, nw_ref[...])
    _store_row_tiles(hn_ref, hn)
    logit_scr[...] = _dot(hn.astype(BF16), wr_ref[...]) + br_ref[...]

    lane = lax.broadcasted_iota(jnp.int32, (tm, LANES), 1)
    member = jnp.zeros((tm, LANES), F32)
    tops, idxs, sels = [], [], []
    for _ in range(TOP_K):
        m = jnp.max(vals, axis=-1, keepdims=True)
        idx = jnp.min(jnp.where(vals == m, lane, LANES), axis=-1, keepdims=True)
        sel = lane == idx
        vals = jnp.where(sel, NEG_INF, vals)
        member = member + sel.astype(F32)
        tops.append(m)
        idxs.append(idx)
        sels.append(sel)
    es = [jnp.exp(t - tops[0]) for t in tops]
    inv = 1.0 / (es[0] + es[1] + es[2] + es[3])
    member = member * routed

    r = lax.broadcasted_iota(jnp.int32, (tm, tm), 0)
    c = lax.broadcasted_iota(jnp.int32, (tm, tm), 1)
    strict = (r > c).astype(F32).astype(BF16)
    carry = cnt_scr[0:1, :]
    rank_all = _dot(strict, member.astype(BF16)) + carry
    total = carry + jnp.sum(member, axis=0, keepdims=True)
    cnt_scr[...] = jnp.broadcast_to(total, cnt_scr.shape)
    cnt_ref[...] = jnp.broadcast_to(total, cnt_ref.shape)

    meta = jnp.zeros((tm, LANES), F32)
    for kk in range(TOP_K):
        rank = jnp.sum(jnp.where(sels[kk], rank_all, 0.0), axis=-1, keepdims=True)
        meta = jnp.where(lane == META_IDX + kk, idxs[kk].astype(F32), meta)
        meta = jnp.where(lane == META_GATE + kk, es[kk] * inv, meta)
        meta = jnp.where(lane == META_RANK + kk, rank, meta)
    meta_ref[...] = meta
    eye = (lax.broadcasted_iota(jnp.int32, (META_ROWS, LANES), 0)
           == lax.broadcasted_iota(jnp.int32, (META_ROWS, LANES), 1)).astype(F32).astype(BF16)
    pieces = _split3(meta)
    meta_t_ref[...] = _dot_nt(eye, pieces[0]) + _dot_nt(eye, pieces[1]) + _dot_nt(eye, pieces[2])


def _outproj(ys, yl, x2, wo, nw, wr, br):
    t = x2.shape[0]
    tm = min(ROW_TILE, t)
    n = t // tm
    projected = lambda i: (jnp.minimum(i, n - 1), 0)
    routed = lambda i: (jnp.maximum(i - 1, 0), 0)
    tokspec = lambda w: pl.BlockSpec((tm, w), projected)
    const = lambda i: (0, 0)
    return pl.pallas_call(
        _outproj_kernel,
        grid=(n + 1,),
        in_specs=[tokspec(SSM_WIDTH), tokspec(LSTM_WIDTH), tokspec(D_MODEL),
                  pl.BlockSpec((SSM_WIDTH + LSTM_WIDTH, D_MODEL), const, pipeline_mode=pl.Buffered(1)),
                  pl.BlockSpec((1, D_MODEL), const),
                  pl.BlockSpec((D_MODEL, LANES), const),
                  pl.BlockSpec((1, LANES), const)],
        out_specs=[tokspec(D_MODEL), pl.BlockSpec((tm * ROW_PITCH, LANES), projected),
                   pl.BlockSpec((tm, LANES), routed),
                   pl.BlockSpec((META_ROWS, tm), lambda i: (0, jnp.maximum(i - 1, 0))),
                   pl.BlockSpec((SUBLANES, LANES), const)],
        out_shape=[jax.ShapeDtypeStruct((t, D_MODEL), F32), jax.ShapeDtypeStruct((t * ROW_PITCH, LANES), F32),
                   jax.ShapeDtypeStruct((t, LANES), F32), jax.ShapeDtypeStruct((META_ROWS, t), F32),
                   jax.ShapeDtypeStruct((SUBLANES, LANES), F32)],
        scratch_shapes=[pltpu.VMEM((SUBLANES, LANES), F32), pltpu.VMEM((tm, LANES), F32)],
        compiler_params=_cparams(("arbitrary",)),
        name="outproj_router",
    )(ys, yl, x2, wo, nw, wr, br)


DISPATCH_TILE = 1024
DISPATCH_GROUP = 4
PAD_CHUNKS = (128, 64, 32, 16, 8, 4, 2, 1)


def _dispatch_kernel(padstart_ref, padlen_ref, misc_ref, pos_ref, hn_ref, xs_hbm, sem, zsem, zeros_scr):
    i = pl.program_id(0)
    td = pos_ref.shape[-1] // TOP_K

    @pl.when(i == 0)
    def _():
        zeros_scr[...] = jnp.zeros_like(zeros_scr)
        for e in range(N_EXPERTS):
            row = padstart_ref[e]
            nrow = padlen_ref[e]
            for chunk in PAD_CHUNKS:
                @pl.when((nrow & chunk) != 0)
                def _(row=row, chunk=chunk):
                    pltpu.make_async_copy(
                        zeros_scr.at[pl.ds(0, chunk * ROW_PITCH), :],
                        xs_hbm.at[pl.ds(row * ROW_PITCH, chunk * ROW_PITCH), :],
                        zsem).start()
                row = row + (nrow & chunk)

        def tail_copy(blk):
            return pltpu.make_async_copy(
                zeros_scr,
                xs_hbm.at[pl.ds(blk * (MOE_BLOCK * ROW_PITCH), MOE_BLOCK * ROW_PITCH), :], zsem)

        n_used, n_blocks = misc_ref[1], misc_ref[2]

        def start_tail(blk, carry):
            tail_copy(blk).start()
            return carry
        lax.fori_loop(n_used, n_blocks, start_tail, 0)

        def drain_row(j, carry):
            _wait_row_copies(zeros_scr, xs_hbm, 1, zsem, ROW_PITCH)
            return carry
        lax.fori_loop(0, misc_ref[0], drain_row, 0)

        def drain_tail(blk, carry):
            tail_copy(blk).wait()
            return carry
        lax.fori_loop(n_used, n_blocks, drain_tail, 0)

    def body(g, carry):
        r0 = g * DISPATCH_GROUP
        slots = [pos_ref[0, 0, (j % TOP_K) * td + r0 + j // TOP_K] for j in range(DISPATCH_GROUP * TOP_K)]
        for j, slot in enumerate(slots):
            _row_copy(hn_ref, r0 + j // TOP_K, xs_hbm, slot, sem, ROW_PITCH).start(priority=j % DMA_PRIORITIES)
        return carry
    lax.fori_loop(0, td // DISPATCH_GROUP, body, 0)

    for _ in range(TOP_K):
        _wait_row_copies(hn_ref, xs_hbm, td, sem, ROW_PITCH)


def _tile_slots(pos, tile):
    t = pos.shape[1]
    return pos.reshape(TOP_K, t // tile, tile).transpose(1, 0, 2).reshape(t // tile, 1, TOP_K * tile)


def _dispatch(pad_start, pad_len, misc, pos, hn_rt, n_rows):
    t = pos.shape[1]
    td = min(DISPATCH_TILE, t)
    n = t // td
    grid_spec = pltpu.PrefetchScalarGridSpec(
        num_scalar_prefetch=3,
        grid=(n,),
        in_specs=[pl.BlockSpec((1, 1, TOP_K * td), lambda i, *_: (i, 0, 0), memory_space=pltpu.SMEM),
                  pl.BlockSpec((td * ROW_PITCH, LANES), lambda i, *_: (i, 0))],
        out_specs=pl.BlockSpec(memory_space=pl.ANY),
        scratch_shapes=[pltpu.SemaphoreType.DMA(()), pltpu.SemaphoreType.DMA(()),
                        pltpu.VMEM((MOE_BLOCK * ROW_PITCH, LANES), F32)],
    )
    return pl.pallas_call(
        _dispatch_kernel,
        grid_spec=grid_spec,
        out_shape=jax.ShapeDtypeStruct((n_rows * ROW_PITCH, LANES), F32),
        compiler_params=_cparams(("arbitrary",)),
        name="dispatch",
    )(pad_start, pad_len, misc, _tile_slots(pos, td), hn_rt)


def _experts_kernel(be_ref, nused_ref, next_ref, x_ref, wgu_hbm, bgu_ref, wdn_hbm, bdn_ref, out_ref,
                    wgu_bf, wdn_bf, wgu_st, wdn_st, wsem, slot_ref):
    i = pl.program_id(0)
    n_used = nused_ref[0]

    def weight_copies(e, slot):
        return (pltpu.make_async_copy(wgu_hbm.at[e], wgu_st.at[slot], wsem.at[0, slot]),
                pltpu.make_async_copy(wdn_hbm.at[e], wdn_st.at[slot], wsem.at[1, slot]))

    @pl.when(i == 0)
    def _():
        slot_ref[0] = 0
        for cp in weight_copies(be_ref[0], 0):
            cp.start()

    @pl.when(i < n_used)
    def _():
        @pl.when(jnp.logical_or(i == 0, be_ref[i] != be_ref[jnp.maximum(i - 1, 0)]))
        def _():
            e = be_ref[i]
            slot = slot_ref[0]
            for cp in weight_copies(e, slot):
                cp.wait()
            wgu_bf[...] = wgu_st[slot].astype(BF16)
            wdn_bf[...] = wdn_st[slot].astype(BF16)
            nxt = next_ref[e]

            @pl.when(nxt >= 0)
            def _():
                for cp in weight_copies(nxt, 1 - slot):
                    cp.start()
            slot_ref[0] = 1 - slot

        xb = jnp.concatenate(_load_row_tile_cols(x_ref, MOE_BLOCK), axis=1).astype(BF16)
        hh = _dot(xb, wgu_bf[...]) + bgu_ref[...]
        gh = jnp.minimum(hh[:, :D_FF], SWIGLU_LIMIT)
        uh = jnp.clip(hh[:, D_FF:], -SWIGLU_LIMIT, SWIGLU_LIMIT)
        act = (uh + 1.0) * (gh * _sigmoid(SWIGLU_ALPHA * gh))
        _store_row_tiles(out_ref, _dot(act.astype(BF16), wdn_bf[...]) + bdn_ref[...])

    @pl.when(i >= n_used)
    def _():
        out_ref[...] = jnp.zeros_like(out_ref)


def _experts(block_expert, n_used, next_expert, xs_rt, wgu, bgu3, wdn, bdn3):
    n_blocks = block_expert.shape[0]
    blk = MOE_BLOCK * ROW_PITCH
    grid_spec = pltpu.PrefetchScalarGridSpec(
        num_scalar_prefetch=3,
        grid=(n_blocks,),
        in_specs=[
            pl.BlockSpec((blk, LANES), lambda i, be, nu, nx: (i, 0)),
            pl.BlockSpec(memory_space=pl.ANY),
            pl.BlockSpec((None, 1, 2 * D_FF), lambda i, be, nu, nx: (be[i], 0, 0)),
            pl.BlockSpec(memory_space=pl.ANY),
            pl.BlockSpec((None, 1, D_MODEL), lambda i, be, nu, nx: (be[i], 0, 0)),
        ],
        out_specs=pl.BlockSpec((blk, LANES), lambda i, be, nu, nx: (i, 0)),
        scratch_shapes=[pltpu.VMEM((D_MODEL, 2 * D_FF), BF16),
                        pltpu.VMEM((D_FF, D_MODEL), BF16),
                        pltpu.VMEM((2, D_MODEL, 2 * D_FF), F32),
                        pltpu.VMEM((2, D_FF, D_MODEL), F32),
                        pltpu.SemaphoreType.DMA((2, 2)),
                        pltpu.SMEM((1,), jnp.int32)],
    )
    return pl.pallas_call(
        _experts_kernel,
        grid_spec=grid_spec,
        out_shape=jax.ShapeDtypeStruct((n_blocks * blk, LANES), F32),
        compiler_params=_cparams(("arbitrary",)),
        name="experts",
    )(block_expert, n_used, next_expert, xs_rt, wgu, bgu3, wdn, bdn3)


def _combine_kernel(pos_cur, pos_next, outs_hbm, h_ref, meta_ref, nw_ref, y_ref, buf, sem):
    i = pl.program_id(0)
    n = pl.num_programs(0)
    tc = h_ref.shape[0]
    slot = i % 2

    def issue(pos_ref, s):
        def body(r, carry):
            for kk in range(TOP_K):
                _row_copy(outs_hbm, pos_ref[0, 0, kk * tc + r], buf.at[s, kk], r,
                          sem.at[s]).start(priority=kk % DMA_PRIORITIES)
            return carry
        lax.fori_loop(0, tc, body, 0, unroll=4)

    @pl.when(i == 0)
    def _():
        issue(pos_cur, 0)

    @pl.when(i + 1 < n)
    def _():
        issue(pos_next, 1 - slot)

    for kk in range(TOP_K):
        _wait_row_copies(outs_hbm, buf.at[slot, kk], tc, sem.at[slot])
    meta = meta_ref[...]
    gates = [meta[:, META_GATE + kk:META_GATE + kk + 1] for kk in range(TOP_K)]
    rows = [_load_row_tile_cols(buf.at[slot, kk], tc) for kk in range(TOP_K)]
    cols = []
    for s in range(ROW_SUBTILES):
        acc = h_ref[:, s * LANES:(s + 1) * LANES]
        for kk in range(TOP_K):
            acc = acc + gates[kk] * rows[kk][s]
        cols.append(acc)
    y_ref[...] = _rms(jnp.concatenate(cols, axis=1), nw_ref[...])


def _combine(pos, outs_rt, h, meta, nw):
    t = h.shape[0]
    tc = min(COMBINE_TILE, t)
    n = t // tc
    pos3 = _tile_slots(pos, tc)
    return pl.pallas_call(
        _combine_kernel,
        grid=(n,),
        in_specs=[
            pl.BlockSpec((1, 1, TOP_K * tc), lambda i: (i, 0, 0), memory_space=pltpu.SMEM),
            pl.BlockSpec((1, 1, TOP_K * tc), lambda i: (jnp.minimum(i + 1, n - 1), 0, 0),
                         memory_space=pltpu.SMEM),
            pl.BlockSpec(memory_space=pl.ANY),
            pl.BlockSpec((tc, D_MODEL), lambda i: (i, 0)),
            pl.BlockSpec((tc, LANES), lambda i: (i, 0)),
            pl.BlockSpec((1, D_MODEL), lambda i: (0, 0)),
        ],
        out_specs=pl.BlockSpec((tc, D_MODEL), lambda i: (i, 0)),
        out_shape=jax.ShapeDtypeStruct((t, D_MODEL), F32),
        scratch_shapes=[pltpu.VMEM((2, TOP_K, tc * ROW_PITCH, LANES), F32), pltpu.SemaphoreType.DMA((2,))],
        compiler_params=_cparams(("arbitrary",)),
        name="combine",
    )(pos3, pos3, outs_rt, h, meta, nw)


def _pad_lanes(v, fill=0.0):
    v = v.astype(F32).reshape(1, -1)
    return jnp.pad(v, ((0, 0), (0, LANES - v.shape[1])), constant_values=fill)


def kernel(x, norm_mix_w, w_in, conv_w, conv_b, dt_bias, a_log, d_skip, ssm_norm_w, lstm_i_bias,
           lstm_f_bias, lstm_norm_w, w_out, norm_ffn_w, w_router, b_router, w_gate_up, b_gate_up,
           w_down, b_down, norm_final_w):
    b, s, d = x.shape
    t = b * s
    x2 = x.reshape(t, d).astype(F32)
    depth = w_in.shape[0]
    assert depth == 1, "the combine kernel fuses the final norm, so exactly one layer is supported"
    for layer in range(depth):
        w_all = _regroup_weights(w_in.astype(F32), layer)
        sel = (jnp.arange(LANES)[:, None] == (jnp.arange(SSM_WIDTH) // SSM_HEAD_DIM)[None, :]).astype(BF16)
        a_neg = _pad_lanes(-jnp.exp(a_log[layer].astype(F32)))
        dskip_x = jnp.repeat(d_skip[layer].astype(F32), SSM_HEAD_DIM).reshape(1, SSM_WIDTH)
        wr = jnp.pad(w_router[layer], ((0, 0), (0, LANES - N_EXPERTS))).astype(BF16)
        br = _pad_lanes(b_router[layer], fill=NEG_INF)

        z, xbc, q, k, v, o, gates = _inproj(x2, norm_mix_w[layer].reshape(1, d).astype(F32), w_all)
        y_ssd, y_lstm = _mixers(
            xbc, z, gates, conv_w[layer].astype(F32), conv_b[layer].reshape(1, -1).astype(F32),
            _pad_lanes(dt_bias[layer]), a_neg, dskip_x, ssm_norm_w[layer].reshape(1, -1).astype(F32), sel,
            q, k, v, o, _pad_lanes(lstm_i_bias[layer]), _pad_lanes(lstm_f_bias[layer]),
            lstm_norm_w[layer].reshape(1, -1).astype(F32), b, s)
        h, hn_rt, meta, meta_t, cnt = _outproj(y_ssd, y_lstm, x2, w_out[layer].astype(BF16),
                                               norm_ffn_w[layer].reshape(1, d).astype(F32), wr, br)

        idx = meta_t[META_IDX:META_IDX + TOP_K].astype(jnp.int32)
        rank = meta_t[META_RANK:META_RANK + TOP_K].astype(jnp.int32)
        counts = cnt[0, :N_EXPERTS].astype(jnp.int32)
        n_blocks = -(-(t * TOP_K) // MOE_BLOCK) + N_EXPERTS
        padded = (counts + MOE_BLOCK - 1) // MOE_BLOCK * MOE_BLOCK
        padded_ends = jnp.cumsum(padded)
        padded_starts = padded_ends - padded
        onehot = idx[..., None] == jnp.arange(N_EXPERTS, dtype=jnp.int32)
        pos = jnp.sum(jnp.where(onehot, padded_starts, 0), axis=-1) + rank
        block_start = jnp.arange(n_blocks, dtype=jnp.int32) * MOE_BLOCK
        block_expert = jnp.minimum(
            jnp.sum((padded_ends[None, :] <= block_start[:, None]).astype(jnp.int32), axis=1), N_EXPERTS - 1)
        n_used = padded_ends[-1:] // MOE_BLOCK
        pad_len = padded - counts
        misc = jnp.concatenate([jnp.sum(pad_len, keepdims=True), n_used,
                                jnp.full((1,), n_blocks, jnp.int32)])

        xs_rt = _dispatch(padded_starts + counts, pad_len, misc, pos, hn_rt, n_blocks * MOE_BLOCK)
        eids = jnp.arange(N_EXPERTS, dtype=jnp.int32)
        later_nonempty = (eids[None, :] > eids[:, None]) & (counts[None, :] > 0)
        next_expert = jnp.min(jnp.where(later_nonempty, eids[None, :], N_EXPERTS), axis=1)
        next_expert = jnp.where(next_expert < N_EXPERTS, next_expert, -1)
        outs_rt = _experts(block_expert, n_used, next_expert, xs_rt, w_gate_up[layer],
                           b_gate_up[layer].reshape(N_EXPERTS, 1, -1), w_down[layer],
                           b_down[layer].reshape(N_EXPERTS, 1, -1))
        x2 = _combine(pos, outs_rt, h, meta, norm_final_w.reshape(1, d).astype(F32))
    return x2.reshape(b, s, d).astype(x.dtype)
```

```python
import functools

import jax
import jax.numpy as jnp
import numpy as np
from jax import lax
from jax.experimental import pallas as pl
from jax.experimental.pallas import tpu as pltpu

F32 = jnp.float32
BF16 = jnp.bfloat16

D_MODEL = 1024
SSM_WIDTH = 1024
SSM_HEAD_DIM = 64
SSM_HEADS = 16
SSM_GROUPS = 2
SSM_STATE = 128
CONV_WIDTH = 4
CONV_DIM = SSM_WIDTH + 2 * SSM_GROUPS * SSM_STATE
LSTM_WIDTH = 1024
LSTM_HEAD_DIM = 128
LSTM_HEADS = 8
N_EXPERTS = 32
TOP_K = 4
D_FF = 1024
SWIGLU_LIMIT = 7.0
SWIGLU_ALPHA = 1.702
MOE_BLOCK = 256
RMS_EPS = 1e-6

LANES = 128
SUBLANES = 8
MIX_CHUNK = 256
ROW_TILE = 512
COMBINE_TILE = 512
VMEM_LIMIT = 56 * 1024 * 1024

GATE_COLS = LANES
GATE_LIVE = SSM_HEADS + 2 * LSTM_HEADS
NEG_INF = float("-inf")


def _cparams(sem):
    return pltpu.CompilerParams(dimension_semantics=sem, vmem_limit_bytes=VMEM_LIMIT)


def _rms(x, w):
    return x * lax.rsqrt(jnp.mean(x * x, axis=-1, keepdims=True) + RMS_EPS) * w


LOG2E = 1.4426950408889634


def _sigmoid(x):
    return 1.0 / (1.0 + jnp.exp2(x * -LOG2E))


def _softplus(x):
    return jnp.maximum(x, 0.0) + jnp.log(1.0 + jnp.exp(-jnp.abs(x)))


def _split3(a):
    hi = a.astype(BF16)
    r = a - hi.astype(F32)
    mid = r.astype(BF16)
    lo = (r - mid.astype(F32)).astype(BF16)
    return hi, mid, lo


def _dot(a, b):
    return jnp.dot(a, b, preferred_element_type=F32)


def _dot_nt(a, b):
    return lax.dot_general(a, b, (((1,), (1,)), ((), ())), preferred_element_type=F32)


def _dot_tn(a, b):
    return lax.dot_general(a, b, (((0,), (0,)), ((), ())), preferred_element_type=F32)


def _sel_dot(sel_bf, a):
    hi, mid, lo = _split3(a)
    return _dot(sel_bf, hi) + _dot(sel_bf, mid) + _dot(sel_bf, lo)


def _expand(a, sel_bf, terms=3):
    out = None
    for piece in _split3(a)[:terms]:
        d = _dot(piece, sel_bf)
        out = d if out is None else out + d
    return out


ROW_SUBTILES = D_MODEL // LANES
ROW_PITCH = ROW_SUBTILES + 1
DMA_PRIORITIES = 2


def _store_row_tiles(ref, x):
    n = x.shape[0]
    for s in range(ROW_SUBTILES):
        ref[pl.ds(s, n, stride=ROW_PITCH), :] = x[:, s * LANES:(s + 1) * LANES]
    ref[pl.ds(ROW_SUBTILES, n, stride=ROW_PITCH), :] = jnp.zeros((n, LANES), ref.dtype)


def _load_row_tile_cols(ref, n):
    return [ref[pl.ds(s, n, stride=ROW_PITCH), :] for s in range(ROW_SUBTILES)]


def _row_copy(src, src_row, dst, dst_row, sem, sublane_rows=ROW_SUBTILES):
    return pltpu.make_async_copy(src.at[pl.ds(src_row * ROW_PITCH, sublane_rows), :],
                                 dst.at[pl.ds(dst_row * ROW_PITCH, sublane_rows), :], sem)


def _wait_row_copies(src, dst, n_rows, sem, sublane_rows=ROW_SUBTILES):
    size = n_rows * sublane_rows
    pltpu.make_async_copy(src.at[pl.ds(0, size), :], dst.at[pl.ds(0, size), :], sem).wait()


_INPROJ_WIDTHS = (SSM_WIDTH, CONV_DIM, LSTM_WIDTH, LSTM_WIDTH, LSTM_WIDTH, LSTM_WIDTH, GATE_COLS)
_C_XBC_END = SSM_WIDTH + CONV_DIM
_C_DT_END = _C_XBC_END + SSM_HEADS
_C_O_END = _C_DT_END + 4 * LSTM_WIDTH
_C_I_END = _C_O_END + LSTM_HEADS
IN_PROJ_DIM = _C_I_END + LSTM_HEADS
REGROUP_ROWS = 128


def _regroup_kernel(w_ref, o_ref):
    rows = w_ref.shape[0]
    big = _C_XBC_END + 4 * LSTM_WIDTH
    o_ref[:, :_C_XBC_END] = w_ref[:, :_C_XBC_END].astype(BF16)
    o_ref[:, _C_XBC_END:big] = w_ref[:, _C_DT_END:_C_O_END].astype(BF16)
    gate_w = jnp.concatenate([w_ref[:, _C_XBC_END:_C_DT_END], w_ref[:, _C_O_END:IN_PROJ_DIM],
                              jnp.zeros((rows, GATE_COLS - GATE_LIVE), F32)], axis=1)
    o_ref[:, big:] = gate_w.astype(BF16)


def _regroup_weights(w_in, layer):
    k = w_in.shape[1]
    ncol = sum(_INPROJ_WIDTHS)
    return pl.pallas_call(
        _regroup_kernel,
        grid=(k // REGROUP_ROWS,),
        in_specs=[pl.BlockSpec((None, REGROUP_ROWS, IN_PROJ_DIM), lambda i: (layer, i, 0))],
        out_specs=pl.BlockSpec((REGROUP_ROWS, ncol), lambda i: (i, 0)),
        out_shape=jax.ShapeDtypeStruct((k, ncol), BF16),
        compiler_params=_cparams(("arbitrary",)),
        name="regroup_weights",
    )(w_in)


def _inproj_kernel(x_ref, nw_ref, w_ref, z_ref, xbc_ref, q_ref, k_ref, v_ref, o_ref, g_ref):
    xb = _rms(x_ref[...], nw_ref[...]).astype(BF16)
    off = 0
    for ref, width in zip((z_ref, xbc_ref, q_ref, k_ref, v_ref, o_ref, g_ref), _INPROJ_WIDTHS):
        ref[...] = _dot(xb, w_ref[:, off:off + width]).astype(ref.dtype)
        off += width


def _inproj(x2, nw, w_all):
    t = x2.shape[0]
    tm = min(ROW_TILE, t)
    ncol = w_all.shape[1]
    out_shape = [jax.ShapeDtypeStruct((t, w), BF16) for w in _INPROJ_WIDTHS[:-1]]
    out_shape.append(jax.ShapeDtypeStruct((t, GATE_COLS), F32))
    return pl.pallas_call(
        _inproj_kernel,
        grid=(t // tm,),
        in_specs=[
            pl.BlockSpec((tm, D_MODEL), lambda i: (i, 0)),
            pl.BlockSpec((1, D_MODEL), lambda i: (0, 0)),
            pl.BlockSpec((D_MODEL, ncol), lambda i: (0, 0), pipeline_mode=pl.Buffered(1)),
        ],
        out_specs=[pl.BlockSpec((tm, w), lambda i: (i, 0)) for w in _INPROJ_WIDTHS],
        out_shape=out_shape,
        compiler_params=_cparams(("arbitrary",)),
        name="inproj",
    )(x2, nw, w_all)


SSD_PAIR_GROUP = 2
LSTM_HEAD_GROUP = 2


def _ssd_phases(xbc_ref, z_ref, g_ref, convw_ref, convb_ref, dtb_ref, aneg_ref, dskip_ref, nw_ref,
                sel_ref, y_ref, u_scr, st_scr):
    L = xbc_ref.shape[0]
    gw = SSM_WIDTH // SSM_GROUPS

    row = lax.broadcasted_iota(jnp.int32, (L, L), 0)
    col = lax.broadcasted_iota(jnp.int32, (L, L), 1)
    causal = row >= col

    xin_bf = xbc_ref[...]
    xin = xin_bf.astype(F32)
    halo = u_scr[...]
    row8 = lax.broadcasted_iota(jnp.int32, (SUBLANES, 1), 0)
    acc = convb_ref[...] + convw_ref[CONV_WIDTH - 1:CONV_WIDTH, :] * xin
    for back in range(1, CONV_WIDTH):
        shifted = _dot((row - col == back).astype(F32).astype(BF16), xin_bf)
        head = shifted[:SUBLANES] + jnp.where(row8 < back, pltpu.roll(halo, back, axis=0), 0.0)
        shifted = jnp.concatenate([head, shifted[SUBLANES:]], axis=0)
        acc = acc + convw_ref[CONV_WIDTH - 1 - back:CONV_WIDTH - back, :] * shifted
    u_scr[...] = xin[L - SUBLANES:, :]
    xbc = acc * _sigmoid(acc)
    xs = xbc[:, :SSM_WIDTH]
    bm = xbc[:, SSM_WIDTH:SSM_WIDTH + SSM_GROUPS * SSM_STATE].astype(BF16)
    cm = xbc[:, SSM_WIDTH + SSM_GROUPS * SSM_STATE:].astype(BF16)
    yield

    tril_bf = causal.astype(F32).astype(BF16)

    lane = lax.broadcasted_iota(jnp.int32, (1, LANES), 1)
    dt = jnp.where(lane < SSM_HEADS, _softplus(g_ref[...] + dtb_ref[...]), 0.0)
    acs = _sel_dot(tril_bf, dt * aneg_ref[...]) * LOG2E
    acs_t = acs.T
    sel = sel_ref[...]
    acs_last = acs[L - 1:L, :]
    ea_x = _expand(jnp.exp2(acs), sel, terms=2)
    xd = xs * _expand(dt, sel, terms=1)
    xd_bf = xd.astype(BF16)
    xw = (xd * _expand(jnp.exp2(acs_last - acs), sel, terms=1)).astype(BF16)
    yield

    cbs = [_dot_nt(cm[:, g * SSM_STATE:(g + 1) * SSM_STATE], bm[:, g * SSM_STATE:(g + 1) * SSM_STATE])
           for g in range(SSM_GROUPS)]
    yoff = [_dot(cm[:, g * SSM_STATE:(g + 1) * SSM_STATE], st_scr[g].astype(BF16)) for g in range(SSM_GROUPS)]
    upd = [_dot_tn(bm[:, g * SSM_STATE:(g + 1) * SSM_STATE], xw[:, g * gw:(g + 1) * gw])
           for g in range(SSM_GROUPS)]
    for g in range(SSM_GROUPS):
        st_scr[g] = st_scr[g] * ea_x[L - 1:L, g * gw:(g + 1) * gw] + upd[g]
    yield

    pair_lane = lax.broadcasted_iota(jnp.int32, (1, LANES), 1)
    heads_per_group = SSM_HEADS // SSM_GROUPS
    rhs = []
    for j in range(SSM_HEADS // 2):
        xpair = xd_bf[:, j * LANES:(j + 1) * LANES]
        zero = jnp.zeros_like(xpair)
        rhs.append(jnp.concatenate([jnp.where(pair_lane < SSM_HEAD_DIM, xpair, zero),
                                    jnp.where(pair_lane >= SSM_HEAD_DIM, xpair, zero)], axis=0))
    ydiag = []
    for j0 in range(0, SSM_HEADS // 2, SSD_PAIR_GROUP):
        pairs = range(j0, j0 + SSD_PAIR_GROUP)
        ms = {h: (cbs[h // heads_per_group]
                  * jnp.exp2(jnp.where(causal, acs[:, h:h + 1] - acs_t[h:h + 1, :], NEG_INF))).astype(BF16)
              for j in pairs for h in (2 * j, 2 * j + 1)}
        yield
        ydiag += [_dot(jnp.concatenate([ms[2 * j], ms[2 * j + 1]], axis=1), rhs[j]) for j in pairs]
        yield
    y = jnp.concatenate(ydiag, axis=1) + jnp.concatenate(yoff, axis=1) * ea_x + dskip_ref[...] * xs

    zz = z_ref[...].astype(F32)
    y = y * (zz * _sigmoid(zz))
    y_ref[...] = _rms(y, nw_ref[...]).astype(BF16)


def _mlstm_phases(q_ref, k_ref, v_ref, o_ref, g_ref, ib_ref, fb_ref, nw_ref, y_ref,
                  st_scr, m_scr):
    L = q_ref.shape[0]
    dh = LSTM_HEAD_DIM
    scale = dh ** -0.5

    row = lax.broadcasted_iota(jnp.int32, (L, L), 0)
    col = lax.broadcasted_iota(jnp.int32, (L, L), 1)
    causal = row >= col
    tril_bf = causal.astype(F32).astype(BF16)
    lane = lax.broadcasted_iota(jnp.int32, (1, LANES), 1)
    live = lane < LSTM_HEADS

    gates = g_ref[...]
    gi = pltpu.roll(gates, LANES - SSM_HEADS, axis=1)
    gf = pltpu.roll(gates, LANES - SSM_HEADS - LSTM_HEADS, axis=1)
    ii = jnp.where(live, gi + ib_ref[...], 0.0)
    logf = jnp.where(live, -_softplus(-(gf + fb_ref[...])), 0.0)
    cumf = _sel_dot(tril_bf, logf)
    g = ii - cumf
    rid = lax.broadcasted_iota(jnp.int32, (L, LANES), 0)
    cmx = g
    step = 1
    while step < L:
        cmx = jnp.maximum(cmx, jnp.where(rid >= step, pltpu.roll(cmx, step, axis=0), NEG_INF))
        step *= 2
    m_prev = m_scr[...]
    mx = jnp.maximum(m_prev, cmx)
    w_inter = jnp.exp(m_prev - mx)
    enm = jnp.exp(-(cumf + mx))
    g2_t = (g * LOG2E).T
    mx2 = mx * LOG2E - np.log2(scale)
    m_last = mx[L - 1:L, :]
    wk = jnp.exp(g - m_last) * scale
    sc = jnp.exp(m_prev - m_last)
    m_scr[...] = cumf[L - 1:L, :] + m_last
    yield

    ones_bf = jnp.ones((L, dh), BF16)
    heads = range(LSTM_HEADS)
    hsl = [slice(h * dh, (h + 1) * dh) for h in heads]
    vaug = [jnp.concatenate([v_ref[:, hsl[h]], ones_bf], axis=1) for h in heads]
    r2 = [_dot(q_ref[:, hsl[h]], st_scr[h].astype(BF16)) for h in heads]
    upd = [_dot_tn((k_ref[:, hsl[h]].astype(F32) * wk[:, h:h + 1]).astype(BF16), vaug[h]) for h in heads]
    for h in heads:
        st_scr[h] = st_scr[h] * sc[:, h:h + 1] + upd[h]
    yield
    r1 = []
    for h0 in range(0, LSTM_HEADS, LSTM_HEAD_GROUP):
        group = range(h0, h0 + LSTM_HEAD_GROUP)
        s_qk = {h: _dot_nt(q_ref[:, hsl[h]], k_ref[:, hsl[h]]) for h in group}
        yield
        p = {h: (s_qk[h] * jnp.exp2(jnp.where(causal, g2_t[h:h + 1, :] - mx2[:, h:h + 1], NEG_INF))).astype(BF16)
             for h in group}
        yield
        r1 += [_dot(p[h], vaug[h]) for h in group]
        yield
    hh = []
    for h in heads:
        wcol = w_inter[:, h:h + 1]
        num = r1[h][:, :dh] + r2[h][:, :dh] * wcol
        den = r1[h][:, dh:] + r2[h][:, dh:] * wcol
        hh.append(num / jnp.maximum(jnp.abs(den), enm[:, h:h + 1]))
    inv = [lax.rsqrt(jnp.mean(hh[h] * hh[h], axis=-1, keepdims=True) + RMS_EPS) for h in heads]
    for h in heads:
        oo = o_ref[:, hsl[h]].astype(F32)
        y_ref[:, hsl[h]] = (_sigmoid(oo) * (hh[h] * inv[h] * nw_ref[:, hsl[h]])).astype(BF16)


N_SSD_IN, N_LSTM_IN = 10, 8


def _mixer_kernel(*refs):
    ssd_in = refs[:N_SSD_IN]
    lstm_in = refs[N_SSD_IN:N_SSD_IN + N_LSTM_IN]
    y_ssd, y_lstm, u_scr, st_ssd, st_lstm, m_scr = refs[N_SSD_IN + N_LSTM_IN:]

    @pl.when(pl.program_id(1) == 0)
    def _():
        u_scr[...] = jnp.zeros_like(u_scr)
        st_ssd[...] = jnp.zeros_like(st_ssd)
        st_lstm[...] = jnp.zeros_like(st_lstm)
        m_scr[...] = jnp.zeros_like(m_scr)

    streams = [_ssd_phases(*ssd_in, y_ssd, u_scr, st_ssd), _mlstm_phases(*lstm_in, y_lstm, st_lstm, m_scr)]
    while streams:
        for stream in list(streams):
            if next(stream, StopIteration) is StopIteration:
                streams.remove(stream)


def _mixers(xbc, z, gates, convw, convb, dtb, aneg, dskip_x, nw_ssd, sel, q, k, v, o, ib, fb, nw_lstm, b, s):
    L = min(MIX_CHUNK, s)
    nc = s // L
    tok = lambda bi, ci: (bi * nc + ci, 0)
    const = lambda bi, ci: (0, 0)
    big = pl.BlockSpec((L, LSTM_WIDTH), tok)
    gate_block = lambda j: pl.BlockSpec((L, LANES), lambda bi, ci: (bi * nc + ci, j))
    vec = lambda w: pl.BlockSpec((1, w), const)
    in_specs = [pl.BlockSpec((L, CONV_DIM), tok), pl.BlockSpec((L, SSM_WIDTH), tok), gate_block(0),
                pl.BlockSpec((CONV_WIDTH, CONV_DIM), const), vec(CONV_DIM), vec(LANES), vec(LANES),
                vec(SSM_WIDTH), vec(SSM_WIDTH), pl.BlockSpec((LANES, SSM_WIDTH), const),
                big, big, big, big, gate_block(0), vec(LANES), vec(LANES), vec(LSTM_WIDTH)]
    assert len(in_specs) == N_SSD_IN + N_LSTM_IN
    return pl.pallas_call(
        _mixer_kernel,
        grid=(b, nc),
        in_specs=in_specs,
        out_specs=[pl.BlockSpec((L, SSM_WIDTH), tok), big],
        out_shape=[jax.ShapeDtypeStruct((b * s, SSM_WIDTH), BF16), jax.ShapeDtypeStruct((b * s, LSTM_WIDTH), BF16)],
        scratch_shapes=[pltpu.VMEM((SUBLANES, CONV_DIM), F32),
                        pltpu.VMEM((SSM_GROUPS, SSM_STATE, SSM_WIDTH // SSM_GROUPS), F32),
                        pltpu.VMEM((LSTM_HEADS, LSTM_HEAD_DIM, 2 * LSTM_HEAD_DIM), F32),
                        pltpu.VMEM((1, LANES), F32)],
        compiler_params=_cparams(("arbitrary", "arbitrary")),
        name="mixers",
    )(xbc, z, gates, convw, convb, dtb, aneg, dskip_x, nw_ssd, sel, q, k, v, o, gates, ib, fb, nw_lstm)


META_IDX, META_GATE, META_RANK = 0, TOP_K, 2 * TOP_K
META_ROWS = 2 * SUBLANES


def _outproj_kernel(ys_ref, yl_ref, x_ref, wo_ref, nw_ref, wr_ref, br_ref, h_ref, hn_ref, meta_ref,
                    meta_t_ref, cnt_ref, cnt_scr, logit_scr):
    tm = x_ref.shape[0]
    step = pl.program_id(0)

    @pl.when(step == 0)
    def _():
        cnt_scr[...] = jnp.zeros_like(cnt_scr)
        logit_scr[...] = jnp.zeros_like(logit_scr)

    vals = logit_scr[...]
    routed = (step > 0).astype(F32)

    h = (x_ref[...] + _dot(ys_ref[...], wo_ref[:SSM_WIDTH, :]) + _dot(yl_ref[...], wo_ref[SSM_WIDTH:, :]))
    h_ref[...] = h
    hn = _rms(h, nw_ref[...])
    _store_row_tiles(hn_ref, hn)
    logit_scr[...] = _dot(hn.astype(BF16), wr_ref[...]) + br_ref[...]

    lane = lax.broadcasted_iota(jnp.int32, (tm, LANES), 1)
    member = jnp.zeros((tm, LANES), F32)
    tops, idxs, sels = [], [], []
    for _ in range(TOP_K):
        m = jnp.max(vals, axis=-1, keepdims=True)
        idx = jnp.min(jnp.where(vals == m, lane, LANES), axis=-1, keepdims=True)
        sel = lane == idx
        vals = jnp.where(sel, NEG_INF, vals)
        member = member + sel.astype(F32)
        tops.append(m)
        idxs.append(idx)
        sels.append(sel)
    es = [jnp.exp(t - tops[0]) for t in tops]
    inv = 1.0 / (es[0] + es[1] + es[2] + es[3])
    member = member * routed

    r = lax.broadcasted_iota(jnp.int32, (tm, tm), 0)
    c = lax.broadcasted_iota(jnp.int32, (tm, tm), 1)
    strict = (r > c).astype(F32).astype(BF16)
    carry = cnt_scr[0:1, :]
    rank_all = _dot(strict, member.astype(BF16)) + carry
    total = carry + jnp.sum(member, axis=0, keepdims=True)
    cnt_scr[...] = jnp.broadcast_to(total, cnt_scr.shape)
    cnt_ref[...] = jnp.broadcast_to(total, cnt_ref.shape)

    meta = jnp.zeros((tm, LANES), F32)
    for kk in range(TOP_K):
        rank = jnp.sum(jnp.where(sels[kk], rank_all, 0.0), axis=-1, keepdims=True)
        meta = jnp.where(lane == META_IDX + kk, idxs[kk].astype(F32), meta)
        meta = jnp.where(lane == META_GATE + kk, es[kk] * inv, meta)
        meta = jnp.where(lane == META_RANK + kk, rank, meta)
    meta_ref[...] = meta
    eye = (lax.broadcasted_iota(jnp.int32, (META_ROWS, LANES), 0)
           == lax.broadcasted_iota(jnp.int32, (META_ROWS, LANES), 1)).astype(F32).astype(BF16)
    pieces = _split3(meta)
    meta_t_ref[...] = _dot_nt(eye, pieces[0]) + _dot_nt(eye, pieces[1]) + _dot_nt(eye, pieces[2])


def _outproj(ys, yl, x2, wo, nw, wr, br):
    t = x2.shape[0]
    tm = min(ROW_TILE, t)
    n = t // tm
    projected = lambda i: (jnp.minimum(i, n - 1), 0)
    routed = lambda i: (jnp.maximum(i - 1, 0), 0)
    tokspec = lambda w: pl.BlockSpec((tm, w), projected)
    const = lambda i: (0, 0)
    return pl.pallas_call(
        _outproj_kernel,
        grid=(n + 1,),
        in_specs=[tokspec(SSM_WIDTH), tokspec(LSTM_WIDTH), tokspec(D_MODEL),
                  pl.BlockSpec((SSM_WIDTH + LSTM_WIDTH, D_MODEL), const, pipeline_mode=pl.Buffered(1)),
                  pl.BlockSpec((1, D_MODEL), const),
                  pl.BlockSpec((D_MODEL, LANES), const),
                  pl.BlockSpec((1, LANES), const)],
        out_specs=[tokspec(D_MODEL), pl.BlockSpec((tm * ROW_PITCH, LANES), projected),
                   pl.BlockSpec((tm, LANES), routed),
                   pl.BlockSpec((META_ROWS, tm), lambda i: (0, jnp.maximum(i - 1, 0))),
                   pl.BlockSpec((SUBLANES, LANES), const)],
        out_shape=[jax.ShapeDtypeStruct((t, D_MODEL), F32), jax.ShapeDtypeStruct((t * ROW_PITCH, LANES), F32),
                   jax.ShapeDtypeStruct((t, LANES), F32), jax.ShapeDtypeStruct((META_ROWS, t), F32),
                   jax.ShapeDtypeStruct((SUBLANES, LANES), F32)],
        scratch_shapes=[pltpu.VMEM((SUBLANES, LANES), F32), pltpu.VMEM((tm, LANES), F32)],
        compiler_params=_cparams(("arbitrary",)),
        name="outproj_router",
    )(ys, yl, x2, wo, nw, wr, br)


DISPATCH_TILE = 1024
DISPATCH_SLOTS = 3
DISPATCH_GROUP = 4
PAD_CHUNKS = (128, 64, 32, 16, 8, 4, 2, 1)


def _dispatch_kernel(padstart_ref, padlen_ref, misc_ref, pos_ref, hn_hbm, xs_hbm, sem, zsem, zeros_scr,
                     tile_buf, tsem, *, n_steps):
    i = pl.program_id(0)
    td = pos_ref.shape[-1] // TOP_K
    tile_rows = td * ROW_PITCH

    def tile_copy(j, slot):
        return pltpu.make_async_copy(hn_hbm.at[pl.ds(j * tile_rows, tile_rows), :], tile_buf.at[slot],
                                     tsem.at[slot])

    def drain_rows(slot):
        for _ in range(TOP_K):
            _wait_row_copies(tile_buf.at[slot], xs_hbm, td, sem.at[slot], ROW_PITCH)

    @pl.when(i == 0)
    def _():
        tile_copy(0, 0).start()

    @pl.when(i == 0)
    def _():
        zeros_scr[...] = jnp.zeros_like(zeros_scr)
        for e in range(N_EXPERTS):
            row = padstart_ref[e]
            nrow = padlen_ref[e]
            for chunk in PAD_CHUNKS:
                @pl.when((nrow & chunk) != 0)
                def _(row=row, chunk=chunk):
                    pltpu.make_async_copy(
                        zeros_scr.at[pl.ds(0, chunk * ROW_PITCH), :],
                        xs_hbm.at[pl.ds(row * ROW_PITCH, chunk * ROW_PITCH), :],
                        zsem).start()
                row = row + (nrow & chunk)

        def tail_copy(blk):
            return pltpu.make_async_copy(
                zeros_scr,
                xs_hbm.at[pl.ds(blk * (MOE_BLOCK * ROW_PITCH), MOE_BLOCK * ROW_PITCH), :], zsem)

        n_used, n_blocks = misc_ref[1], misc_ref[2]

        def start_tail(blk, carry):
            tail_copy(blk).start()
            return carry
        lax.fori_loop(n_used, n_blocks, start_tail, 0)

        def drain_row(j, carry):
            _wait_row_copies(zeros_scr, xs_hbm, 1, zsem, ROW_PITCH)
            return carry
        lax.fori_loop(0, misc_ref[0], drain_row, 0)

        def drain_tail(blk, carry):
            tail_copy(blk).wait()
            return carry
        lax.fori_loop(n_used, n_blocks, drain_tail, 0)

    cur = i % DISPATCH_SLOTS
    nxt = (i + 1) % DISPATCH_SLOTS

    @pl.when(i >= DISPATCH_SLOTS - 1)
    def _():
        drain_rows(nxt)

    @pl.when(i + 1 < n_steps)
    def _():
        tile_copy(i + 1, nxt).start()

    tile_copy(i, cur).wait()
    src = tile_buf.at[cur]

    def body(g, carry):
        r0 = g * DISPATCH_GROUP
        slots = [pos_ref[0, 0, (j % TOP_K) * td + r0 + j // TOP_K] for j in range(DISPATCH_GROUP * TOP_K)]
        for j, slot in enumerate(slots):
            _row_copy(src, r0 + j // TOP_K, xs_hbm, slot, sem.at[cur],
                      ROW_PITCH).start(priority=j % DMA_PRIORITIES)
        return carry
    lax.fori_loop(0, td // DISPATCH_GROUP, body, 0)

    @pl.when(i == n_steps - 1)
    def _():
        if n_steps >= 2:
            drain_rows((i + DISPATCH_SLOTS - 1) % DISPATCH_SLOTS)
        drain_rows(cur)


def _tile_slots(pos, tile):
    t = pos.shape[1]
    return pos.reshape(TOP_K, t // tile, tile).transpose(1, 0, 2).reshape(t // tile, 1, TOP_K * tile)


def _dispatch(pad_start, pad_len, misc, pos, hn_rt, n_rows):
    t = pos.shape[1]
    td = min(DISPATCH_TILE, t)
    n = t // td
    grid_spec = pltpu.PrefetchScalarGridSpec(
        num_scalar_prefetch=3,
        grid=(n,),
        in_specs=[pl.BlockSpec((1, 1, TOP_K * td), lambda i, *_: (i, 0, 0), memory_space=pltpu.SMEM),
                  pl.BlockSpec(memory_space=pl.ANY)],
        out_specs=pl.BlockSpec(memory_space=pl.ANY),
        scratch_shapes=[pltpu.SemaphoreType.DMA((DISPATCH_SLOTS,)), pltpu.SemaphoreType.DMA(()),
                        pltpu.VMEM((MOE_BLOCK * ROW_PITCH, LANES), F32),
                        pltpu.VMEM((DISPATCH_SLOTS, td * ROW_PITCH, LANES), F32),
                        pltpu.SemaphoreType.DMA((DISPATCH_SLOTS,))],
    )
    return pl.pallas_call(
        functools.partial(_dispatch_kernel, n_steps=n),
        grid_spec=grid_spec,
        out_shape=jax.ShapeDtypeStruct((n_rows * ROW_PITCH, LANES), F32),
        compiler_params=_cparams(("arbitrary",)),
        name="dispatch",
    )(pad_start, pad_len, misc, _tile_slots(pos, td), hn_rt)


def _experts_kernel(be_ref, nused_ref, next_ref, x_ref, wgu_hbm, bgu_ref, wdn_hbm, bdn_ref, out_ref,
                    wgu_bf, wdn_bf, wgu_st, wdn_st, wsem, slot_ref):
    i = pl.program_id(0)
    n_used = nused_ref[0]

    def weight_copies(e, slot):
        return (pltpu.make_async_copy(wgu_hbm.at[e], wgu_st.at[slot], wsem.at[0, slot]),
                pltpu.make_async_copy(wdn_hbm.at[e], wdn_st.at[slot], wsem.at[1, slot]))

    @pl.when(i == 0)
    def _():
        slot_ref[0] = 0
        for cp in weight_copies(be_ref[0], 0):
            cp.start()

    @pl.when(i < n_used)
    def _():
        @pl.when(jnp.logical_or(i == 0, be_ref[i] != be_ref[jnp.maximum(i - 1, 0)]))
        def _():
            e = be_ref[i]
            slot = slot_ref[0]
            for cp in weight_copies(e, slot):
                cp.wait()
            wgu_bf[...] = wgu_st[slot].astype(BF16)
            wdn_bf[...] = wdn_st[slot].astype(BF16)
            nxt = next_ref[e]

            @pl.when(nxt >= 0)
            def _():
                for cp in weight_copies(nxt, 1 - slot):
                    cp.start()
            slot_ref[0] = 1 - slot

        xb = jnp.concatenate(_load_row_tile_cols(x_ref, MOE_BLOCK), axis=1).astype(BF16)
        hh = _dot(xb, wgu_bf[...]) + bgu_ref[...]
        gh = jnp.minimum(hh[:, :D_FF], SWIGLU_LIMIT)
        uh = jnp.clip(hh[:, D_FF:], -SWIGLU_LIMIT, SWIGLU_LIMIT)
        act = (uh + 1.0) * (gh * _sigmoid(SWIGLU_ALPHA * gh))
        _store_row_tiles(out_ref, _dot(act.astype(BF16), wdn_bf[...]) + bdn_ref[...])

    @pl.when(i >= n_used)
    def _():
        out_ref[...] = jnp.zeros_like(out_ref)


def _experts(block_expert, n_used, next_expert, xs_rt, wgu, bgu3, wdn, bdn3):
    n_blocks = block_expert.shape[0]
    blk = MOE_BLOCK * ROW_PITCH
    grid_spec = pltpu.PrefetchScalarGridSpec(
        num_scalar_prefetch=3,
        grid=(n_blocks,),
        in_specs=[
            pl.BlockSpec((blk, LANES), lambda i, be, nu, nx: (i, 0)),
            pl.BlockSpec(memory_space=pl.ANY),
            pl.BlockSpec((None, 1, 2 * D_FF), lambda i, be, nu, nx: (be[i], 0, 0)),
            pl.BlockSpec(memory_space=pl.ANY),
            pl.BlockSpec((None, 1, D_MODEL), lambda i, be, nu, nx: (be[i], 0, 0)),
        ],
        out_specs=pl.BlockSpec((blk, LANES), lambda i, be, nu, nx: (i, 0)),
        scratch_shapes=[pltpu.VMEM((D_MODEL, 2 * D_FF), BF16),
                        pltpu.VMEM((D_FF, D_MODEL), BF16),
                        pltpu.VMEM((2, D_MODEL, 2 * D_FF), F32),
                        pltpu.VMEM((2, D_FF, D_MODEL), F32),
                        pltpu.SemaphoreType.DMA((2, 2)),
                        pltpu.SMEM((1,), jnp.int32)],
    )
    return pl.pallas_call(
        _experts_kernel,
        grid_spec=grid_spec,
        out_shape=jax.ShapeDtypeStruct((n_blocks * blk, LANES), F32),
        compiler_params=_cparams(("arbitrary",)),
        name="experts",
    )(block_expert, n_used, next_expert, xs_rt, wgu, bgu3, wdn, bdn3)


def _combine_kernel(pos_cur, pos_next, outs_hbm, h_ref, meta_ref, nw_ref, y_ref, buf, sem):
    i = pl.program_id(0)
    n = pl.num_programs(0)
    tc = h_ref.shape[0]
    slot = i % 2

    def issue(pos_ref, s):
        def body(r, carry):
            for kk in range(TOP_K):
                _row_copy(outs_hbm, pos_ref[0, 0, kk * tc + r], buf.at[s, kk], r,
                          sem.at[s]).start(priority=kk % DMA_PRIORITIES)
            return carry
        lax.fori_loop(0, tc, body, 0, unroll=4)

    @pl.when(i == 0)
    def _():
        issue(pos_cur, 0)

    @pl.when(i + 1 < n)
    def _():
        issue(pos_next, 1 - slot)

    for kk in range(TOP_K):
        _wait_row_copies(outs_hbm, buf.at[slot, kk], tc, sem.at[slot])
    meta = meta_ref[...]
    gates = [meta[:, META_GATE + kk:META_GATE + kk + 1] for kk in range(TOP_K)]
    rows = [_load_row_tile_cols(buf.at[slot, kk], tc) for kk in range(TOP_K)]
    cols = []
    for s in range(ROW_SUBTILES):
        acc = h_ref[:, s * LANES:(s + 1) * LANES]
        for kk in range(TOP_K):
            acc = acc + gates[kk] * rows[kk][s]
        cols.append(acc)
    y_ref[...] = _rms(jnp.concatenate(cols, axis=1), nw_ref[...])


def _combine(pos, outs_rt, h, meta, nw):
    t = h.shape[0]
    tc = min(COMBINE_TILE, t)
    n = t // tc
    pos3 = _tile_slots(pos, tc)
    return pl.pallas_call(
        _combine_kernel,
        grid=(n,),
        in_specs=[
            pl.BlockSpec((1, 1, TOP_K * tc), lambda i: (i, 0, 0), memory_space=pltpu.SMEM),
            pl.BlockSpec((1, 1, TOP_K * tc), lambda i: (jnp.minimum(i + 1, n - 1), 0, 0),
                         memory_space=pltpu.SMEM),
            pl.BlockSpec(memory_space=pl.ANY),
            pl.BlockSpec((tc, D_MODEL), lambda i: (i, 0)),
            pl.BlockSpec((tc, LANES), lambda i: (i, 0)),
            pl.BlockSpec((1, D_MODEL), lambda i: (0, 0)),
        ],
        out_specs=pl.BlockSpec((tc, D_MODEL), lambda i: (i, 0)),
        out_shape=jax.ShapeDtypeStruct((t, D_MODEL), F32),
        scratch_shapes=[pltpu.VMEM((2, TOP_K, tc * ROW_PITCH, LANES), F32), pltpu.SemaphoreType.DMA((2,))],
        compiler_params=_cparams(("arbitrary",)),
        name="combine",
    )(pos3, pos3, outs_rt, h, meta, nw)


def _pad_lanes(v, fill=0.0):
    v = v.astype(F32).reshape(1, -1)
    return jnp.pad(v, ((0, 0), (0, LANES - v.shape[1])), constant_values=fill)


def kernel(x, norm_mix_w, w_in, conv_w, conv_b, dt_bias, a_log, d_skip, ssm_norm_w, lstm_i_bias,
           lstm_f_bias, lstm_norm_w, w_out, norm_ffn_w, w_router, b_router, w_gate_up, b_gate_up,
           w_down, b_down, norm_final_w):
    b, s, d = x.shape
    t = b * s
    x2 = x.reshape(t, d).astype(F32)
    depth = w_in.shape[0]
    assert depth == 1, "the combine kernel fuses the final norm, so exactly one layer is supported"
    for layer in range(depth):
        w_all = _regroup_weights(w_in.astype(F32), layer)
        sel = (jnp.arange(LANES)[:, None] == (jnp.arange(SSM_WIDTH) // SSM_HEAD_DIM)[None, :]).astype(BF16)
        a_neg = _pad_lanes(-jnp.exp(a_log[layer].astype(F32)))
        dskip_x = jnp.repeat(d_skip[layer].astype(F32), SSM_HEAD_DIM).reshape(1, SSM_WIDTH)
        wr = jnp.pad(w_router[layer], ((0, 0), (0, LANES - N_EXPERTS))).astype(BF16)
        br = _pad_lanes(b_router[layer], fill=NEG_INF)

        z, xbc, q, k, v, o, gates = _inproj(x2, norm_mix_w[layer].reshape(1, d).astype(F32), w_all)
        y_ssd, y_lstm = _mixers(
            xbc, z, gates, conv_w[layer].astype(F32), conv_b[layer].reshape(1, -1).astype(F32),
            _pad_lanes(dt_bias[layer]), a_neg, dskip_x, ssm_norm_w[layer].reshape(1, -1).astype(F32), sel,
            q, k, v, o, _pad_lanes(lstm_i_bias[layer]), _pad_lanes(lstm_f_bias[layer]),
            lstm_norm_w[layer].reshape(1, -1).astype(F32), b, s)
        h, hn_rt, meta, meta_t, cnt = _outproj(y_ssd, y_lstm, x2, w_out[layer].astype(BF16),
                                               norm_ffn_w[layer].reshape(1, d).astype(F32), wr, br)

        idx = meta_t[META_IDX:META_IDX + TOP_K].astype(jnp.int32)
        rank = meta_t[META_RANK:META_RANK + TOP_K].astype(jnp.int32)
        counts = cnt[0, :N_EXPERTS].astype(jnp.int32)
        n_blocks = -(-(t * TOP_K) // MOE_BLOCK) + N_EXPERTS
        padded = (counts + MOE_BLOCK - 1) // MOE_BLOCK * MOE_BLOCK
        padded_ends = jnp.cumsum(padded)
        padded_starts = padded_ends - padded
        onehot = idx[..., None] == jnp.arange(N_EXPERTS, dtype=jnp.int32)
        pos = jnp.sum(jnp.where(onehot, padded_starts, 0), axis=-1) + rank
        block_start = jnp.arange(n_blocks, dtype=jnp.int32) * MOE_BLOCK
        block_expert = jnp.minimum(
            jnp.sum((padded_ends[None, :] <= block_start[:, None]).astype(jnp.int32), axis=1), N_EXPERTS - 1)
        n_used = padded_ends[-1:] // MOE_BLOCK
        pad_len = padded - counts
        misc = jnp.concatenate([jnp.sum(pad_len, keepdims=True), n_used,
                                jnp.full((1,), n_blocks, jnp.int32)])

        xs_rt = _dispatch(padded_starts + counts, pad_len, misc, pos, hn_rt, n_blocks * MOE_BLOCK)
        eids = jnp.arange(N_EXPERTS, dtype=jnp.int32)
        later_nonempty = (eids[None, :] > eids[:, None]) & (counts[None, :] > 0)
        next_expert = jnp.min(jnp.where(later_nonempty, eids[None, :], N_EXPERTS), axis=1)
        next_expert = jnp.where(next_expert < N_EXPERTS, next_expert, -1)
        outs_rt = _experts(block_expert, n_used, next_expert, xs_rt, w_gate_up[layer],
                           b_gate_up[layer].reshape(N_EXPERTS, 1, -1), w_down[layer],
                           b_down[layer].reshape(N_EXPERTS, 1, -1))
        x2 = _combine(pos, outs_rt, h, meta, norm_final_w.reshape(1, d).astype(F32))
    return x2.reshape(b, s, d).astype(x.dtype)
```
